```python
import jax, jax.numpy as jnp
from jax import lax
import numpy as np

D_MODEL = 4096
BATCH = 4
SEQ = 2048
DEPTH = 2
DEC_BATCH = 32
DEC_SEQ = 1
PAST_LEN = 16384
PAGE_SIZE = 128

N_A_LAYERS = DEPTH // 2
N_B_LAYERS = DEPTH - N_A_LAYERS
M_HEADS = 8
M_DK = D_MODEL // (2 * M_HEADS)
M_DV = D_MODEL // M_HEADS
M_CHUNK = 128
GATE_SOFT_CAP = 15.0
A_HEADS = 32
A_KV_HEADS = 8
A_GROUP = A_HEADS // A_KV_HEADS
A_HEAD_DIM = D_MODEL // A_HEADS
WINDOW = 128
ROPE_THETA = 10000.0
NEG_INF = -1e30
D_FF = 11008
CONV_W = 3
RMS_EPS = 1e-6

kernel_name = 'yoco_mlstm_swa_sink_convffn_step'


def rms_norm(x, g):
    xf = x.astype(jnp.float32)
    r = lax.rsqrt(jnp.mean(xf * xf, axis=-1, keepdims=True) + RMS_EPS)
    return (xf * r * g.astype(jnp.float32)).astype(x.dtype)


def rope(x, pos):
    half = A_HEAD_DIM // 2
    inv_freq = ROPE_THETA ** (-jnp.arange(0, A_HEAD_DIM, 2, dtype=jnp.float32) / A_HEAD_DIM)
    ang = pos.astype(jnp.float32)[:, None] * inv_freq[None, :]
    shape = (1, pos.shape[0]) + (1,) * (x.ndim - 3) + (half,)
    cos, sin = jnp.cos(ang).reshape(shape), jnp.sin(ang).reshape(shape)
    x1 = x[..., :half].astype(jnp.float32)
    x2 = x[..., half:].astype(jnp.float32)
    return jnp.concatenate([x1 * cos - x2 * sin, x2 * cos + x1 * sin], axis=-1).astype(x.dtype)


def mlstm_chunk(carry, inp):
    C, n, m = carry
    q, k, v, ig, lf = inp
    L = q.shape[1]
    b = jnp.cumsum(lf, axis=1)
    causal = jnp.tril(jnp.ones((L, L), dtype=bool))
    dlog = jnp.where(causal[None, :, :, None],
                     b[:, :, None, :] - b[:, None, :, :] + ig[:, None, :, :], -jnp.inf)
    g = b + m[:, None, :]
    m_t = jnp.maximum(g, jnp.max(dlog, axis=2))
    w_intra = jnp.exp(dlog - m_t[:, :, None, :])
    w_inter = jnp.exp(g - m_t)
    s = jnp.einsum('bthd,bshd->btsh', q, k) * w_intra
    num = jnp.einsum('btsh,bshv->bthv', s, v) + w_inter[..., None] * jnp.einsum('bthd,bhdv->bthv', q, C)
    den = jnp.sum(s, axis=2) + w_inter * jnp.einsum('bthd,bhd->bth', q, n)
    h = num / jnp.maximum(jnp.abs(den), jnp.exp(-m_t))[..., None]
    w_end = w_intra[:, -1]
    decay = w_inter[:, -1]
    C_new = decay[:, :, None, None] * C + jnp.einsum('bsh,bshd,bshv->bhdv', w_end, k, v)
    n_new = decay[:, :, None] * n + jnp.einsum('bsh,bshd->bhd', w_end, k)
    return (C_new, n_new, m_t[:, -1]), h


def mlstm_mixer(xn, w_in, b_gates, w_out, C0, n0, m0):
    b, s, _ = xn.shape
    f32 = jnp.float32
    qk = M_HEADS * M_DK
    vw = M_HEADS * M_DV
    p = xn @ w_in
    q = p[..., :qk].reshape(b, s, M_HEADS, M_DK).astype(f32) * (M_DK ** -0.5)
    k = p[..., qk:2 * qk].reshape(b, s, M_HEADS, M_DK).astype(f32)
    v = p[..., 2 * qk:2 * qk + vw].reshape(b, s, M_HEADS, M_DV).astype(f32)
    o = jax.nn.sigmoid(p[..., 2 * qk + vw:2 * qk + 2 * vw].astype(f32))
    gates = p[..., 2 * qk + 2 * vw:].astype(f32) + b_gates.astype(f32)
    ig = GATE_SOFT_CAP * jnp.tanh(gates[..., :M_HEADS] / GATE_SOFT_CAP)
    lf = jax.nn.log_sigmoid(gates[..., M_HEADS:])
    L = M_CHUNK if s % M_CHUNK == 0 else s
    nc = s // L

    def chunks(a):
        return jnp.moveaxis(a.reshape((b, nc, L) + a.shape[2:]), 1, 0)

    carry0 = (C0.astype(f32), n0.astype(f32), m0.astype(f32))
    (C, n, m), hs = lax.scan(mlstm_chunk, carry0,
                             (chunks(q), chunks(k), chunks(v), chunks(ig), chunks(lf)))
    h = jnp.moveaxis(hs, 0, 1).reshape(b, s, vw)
    y = (o * h).astype(xn.dtype) @ w_out
    return y, (C, n, m)


def sink_attention(q, k, v, mask, sinks):
    scores = jnp.einsum('bnqhgd,bnkhd->bnhgqk', q, k).astype(jnp.float32) * (A_HEAD_DIM ** -0.5)
    scores = jnp.where(mask[None, :, None, None], scores, NEG_INF)
    sink = sinks.astype(jnp.float32).reshape(1, 1, A_KV_HEADS, A_GROUP, 1, 1)
    mx = jnp.maximum(jnp.max(scores, axis=-1, keepdims=True), sink)
    e = jnp.exp(scores - mx)
    p = e / (jnp.sum(e, axis=-1, keepdims=True) + jnp.exp(sink - mx))
    return jnp.einsum('bnhgqk,bnkhd->bnqhgd', p.astype(v.dtype), v)


def banded_window_attention(q, k, v, sinks):
    b, s = q.shape[0], q.shape[1]
    nb = s // WINDOW
    qb = q.reshape(b, nb, WINDOW, A_KV_HEADS, A_GROUP, A_HEAD_DIM)

    def band(t):
        tp = jnp.concatenate([jnp.zeros_like(t[:, :WINDOW]), t], axis=1)
        tp = tp.reshape(b, nb + 1, WINDOW, A_KV_HEADS, A_HEAD_DIM)
        return jnp.concatenate([tp[:, :-1], tp[:, 1:]], axis=2)

    kb, vb = band(k), band(v)
    i = jnp.arange(WINDOW)[:, None]
    j = jnp.arange(2 * WINDOW)[None, :]
    rel = i + WINDOW - j
    blk = jnp.arange(nb)[:, None, None]
    mask = (rel >= 0) & (rel <= WINDOW) & (blk * WINDOW - WINDOW + j >= 0)
    o = sink_attention(qb, kb, vb, mask, sinks)
    return o.reshape(b, s, A_KV_HEADS, A_GROUP, A_HEAD_DIM)


def cached_window_attention(q, k_all, v_all, sinks):
    ds, nk = q.shape[1], k_all.shape[1]
    i = jnp.arange(ds)[:, None]
    j = jnp.arange(nk)[None, :]
    rel = i + WINDOW - j
    mask = ((rel >= 0) & (rel <= WINDOW))[None]
    o = sink_attention(q[:, None], k_all[:, None], v_all[:, None], mask, sinks)
    return o[:, 0]


def conv_ffn(xn, w_up, conv_w, conv_b, w_down, buf):
    s = xn.shape[1]
    u = xn @ w_up
    gate, val = u[..., :D_FF], u[..., D_FF:]
    full = jnp.concatenate([buf.astype(gate.dtype), gate], axis=1)
    conv = conv_b
    for j in range(CONV_W):
        conv = conv + conv_w[j] * full[:, j:j + s]
    y = (jax.nn.silu(conv) * val) @ w_down
    return y, full[:, s:]


def setup_inputs(seed: int = 0) -> dict:
    key = jax.random.key(seed)
    ks = jax.random.split(key, 24)
    f32 = jnp.float32

    def nrm(k, shape, scale):
        return jax.random.normal(k, shape, f32) * scale

    m_in = 2 * M_HEADS * M_DK + 2 * M_HEADS * M_DV + 2 * M_HEADS
    kv_w = A_KV_HEADS * A_HEAD_DIM
    b_i = nrm(ks[0], (N_A_LAYERS, M_HEADS), 0.1)
    b_f = 3.0 + 3.0 * jax.random.uniform(ks[1], (N_A_LAYERS, M_HEADS), f32)
    return {
        'x_prompt': nrm(ks[2], (BATCH, SEQ, D_MODEL), 1.0),
        'x_sample': nrm(ks[3], (DEC_BATCH, DEC_SEQ, D_MODEL), 1.0),
        'state_mlstm_C': nrm(ks[4], (N_A_LAYERS, DEC_BATCH, M_HEADS, M_DK, M_DV), 1.0),
        'state_mlstm_n': nrm(ks[5], (N_A_LAYERS, DEC_BATCH, M_HEADS, M_DK), 1.0),
        'state_mlstm_m': nrm(ks[6], (N_A_LAYERS, DEC_BATCH, M_HEADS), 1.0),
        'cache_k_win': nrm(ks[7], (DEC_BATCH, WINDOW, A_KV_HEADS, A_HEAD_DIM), 1.0),
        'cache_v_win': nrm(ks[8], (DEC_BATCH, WINDOW, A_KV_HEADS, A_HEAD_DIM), 1.0),
        'state_ffn_conv': nrm(ks[9], (DEPTH, DEC_BATCH, CONV_W - 1, D_FF), 1.0),
        'g_mix': 1.0 + nrm(ks[10], (DEPTH, D_MODEL), 0.02),
        'g_ffn': 1.0 + nrm(ks[11], (DEPTH, D_MODEL), 0.02),
        'w_mlstm_in': nrm(ks[12], (N_A_LAYERS, D_MODEL, m_in), D_MODEL ** -0.5),
        'b_mlstm_gates': jnp.concatenate([b_i, b_f], axis=-1),
        'w_mlstm_out': nrm(ks[13], (N_A_LAYERS, M_HEADS * M_DV, D_MODEL), (M_HEADS * M_DV) ** -0.5),
        'g_kv': 1.0 + nrm(ks[14], (D_MODEL,), 0.02),
        'w_kv': nrm(ks[15], (D_MODEL, 2 * kv_w), D_MODEL ** -0.5),
        'w_attn_q': nrm(ks[16], (N_B_LAYERS, D_MODEL, A_HEADS * A_HEAD_DIM), D_MODEL ** -0.5),
        'attn_sinks': nrm(ks[17], (N_B_LAYERS, A_HEADS), 0.5),
        'w_attn_o': nrm(ks[18], (N_B_LAYERS, A_HEADS * A_HEAD_DIM, D_MODEL), (A_HEADS * A_HEAD_DIM) ** -0.5),
        'w_ffn_up': nrm(ks[19], (DEPTH, D_MODEL, 2 * D_FF), D_MODEL ** -0.5),
        'ffn_conv_w': nrm(ks[20], (DEPTH, CONV_W, D_FF), CONV_W ** -0.5),
        'ffn_conv_b': nrm(ks[21], (DEPTH, D_FF), 0.02),
        'w_ffn_down': nrm(ks[22], (DEPTH, D_FF, D_MODEL), D_FF ** -0.5),
        'g_final': 1.0 + nrm(ks[23], (D_MODEL,), 0.02),
    }


def reference(x_prompt, x_sample, state_mlstm_C, state_mlstm_n, state_mlstm_m, cache_k_win, cache_v_win,
              state_ffn_conv, g_mix, g_ffn, w_mlstm_in, b_mlstm_gates, w_mlstm_out, g_kv, w_kv, w_attn_q,
              attn_sinks, w_attn_o, w_ffn_up, ffn_conv_w, ffn_conv_b, w_ffn_down, g_final):
    kv_w = A_KV_HEADS * A_HEAD_DIM

    def trunk(x, pos, C0, n0, m0, conv0, k_win0, v_win0):
        bx, s, _ = x.shape
        h = x
        Cs, ns, ms, convs = [], [], [], []
        k_win = v_win = None
        k_all = v_all = None
        for l in range(DEPTH):
            hn = rms_norm(h, g_mix[l])
            if l < N_A_LAYERS:
                y, (C, n, m) = mlstm_mixer(hn, w_mlstm_in[l], b_mlstm_gates[l], w_mlstm_out[l],
                                           C0[l], n0[l], m0[l])
                Cs.append(C)
                ns.append(n)
                ms.append(m)
            else:
                if l == N_A_LAYERS:
                    kvp = rms_norm(h, g_kv) @ w_kv
                    k = rope(kvp[..., :kv_w].reshape(bx, s, A_KV_HEADS, A_HEAD_DIM), pos)
                    v = kvp[..., kv_w:].reshape(bx, s, A_KV_HEADS, A_HEAD_DIM)
                    if k_win0 is None:
                        k_all, v_all = k, v
                    else:
                        k_all = jnp.concatenate([k_win0.astype(k.dtype), k], axis=1)
                        v_all = jnp.concatenate([v_win0.astype(v.dtype), v], axis=1)
                    k_win, v_win = k_all[:, -WINDOW:], v_all[:, -WINDOW:]
                jb = l - N_A_LAYERS
                q = rope((hn @ w_attn_q[jb]).reshape(bx, s, A_KV_HEADS, A_GROUP, A_HEAD_DIM), pos)
                if k_win0 is None:
                    o = banded_window_attention(q, k_all, v_all, attn_sinks[jb])
                else:
                    o = cached_window_attention(q, k_all, v_all, attn_sinks[jb])
                y = o.reshape(bx, s, A_HEADS * A_HEAD_DIM) @ w_attn_o[jb]
            h = h + y
            y, buf = conv_ffn(rms_norm(h, g_ffn[l]), w_ffn_up[l], ffn_conv_w[l], ffn_conv_b[l],
                              w_ffn_down[l], conv0[l])
            convs.append(buf)
            h = h + y
        return (rms_norm(h, g_final), jnp.stack(Cs), jnp.stack(ns), jnp.stack(ms), k_win, v_win,
                jnp.stack(convs))

    bp, sp = x_prompt.shape[0], x_prompt.shape[1]
    f32 = jnp.float32
    pC0 = jnp.zeros((N_A_LAYERS, bp, M_HEADS, M_DK, M_DV), f32)
    pn0 = jnp.zeros((N_A_LAYERS, bp, M_HEADS, M_DK), f32)
    pm0 = jnp.zeros((N_A_LAYERS, bp, M_HEADS), f32)
    pconv0 = jnp.zeros((DEPTH, bp, CONV_W - 1, D_FF), x_prompt.dtype)
    y_prompt, pC, pn, pm, pk, pv, pconv = trunk(x_prompt, jnp.arange(sp), pC0, pn0, pm0, pconv0, None, None)

    pos_s = PAST_LEN + jnp.arange(x_sample.shape[1])
    y_sample, sC, sn, sm, sk, sv, sconv = trunk(x_sample, pos_s, state_mlstm_C, state_mlstm_n, state_mlstm_m,
                                                 state_ffn_conv, cache_k_win, cache_v_win)
    return (y_prompt, y_sample, pC, pn, pm, pk, pv, pconv, sC, sn, sm, sk, sv, sconv)
```

```python
import functools
import math

import jax
import jax.numpy as jnp
from jax import lax
from jax.experimental import pallas as pl
from jax.experimental.pallas import tpu as pltpu

F32 = jnp.float32
BF16 = jnp.bfloat16

D_MODEL = 4096
PAST_LEN = 16384
M_HEADS = 8
M_DK = D_MODEL // (2 * M_HEADS)
M_DV = D_MODEL // M_HEADS
M_CHUNK = 128
GATE_SOFT_CAP = 15.0
A_HEADS = 32
A_KV_HEADS = 8
A_GROUP = A_HEADS // A_KV_HEADS
A_HEAD_DIM = D_MODEL // A_HEADS
WINDOW = 128
ROPE_THETA = 10000.0
NEG_INF = -1e30
D_FF = 11008
CONV_W = 3
RMS_EPS = 1e-6

V7X_VMEM_BYTES = 64 * 1024 * 1024
VMEM_LIMIT_BYTES = V7X_VMEM_BYTES - 8 * 1024 * 1024
LANES = 128
SUBLANES = 8

MM_BLOCK_M = 1024
MM_BLOCK_N = 512
FFN_BLOCK_N = 256
DOWN_BLOCK_M = 512
DOWN_BLOCK_N = 512
RMS_BLOCK_M = 256
CAST_ROWS = 256


def _params(sem):
    return pltpu.CompilerParams(dimension_semantics=sem, vmem_limit_bytes=VMEM_LIMIT_BYTES)


def _rms_kernel(x_ref, g_ref, *o_refs):
    xf = x_ref[...]
    r = lax.rsqrt(jnp.mean(xf * xf, axis=-1, keepdims=True) + RMS_EPS)
    xr = xf * r
    for k, o_ref in enumerate(o_refs):
        o_ref[...] = (xr * g_ref[k:k + 1, :]).astype(o_ref.dtype)


def _rms_norm(x, gains, out_dtype):
    m, d = x.shape
    n_g = gains.shape[0]
    bm = min(RMS_BLOCK_M, m)
    assert m % bm == 0
    outs = pl.pallas_call(
        _rms_kernel,
        grid=(m // bm,),
        in_specs=[pl.BlockSpec((bm, d), lambda i: (i, 0)),
                  pl.BlockSpec((n_g, d), lambda i: (0, 0))],
        out_specs=[pl.BlockSpec((bm, d), lambda i: (i, 0)) for _ in range(n_g)],
        out_shape=[jax.ShapeDtypeStruct((m, d), out_dtype) for _ in range(n_g)],
        compiler_params=_params(("arbitrary",)),
        name="rms_norm",
    )(x, gains)
    return outs


def _cast_weight(w_ref, wbf_ref, col0, ncols):
    k = w_ref.shape[0]
    rows_per = min(CAST_ROWS, k)
    assert k % rows_per == 0

    def body(r, carry):
        rows = pl.ds(pl.multiple_of(r * rows_per, rows_per), rows_per)
        wbf_ref[rows, col0:col0 + ncols] = w_ref[rows, :].astype(BF16)
        return carry

    lax.fori_loop(0, k // rows_per, body, 0)


def _mm_kernel(*refs, n_w, n_extra, n_out, epilogue, cast):
    x_ref = refs[0]
    w_refs = refs[1:1 + n_w]
    extra = refs[1 + n_w:1 + n_w + n_extra]
    outs = refs[1 + n_w + n_extra:1 + n_w + n_extra + n_out]
    scratch = refs[1 + n_w + n_extra + n_out:]
    j = pl.program_id(0)
    i = pl.program_id(1)
    if cast:
        wbf_ref = scratch[0]
        scratch = scratch[1:]
        bn = w_refs[0].shape[1]

        @pl.when(i == 0)
        def _():
            for t, w_ref in enumerate(w_refs):
                _cast_weight(w_ref, wbf_ref, t * bn, bn)

        w = wbf_ref[...]
    else:
        w = w_refs[0][...]
    acc = jnp.dot(x_ref[...], w, preferred_element_type=F32)
    epilogue(acc, i, j, extra, outs, scratch)


class _Epilogue:
    n_scratch = 0

    def scratch_shapes(self, bm, bn):
        return []


def _matmul(x, w, epilogue, *, w_col_blocks, bn, bm=MM_BLOCK_M, extra=(), extra_specs=(),
            out_shapes, out_specs, cast=True, name):
    m, k = x.shape
    bm = min(bm, m)
    assert m % bm == 0 and w.shape[0] == k
    n_w = len(w_col_blocks)
    n_j = epilogue.n_col_steps
    in_specs = [pl.BlockSpec((bm, k), lambda j, i: (i, 0))]
    for f in w_col_blocks:
        in_specs.append(pl.BlockSpec((k, bn), functools.partial(lambda j, i, f: (0, f(j)), f=f)))
    in_specs.extend(extra_specs)
    scratch = []
    if cast:
        scratch.append(pltpu.VMEM((k, n_w * bn), BF16))
    scratch.extend(epilogue.scratch_shapes(bm, bn))
    n_out = len(out_shapes) if isinstance(out_shapes, (list, tuple)) else 1
    kern = functools.partial(_mm_kernel, n_w=n_w, n_extra=len(extra), n_out=n_out, epilogue=epilogue,
                             cast=cast)
    return pl.pallas_call(
        kern,
        grid=(n_j, m // bm),
        in_specs=in_specs,
        out_specs=out_specs,
        out_shape=out_shapes,
        scratch_shapes=scratch,
        compiler_params=_params(("arbitrary", "arbitrary")),
        name=name,
    )(x, *([w] * n_w), *extra)


class _StoreEpilogue(_Epilogue):
    def __init__(self, n_col_steps, scaled_col_steps=0, scale=1.0, residual=False):
        self.n_col_steps = n_col_steps
        self.scaled_col_steps = scaled_col_steps
        self.scale = scale
        self.residual = residual

    def __call__(self, acc, i, j, extra, outs, scratch):
        if self.scaled_col_steps:
            acc = acc * jnp.where(j < self.scaled_col_steps, self.scale, 1.0).astype(F32)
        if self.residual:
            acc = extra[0][...] + acc
        outs[0][...] = acc.astype(outs[0].dtype)


def _linear(x, w, *, col0, n, out_dtype, bn=MM_BLOCK_N, scaled_cols=0, scale=1.0, residual=None,
            cast=True, bm=MM_BLOCK_M, name):
    m = x.shape[0]
    bm = min(bm, m)
    assert n % bn == 0 and col0 % bn == 0 and scaled_cols % bn == 0
    j0 = col0 // bn
    ep = _StoreEpilogue(n // bn, scaled_cols // bn, scale, residual is not None)
    extra, extra_specs = (), ()
    if residual is not None:
        extra = (residual,)
        extra_specs = (pl.BlockSpec((bm, bn), lambda j, i: (i, j)),)
    return _matmul(
        x, w, ep, w_col_blocks=[lambda j: j + j0], bn=bn, bm=bm, extra=extra, extra_specs=extra_specs,
        out_shapes=jax.ShapeDtypeStruct((m, n), out_dtype),
        out_specs=pl.BlockSpec((bm, bn), lambda j, i: (i, j)),
        cast=cast, name=name)


class _RopeEpilogue(_Epilogue):
    def __init__(self, n_col_steps, rope_col_steps, n_out):
        self.n_col_steps = n_col_steps
        self.rope_col_steps = rope_col_steps
        self.n_out = n_out

    def __call__(self, acc, i, j, extra, outs, scratch):
        cos_ref, sin_ref = extra

        def store(val):
            for o_ref in outs:
                o_ref[...] = val.astype(o_ref.dtype)

        def rotated():
            cos, sin = cos_ref[...], sin_ref[...]
            heads = []
            for h in range(acc.shape[1] // A_HEAD_DIM):
                xh = acc[:, h * A_HEAD_DIM:(h + 1) * A_HEAD_DIM]
                heads.append(xh * cos + pltpu.roll(xh, A_HEAD_DIM // 2, axis=1) * sin)
            return jnp.concatenate(heads, axis=1)

        if self.rope_col_steps >= self.n_col_steps:
            store(rotated())
        else:
            @pl.when(j < self.rope_col_steps)
            def _():
                store(rotated())

            @pl.when(j >= self.rope_col_steps)
            def _():
                store(acc)


def _rope_linear(x, w, cos_t, sin_t, *, n, rope_cols, out_dtypes, rows_per_seq, name, bn=MM_BLOCK_N):
    m = x.shape[0]
    bm = min(MM_BLOCK_M, m, rows_per_seq)
    assert rows_per_seq % bm == 0 and n % bn == 0 and rope_cols % bn == 0
    tiles_per_seq = rows_per_seq // bm
    ep = _RopeEpilogue(n // bn, rope_cols // bn, len(out_dtypes))
    tab_spec = pl.BlockSpec((bm, A_HEAD_DIM), lambda j, i: (i % tiles_per_seq, 0))
    return _matmul(
        x, w, ep, w_col_blocks=[lambda j: j], bn=bn, bm=bm, extra=(cos_t, sin_t),
        extra_specs=(tab_spec, tab_spec),
        out_shapes=[jax.ShapeDtypeStruct((m, n), dt) for dt in out_dtypes],
        out_specs=[pl.BlockSpec((bm, bn), lambda j, i: (i, j)) for _ in out_dtypes],
        name=name)


def _gates_kernel(x_ref, w_ref, b_ref, o_ref):
    g = jnp.dot(x_ref[...], w_ref[...].astype(BF16), preferred_element_type=F32) + b_ref[...]
    ig = GATE_SOFT_CAP * jnp.tanh(g / GATE_SOFT_CAP)
    lf = jnp.minimum(g, 0.0) - jnp.log1p(jnp.exp(-jnp.abs(g)))
    lane = lax.broadcasted_iota(jnp.int32, g.shape, 1)
    o_ref[...] = jnp.where(lane < M_HEADS, ig, lf)


def _mlstm_gates(xn, w_gates, b_gates):
    m, d = xn.shape
    bm = min(MM_BLOCK_M, m)
    ng = 2 * M_HEADS
    return pl.pallas_call(
        _gates_kernel,
        grid=(m // bm,),
        in_specs=[pl.BlockSpec((bm, d), lambda i: (i, 0)),
                  pl.BlockSpec((d, ng), lambda i: (0, 0)),
                  pl.BlockSpec((1, ng), lambda i: (0, 0))],
        out_specs=pl.BlockSpec((bm, ng), lambda i: (i, 0)),
        out_shape=jax.ShapeDtypeStruct((m, ng), F32),
        compiler_params=_params(("arbitrary",)),
        name="mlstm_gates",
    )(xn, w_gates, b_gates.reshape(1, ng).astype(F32))


def _mlstm_chunk_kernel(q_ref, k_ref, v_ref, o_ref, gcol_ref, grow_ref, c0_ref, n0_ref, m0_ref,
                        h_ref, c_ref, n_ref, m_ref):
    L = q_ref.shape[0]

    @pl.when(pl.program_id(1) == 0)
    def _():
        c_ref[...] = c0_ref[...]
        n_ref[...] = n0_ref[...]
        m_ref[...] = m0_ref[...]

    t_idx = lax.broadcasted_iota(jnp.int32, (L, L), 0)
    s_idx = lax.broadcasted_iota(jnp.int32, (L, L), 1)
    causal = s_idx <= t_idx
    for h in range(M_HEADS):
        q = q_ref[:, h * M_DK:(h + 1) * M_DK]
        k = k_ref[:, h * M_DK:(h + 1) * M_DK]
        v = v_ref[:, h * M_DV:(h + 1) * M_DV]
        ig_col = gcol_ref[:, h:h + 1]
        lf_col = gcol_ref[:, M_HEADS + h:M_HEADS + h + 1]
        ig_row = grow_ref[h:h + 1, :]
        lf_row = grow_ref[M_HEADS + h:M_HEADS + h + 1, :]
        c_prev = c_ref[0, h]
        n_prev = n_ref[0, h:h + 1, :]
        m_prev = m_ref[0, h:h + 1, 0:1]

        b_col = jnp.sum(jnp.where(causal, lf_row, 0.0), axis=1, keepdims=True)
        b_row = jnp.sum(jnp.where(t_idx <= s_idx, lf_col, 0.0), axis=0, keepdims=True)
        dlog = jnp.where(causal, b_col - b_row + ig_row, -jnp.inf)
        g = b_col + m_prev
        m_t = jnp.maximum(g, jnp.max(dlog, axis=1, keepdims=True))
        w_intra = jnp.exp(dlog - m_t)
        w_inter = jnp.exp(g - m_t)

        qk = lax.dot_general(q, k, (((1,), (1,)), ((), ())), preferred_element_type=F32)
        s = qk * w_intra
        num = (jnp.dot(s.astype(BF16), v, preferred_element_type=F32)
               + w_inter * jnp.dot(q, c_prev.astype(BF16), preferred_element_type=F32))
        qn = jnp.sum(q.astype(F32) * n_prev, axis=1, keepdims=True)
        den = jnp.sum(s, axis=1, keepdims=True) + w_inter * qn
        hh = num / jnp.maximum(jnp.abs(den), jnp.exp(-m_t))
        gate = jax.nn.sigmoid(o_ref[:, h * M_DV:(h + 1) * M_DV])
        h_ref[:, h * M_DV:(h + 1) * M_DV] = (gate * hh).astype(h_ref.dtype)

        b_last = b_col[L - 1:L, :]
        m_last = m_t[L - 1:L, :]
        decay = w_inter[L - 1:L, :]
        w_end = jnp.exp(b_last - b_col + ig_col - m_last)
        kw = k.astype(F32) * w_end
        c_ref[0, h] = decay * c_prev + lax.dot_general(
            kw.astype(BF16), v, (((0,), (0,)), ((), ())), preferred_element_type=F32)
        n_ref[0, h:h + 1, :] = decay * n_prev + jnp.sum(kw, axis=0, keepdims=True)
        m_ref[0, h:h + 1, :] = jnp.broadcast_to(m_last, (1, LANES))


def _mlstm_chunks(qkv, o_pre, gates, c0, n0, m0, batch, seq):
    L = M_CHUNK
    assert seq % L == 0
    nc = seq // L
    qk_w = M_HEADS * M_DK
    v_w = M_HEADS * M_DV
    m_rows = batch * seq
    grow = gates.T
    m0b = jnp.broadcast_to(m0[:, :, None], (batch, M_HEADS, LANES))
    row = lambda b, c: b * nc + c
    hg, c_new, n_new, m_new = pl.pallas_call(
        _mlstm_chunk_kernel,
        grid=(batch, nc),
        in_specs=[
            pl.BlockSpec((L, qk_w), lambda b, c: (row(b, c), 0)),
            pl.BlockSpec((L, qk_w), lambda b, c: (row(b, c), 1)),
            pl.BlockSpec((L, v_w), lambda b, c: (row(b, c), 1)),
            pl.BlockSpec((L, v_w), lambda b, c: (row(b, c), 0)),
            pl.BlockSpec((L, 2 * M_HEADS), lambda b, c: (row(b, c), 0)),
            pl.BlockSpec((2 * M_HEADS, L), lambda b, c: (0, row(b, c))),
            pl.BlockSpec((1, M_HEADS, M_DK, M_DV), lambda b, c: (b, 0, 0, 0)),
            pl.BlockSpec((1, M_HEADS, M_DK), lambda b, c: (b, 0, 0)),
            pl.BlockSpec((1, M_HEADS, LANES), lambda b, c: (b, 0, 0)),
        ],
        out_specs=[
            pl.BlockSpec((L, v_w), lambda b, c: (row(b, c), 0)),
            pl.BlockSpec((1, M_HEADS, M_DK, M_DV), lambda b, c: (b, 0, 0, 0)),
            pl.BlockSpec((1, M_HEADS, M_DK), lambda b, c: (b, 0, 0)),
            pl.BlockSpec((1, M_HEADS, LANES), lambda b, c: (b, 0, 0)),
        ],
        out_shape=[
            jax.ShapeDtypeStruct((m_rows, v_w), BF16),
            jax.ShapeDtypeStruct((batch, M_HEADS, M_DK, M_DV), F32),
            jax.ShapeDtypeStruct((batch, M_HEADS, M_DK), F32),
            jax.ShapeDtypeStruct((batch, M_HEADS, LANES), F32),
        ],
        compiler_params=_params(("arbitrary", "arbitrary")),
        name="mlstm_chunks",
    )(qkv, qkv, qkv, o_pre, gates, grow, c0, n0, m0b)
    return hg, c_new, n_new, m_new[:, :, 0]


def _mlstm_step_kernel(q_ref, k_ref, v_ref, o_ref, g_ref, c0_ref, n0_ref, m0_ref,
                       h_ref, c_ref, n_ref, m_ref):
    dk_idx = lax.broadcasted_iota(jnp.int32, (M_DK, M_DK), 0)
    dk_lane = lax.broadcasted_iota(jnp.int32, (M_DK, M_DK), 1)
    eye = dk_idx == dk_lane
    m_rows = []
    for h in range(M_HEADS):
        q = q_ref[0, :, h * M_DK:(h + 1) * M_DK]
        k = k_ref[0, :, h * M_DK:(h + 1) * M_DK]
        v = v_ref[0, :, h * M_DV:(h + 1) * M_DV]
        ig = g_ref[0, :, h:h + 1]
        lf = g_ref[0, :, M_HEADS + h:M_HEADS + h + 1]
        c_prev = c0_ref[0, h]
        n_prev = n0_ref[0, h:h + 1, :]
        m_prev = m0_ref[0, h:h + 1, 0:1]
        qf, kf, vf = q.astype(F32), k.astype(F32), v.astype(F32)

        g = lf + m_prev
        m_t = jnp.maximum(g, ig)
        w_intra = jnp.exp(ig - m_t)
        w_inter = jnp.exp(g - m_t)
        s = jnp.sum(qf * kf, axis=1, keepdims=True) * w_intra
        q_col = jnp.sum(jnp.where(eye, qf, 0.0), axis=1, keepdims=True)
        qc = jnp.sum(q_col * c_prev, axis=0, keepdims=True)
        num = s * vf + w_inter * qc
        den = s + w_inter * jnp.sum(qf * n_prev, axis=1, keepdims=True)
        hh = num / jnp.maximum(jnp.abs(den), jnp.exp(-m_t))
        gate = jax.nn.sigmoid(o_ref[0, :, h * M_DV:(h + 1) * M_DV])
        h_ref[0, :, h * M_DV:(h + 1) * M_DV] = (gate * hh).astype(h_ref.dtype)

        kw = kf * w_intra
        kw_col = jnp.sum(jnp.where(eye, kw, 0.0), axis=1, keepdims=True)
        c_ref[0, h] = w_inter * c_prev + kw_col * vf
        n_ref[0, h:h + 1, :] = w_inter * n_prev + kw
        m_rows.append(jnp.broadcast_to(m_t, (1, LANES)))
    m_ref[0] = jnp.concatenate(m_rows, axis=0)


def _mlstm_step(qkv, o_pre, gates, c0, n0, m0):
    batch = qkv.shape[0]
    qk_w = M_HEADS * M_DK
    v_w = M_HEADS * M_DV
    qkv3 = qkv.reshape(batch, 1, 2 * qk_w + v_w)
    m0b = jnp.broadcast_to(m0[:, :, None], (batch, M_HEADS, LANES))
    hg, c_new, n_new, m_new = pl.pallas_call(
        _mlstm_step_kernel,
        grid=(batch,),
        in_specs=[
            pl.BlockSpec((1, 1, qk_w), lambda b: (b, 0, 0)),
            pl.BlockSpec((1, 1, qk_w), lambda b: (b, 0, 1)),
            pl.BlockSpec((1, 1, v_w), lambda b: (b, 0, 1)),
            pl.BlockSpec((1, 1, v_w), lambda b: (b, 0, 0)),
            pl.BlockSpec((1, 1, 2 * M_HEADS), lambda b: (b, 0, 0)),
            pl.BlockSpec((1, M_HEADS, M_DK, M_DV), lambda b: (b, 0, 0, 0)),
            pl.BlockSpec((1, M_HEADS, M_DK), lambda b: (b, 0, 0)),
            pl.BlockSpec((1, M_HEADS, LANES), lambda b: (b, 0, 0)),
        ],
        out_specs=[
            pl.BlockSpec((1, 1, v_w), lambda b: (b, 0, 0)),
            pl.BlockSpec((1, M_HEADS, M_DK, M_DV), lambda b: (b, 0, 0, 0)),
            pl.BlockSpec((1, M_HEADS, M_DK), lambda b: (b, 0, 0)),
            pl.BlockSpec((1, M_HEADS, LANES), lambda b: (b, 0, 0)),
        ],
        out_shape=[
            jax.ShapeDtypeStruct((batch, 1, v_w), BF16),
            jax.ShapeDtypeStruct((batch, M_HEADS, M_DK, M_DV), F32),
            jax.ShapeDtypeStruct((batch, M_HEADS, M_DK), F32),
            jax.ShapeDtypeStruct((batch, M_HEADS, LANES), F32),
        ],
        compiler_params=_params(("arbitrary",)),
        name="mlstm_step",
    )(qkv3, qkv3, qkv3, o_pre.reshape(batch, 1, v_w), gates.reshape(batch, 1, 2 * M_HEADS),
      c0, n0, m0b)
    return hg.reshape(batch, v_w), c_new, n_new, m_new[:, :, 0]


def _softmax_with_sink(scores, sink):
    mx = jnp.maximum(jnp.max(scores, axis=-1, keepdims=True), sink)
    e = jnp.exp(scores - mx)
    return e / (jnp.sum(e, axis=-1, keepdims=True) + jnp.exp(sink - mx))


def _swa_kernel(sink_ref, q_ref, kp_ref, ko_ref, vp_ref, vo_ref, o_ref):
    W = WINDOW
    blk = pl.program_id(1)
    kv = pl.program_id(2)
    k_cat = jnp.concatenate([kp_ref[...], ko_ref[...]], axis=0)
    v_cat = jnp.concatenate([vp_ref[...], vo_ref[...]], axis=0)
    q = jnp.concatenate([q_ref[:, g * A_HEAD_DIM:(g + 1) * A_HEAD_DIM] for g in range(A_GROUP)], axis=0)
    scores = lax.dot_general(q, k_cat, (((1,), (1,)), ((), ())), preferred_element_type=F32)
    scores = scores * (A_HEAD_DIM ** -0.5)
    i_idx = lax.broadcasted_iota(jnp.int32, (W, 2 * W), 0)
    j_idx = lax.broadcasted_iota(jnp.int32, (W, 2 * W), 1)
    first_key = jnp.where(blk > 0, 0, W)
    mask = (j_idx >= jnp.maximum(i_idx, first_key)) & (j_idx <= i_idx + W)
    probs = []
    for g in range(A_GROUP):
        sc = jnp.where(mask, scores[g * W:(g + 1) * W, :], NEG_INF)
        probs.append(_softmax_with_sink(sc, sink_ref[kv * A_GROUP + g]).astype(BF16))
    p = jnp.concatenate(probs, axis=0)
    o = jnp.dot(p, v_cat, preferred_element_type=F32)
    for g in range(A_GROUP):
        o_ref[:, g * A_HEAD_DIM:(g + 1) * A_HEAD_DIM] = o[g * W:(g + 1) * W, :].astype(o_ref.dtype)


def _swa_attention(q, k, v, sinks, batch, seq):
    W = WINDOW
    assert seq % W == 0
    nb = seq // W
    gd = A_GROUP * A_HEAD_DIM
    row = lambda b, n: b * nb + n
    prev = lambda b, n: jnp.maximum(b * nb + n - 1, 0)
    return pl.pallas_call(
        _swa_kernel,
        grid=(batch, nb, A_KV_HEADS),
        in_specs=[
            pl.BlockSpec(memory_space=pltpu.SMEM),
            pl.BlockSpec((W, gd), lambda b, n, h: (row(b, n), h)),
            pl.BlockSpec((W, A_HEAD_DIM), lambda b, n, h: (prev(b, n), h)),
            pl.BlockSpec((W, A_HEAD_DIM), lambda b, n, h: (row(b, n), h)),
            pl.BlockSpec((W, A_HEAD_DIM), lambda b, n, h: (prev(b, n), h)),
            pl.BlockSpec((W, A_HEAD_DIM), lambda b, n, h: (row(b, n), h)),
        ],
        out_specs=pl.BlockSpec((W, gd), lambda b, n, h: (row(b, n), h)),
        out_shape=jax.ShapeDtypeStruct((batch * seq, A_HEADS * A_HEAD_DIM), BF16),
        compiler_params=_params(("arbitrary", "arbitrary", "arbitrary")),
        name="swa_attention",
    )(sinks.astype(F32), q, k, k, v, v)


def _decode_attn_kernel(sink_ref, q_ref, kn_ref, vn_ref, ck_ref, cv_ref, o_ref, kw_ref, vw_ref):
    W = WINDOW
    D = A_HEAD_DIM
    R = SUBLANES
    row_idx = lax.broadcasted_iota(jnp.int32, (R, 1), 0)
    for kv in range(A_KV_HEADS):
        kc = ck_ref[0, :, kv * D:(kv + 1) * D].astype(BF16)
        vc = cv_ref[0, :, kv * D:(kv + 1) * D].astype(BF16)
        kn = kn_ref[0, :, kv * D:(kv + 1) * D].astype(BF16).astype(F32)
        vn = vn_ref[0, :, kv * D:(kv + 1) * D].astype(BF16).astype(F32)
        qf = jnp.zeros((R, D), F32)
        sink = jnp.zeros((R, 1), F32)
        for g in range(A_GROUP):
            hq = kv * A_GROUP + g
            qf = jnp.where(row_idx == g, q_ref[0, :, hq * D:(hq + 1) * D].astype(F32), qf)
            sink = jnp.where(row_idx == g, sink_ref[hq], sink)
        scale = A_HEAD_DIM ** -0.5
        s_c = lax.dot_general(qf.astype(BF16), kc, (((1,), (1,)), ((), ())),
                              preferred_element_type=F32) * scale
        s_n = jnp.sum(qf * kn, axis=1, keepdims=True) * scale
        mx = jnp.maximum(jnp.maximum(jnp.max(s_c, axis=1, keepdims=True), s_n), sink)
        e_c = jnp.exp(s_c - mx)
        e_n = jnp.exp(s_n - mx)
        den = jnp.sum(e_c, axis=1, keepdims=True) + e_n + jnp.exp(sink - mx)
        p_c = (e_c / den).astype(BF16)
        p_n = (e_n / den).astype(BF16).astype(F32)
        o = jnp.dot(p_c, vc, preferred_element_type=F32) + p_n * vn
        for g in range(A_GROUP):
            o_ref[0, :, (kv * A_GROUP + g) * D:(kv * A_GROUP + g + 1) * D] = o[g:g + 1, :].astype(o_ref.dtype)
    last = lax.broadcasted_iota(jnp.int32, (W, 1), 0) == W - 1
    kw_ref[0] = jnp.where(last, kn_ref[0], pltpu.roll(ck_ref[0], W - 1, axis=0))
    vw_ref[0] = jnp.where(last, vn_ref[0], pltpu.roll(cv_ref[0], W - 1, axis=0))


def _decode_attention(q, k_new, v_new, cache_k, cache_v, sinks):
    batch = q.shape[0]
    kvw = A_KV_HEADS * A_HEAD_DIM
    qw = A_HEADS * A_HEAD_DIM
    o, k_win, v_win = pl.pallas_call(
        _decode_attn_kernel,
        grid=(batch,),
        in_specs=[
            pl.BlockSpec(memory_space=pltpu.SMEM),
            pl.BlockSpec((1, 1, qw), lambda b: (b, 0, 0)),
            pl.BlockSpec((1, 1, kvw), lambda b: (b, 0, 0)),
            pl.BlockSpec((1, 1, kvw), lambda b: (b, 0, 0)),
            pl.BlockSpec((1, WINDOW, kvw), lambda b: (b, 0, 0)),
            pl.BlockSpec((1, WINDOW, kvw), lambda b: (b, 0, 0)),
        ],
        out_specs=[
            pl.BlockSpec((1, 1, qw), lambda b: (b, 0, 0)),
            pl.BlockSpec((1, WINDOW, kvw), lambda b: (b, 0, 0)),
            pl.BlockSpec((1, WINDOW, kvw), lambda b: (b, 0, 0)),
        ],
        out_shape=[
            jax.ShapeDtypeStruct((batch, 1, qw), BF16),
            jax.ShapeDtypeStruct((batch, WINDOW, kvw), F32),
            jax.ShapeDtypeStruct((batch, WINDOW, kvw), F32),
        ],
        compiler_params=_params(("arbitrary",)),
        name="decode_attention",
    )(sinks.astype(F32), q.reshape(batch, 1, qw), k_new.reshape(batch, 1, kvw),
      v_new.reshape(batch, 1, kvw), cache_k, cache_v)
    return o.reshape(batch, qw), k_win, v_win


def _conv_act(g2, g1, g0, val, cw_ref, cb_ref):
    conv = cb_ref[...] + cw_ref[0:1, :] * g2
    conv = conv + cw_ref[1:2, :] * g1
    conv = conv + cw_ref[2:3, :] * g0
    return conv * jax.nn.sigmoid(conv) * val


class _ConvFfnSeqEpilogue(_Epilogue):
    n_scratch = 1

    def __init__(self, n_col_steps, tiles_per_seq):
        self.n_col_steps = n_col_steps
        self.tiles_per_seq = tiles_per_seq

    def scratch_shapes(self, bm, bn):
        return [pltpu.VMEM((SUBLANES, bn), F32)]

    def __call__(self, acc, i, j, extra, outs, scratch):
        cw_ref, cb_ref, halo0_ref = extra
        act_ref, tail_ref = outs
        carry_ref, = scratch
        bm = acc.shape[0]
        bn = acc.shape[1] // 2
        gate, val = acc[:, :bn], acc[:, bn:]
        seq_start = (i % self.tiles_per_seq) == 0
        halo = jnp.where(seq_start, halo0_ref[0], carry_ref[...])
        ext = jnp.concatenate([halo, gate], axis=0)
        g1 = ext[SUBLANES - 1:SUBLANES - 1 + bm, :]
        g2 = ext[SUBLANES - 2:SUBLANES - 2 + bm, :]
        act_ref[...] = _conv_act(g2, g1, gate, val, cw_ref, cb_ref).astype(act_ref.dtype)
        tail = gate[bm - SUBLANES:, :]
        carry_ref[...] = tail
        tail_ref[0] = tail


class _ConvFfnStepEpilogue(_Epilogue):
    def __init__(self, n_col_steps):
        self.n_col_steps = n_col_steps

    def __call__(self, acc, i, j, extra, outs, scratch):
        cw_ref, cb_ref, g2_ref, g1_ref = extra
        act_ref, gate_ref = outs
        bn = acc.shape[1] // 2
        gate, val = acc[:, :bn], acc[:, bn:]
        act_ref[...] = _conv_act(g2_ref[...], g1_ref[...], gate, val, cw_ref, cb_ref).astype(act_ref.dtype)
        gate_ref[...] = gate


def _conv_ffn_up(xn, w_up, conv_w, conv_b, conv0, batch, seq):
    m = xn.shape[0]
    bn = FFN_BLOCK_N
    assert D_FF % bn == 0
    n_j = D_FF // bn
    cw = conv_w.astype(F32)
    cb = conv_b.reshape(1, D_FF).astype(F32)
    col_spec = lambda rows: pl.BlockSpec((rows, bn), lambda j, i: (0, j))
    w_blocks = [lambda j: j, lambda j: j + n_j]
    if seq == 1:
        bm = m
        g2, g1 = conv0[:, 0, :], conv0[:, 1, :]
        row_spec = pl.BlockSpec((bm, bn), lambda j, i: (i, j))
        act, gate = _matmul(
            xn, w_up, _ConvFfnStepEpilogue(n_j), w_col_blocks=w_blocks, bn=bn, bm=bm,
            extra=(cw, cb, g2, g1), extra_specs=(col_spec(CONV_W), col_spec(1), row_spec, row_spec),
            out_shapes=[jax.ShapeDtypeStruct((m, D_FF), BF16), jax.ShapeDtypeStruct((m, D_FF), F32)],
            out_specs=[row_spec, row_spec], name="conv_ffn_up_step")
        return act, jnp.stack([g1, gate], axis=1)
    bm = min(MM_BLOCK_M, seq)
    assert seq % bm == 0 and bm % SUBLANES == 0
    tiles_per_seq = seq // bm
    n_i = m // bm
    halo0 = jnp.concatenate(
        [jnp.zeros((batch, SUBLANES - (CONV_W - 1), D_FF), F32), conv0.astype(F32)], axis=1)
    act, tails = _matmul(
        xn, w_up, _ConvFfnSeqEpilogue(n_j, tiles_per_seq), w_col_blocks=w_blocks, bn=bn, bm=bm,
        extra=(cw, cb, halo0),
        extra_specs=(col_spec(CONV_W), col_spec(1),
                     pl.BlockSpec((1, SUBLANES, bn), lambda j, i: (i // tiles_per_seq, 0, j))),
        out_shapes=[jax.ShapeDtypeStruct((m, D_FF), BF16),
                    jax.ShapeDtypeStruct((n_i, SUBLANES, D_FF), F32)],
        out_specs=[pl.BlockSpec((bm, bn), lambda j, i: (i, j)),
                   pl.BlockSpec((1, SUBLANES, bn), lambda j, i: (i, 0, j))],
        name="conv_ffn_up")
    last_tiles = tails.reshape(batch, tiles_per_seq, SUBLANES, D_FF)[:, -1]
    return act, last_tiles[:, SUBLANES - (CONV_W - 1):, :]


def _conv_ffn(h, gain, w_up, conv_w, conv_b, w_down_bf16, conv0, batch, seq):
    xn, = _rms_norm(h, gain.reshape(1, D_MODEL), BF16)
    act, conv_new = _conv_ffn_up(xn, w_up, conv_w, conv_b, conv0, batch, seq)
    out = _linear(act, w_down_bf16, col0=0, n=D_MODEL, out_dtype=F32, bn=DOWN_BLOCK_N,
                  bm=DOWN_BLOCK_M, residual=h, cast=False, name="ffn_down")
    return out, conv_new


def _rope_tables(pos):
    inv_freq = ROPE_THETA ** (-jnp.arange(0, A_HEAD_DIM, 2, dtype=F32) / A_HEAD_DIM)
    ang = pos.astype(F32)[:, None] * inv_freq[None, :]
    cos, sin = jnp.cos(ang), jnp.sin(ang)
    return jnp.concatenate([cos, cos], axis=1), jnp.concatenate([-sin, sin], axis=1)


def _trunk(x, pos, c0, n0, m0, conv0, k_win0, v_win0, w):
    batch, seq, _ = x.shape
    m = batch * seq
    h = x.reshape(m, D_MODEL)
    qk_w = M_HEADS * M_DK
    v_w = M_HEADS * M_DV
    kv_w = A_KV_HEADS * A_HEAD_DIM

    hn, = _rms_norm(h, w["g_mix"][0:1], BF16)
    w_in = w["w_mlstm_in"][0]
    qkv = _linear(hn, w_in, col0=0, n=2 * qk_w + v_w, out_dtype=BF16, scaled_cols=qk_w,
                  scale=M_DK ** -0.5, name="mlstm_qkv")
    o_pre = _linear(hn, w_in, col0=2 * qk_w + v_w, n=v_w, out_dtype=F32, name="mlstm_ogate")
    gates = _mlstm_gates(hn, w_in[:, 2 * qk_w + 2 * v_w:], w["b_mlstm_gates"][0])
    if seq == 1:
        hg, c_new, n_new, m_new = _mlstm_step(qkv, o_pre, gates, c0[0], n0[0], m0[0])
    else:
        hg, c_new, n_new, m_new = _mlstm_chunks(qkv, o_pre, gates, c0[0], n0[0], m0[0], batch, seq)
    h = _linear(hg, w["w_mlstm_out"][0], col0=0, n=D_MODEL, out_dtype=F32, residual=h, name="mlstm_out")
    h, conv_a = _conv_ffn(h, w["g_ffn"][0], w["w_ffn_up"][0], w["ffn_conv_w"][0], w["ffn_conv_b"][0],
                          w["w_ffn_down_bf16"][0], conv0[0], batch, seq)

    kvn, hn = _rms_norm(h, jnp.stack([w["g_kv"], w["g_mix"][1]]), BF16)
    cos_t, sin_t = _rope_tables(pos)
    rows_per_seq = seq
    if seq == 1:
        cos_t = jnp.broadcast_to(cos_t, (batch, A_HEAD_DIM))
        sin_t = jnp.broadcast_to(sin_t, (batch, A_HEAD_DIM))
        rows_per_seq = batch
    kv_bf, kv_f32 = _rope_linear(kvn, w["w_kv"], cos_t, sin_t, n=2 * kv_w, rope_cols=kv_w,
                                 out_dtypes=(BF16, F32), rows_per_seq=rows_per_seq, name="kv_proj")
    q, = _rope_linear(hn, w["w_attn_q"][0], cos_t, sin_t, n=A_HEADS * A_HEAD_DIM,
                      rope_cols=A_HEADS * A_HEAD_DIM, out_dtypes=(BF16,), rows_per_seq=rows_per_seq,
                      name="q_proj")
    sinks = w["attn_sinks"][0]
    if k_win0 is None:
        o = _swa_attention(q, kv_bf[:, :kv_w], kv_bf[:, kv_w:], sinks, batch, seq)
        kv4 = kv_f32.reshape(batch, seq, 2, A_KV_HEADS, A_HEAD_DIM)
        k_win, v_win = kv4[:, -WINDOW:, 0], kv4[:, -WINDOW:, 1]
    else:
        assert seq == 1
        o, k_win, v_win = _decode_attention(
            q, kv_f32[:, :kv_w], kv_f32[:, kv_w:], k_win0.reshape(batch, WINDOW, kv_w),
            v_win0.reshape(batch, WINDOW, kv_w), sinks)
        k_win = k_win.reshape(batch, WINDOW, A_KV_HEADS, A_HEAD_DIM)
        v_win = v_win.reshape(batch, WINDOW, A_KV_HEADS, A_HEAD_DIM)
    h = _linear(o, w["w_attn_o"][0], col0=0, n=D_MODEL, out_dtype=F32, residual=h, name="attn_out")
    h, conv_b = _conv_ffn(h, w["g_ffn"][1], w["w_ffn_up"][1], w["ffn_conv_w"][1], w["ffn_conv_b"][1],
                          w["w_ffn_down_bf16"][1], conv0[1], batch, seq)
    y, = _rms_norm(h, w["g_final"].reshape(1, D_MODEL), F32)
    return (y.reshape(batch, seq, D_MODEL), c_new[None], n_new[None], m_new[None], k_win, v_win,
            jnp.stack([conv_a, conv_b]))


def kernel(x_prompt, x_sample, state_mlstm_C, state_mlstm_n, state_mlstm_m, cache_k_win, cache_v_win,
           state_ffn_conv, g_mix, g_ffn, w_mlstm_in, b_mlstm_gates, w_mlstm_out, g_kv, w_kv, w_attn_q,
           attn_sinks, w_attn_o, w_ffn_up, ffn_conv_w, ffn_conv_b, w_ffn_down, g_final):
    w = dict(g_mix=g_mix, g_ffn=g_ffn, w_mlstm_in=w_mlstm_in, b_mlstm_gates=b_mlstm_gates,
             w_mlstm_out=w_mlstm_out, g_kv=g_kv, w_kv=w_kv, w_attn_q=w_attn_q, attn_sinks=attn_sinks,
             w_attn_o=w_attn_o, w_ffn_up=w_ffn_up, ffn_conv_w=ffn_conv_w, ffn_conv_b=ffn_conv_b,
             w_ffn_down_bf16=w_ffn_down.astype(BF16), g_final=g_final)
    bp, sp = x_prompt.shape[0], x_prompt.shape[1]
    bs, ss = x_sample.shape[0], x_sample.shape[1]
    n_a = state_mlstm_C.shape[0]
    depth = state_ffn_conv.shape[0]
    p_c0 = jnp.zeros((n_a, bp, M_HEADS, M_DK, M_DV), F32)
    p_n0 = jnp.zeros((n_a, bp, M_HEADS, M_DK), F32)
    p_m0 = jnp.zeros((n_a, bp, M_HEADS), F32)
    p_conv0 = jnp.zeros((depth, bp, CONV_W - 1, D_FF), F32)
    prompt = _trunk(x_prompt, jnp.arange(sp), p_c0, p_n0, p_m0, p_conv0, None, None, w)
    pos_s = PAST_LEN + jnp.arange(ss)
    sample = _trunk(x_sample, pos_s, state_mlstm_C, state_mlstm_n, state_mlstm_m, state_ffn_conv,
                    cache_k_win, cache_v_win, w)
    return (prompt[0], sample[0]) + prompt[1:] + sample[1:]
```

```python
import functools
from typing import NamedTuple

import jax
import jax.numpy as jnp
from jax import lax
from jax.experimental import pallas as pl
from jax.experimental.pallas import tpu as pltpu

F32 = jnp.float32
BF16 = jnp.bfloat16

D_MODEL = 4096
PAST_LEN = 16384
M_HEADS = 8
M_DK = D_MODEL // (2 * M_HEADS)
M_DV = D_MODEL // M_HEADS
M_CHUNK = 128
GATE_SOFT_CAP = 15.0
A_HEADS = 32
A_KV_HEADS = 8
A_GROUP = A_HEADS // A_KV_HEADS
A_HEAD_DIM = D_MODEL // A_HEADS
WINDOW = 128
ROPE_THETA = 10000.0
NEG_INF = -1e30
D_FF = 11008
CONV_W = 3
RMS_EPS = 1e-6

V7X_VMEM_BYTES = 64 * 1024 * 1024
VMEM_LIMIT_BYTES = V7X_VMEM_BYTES - 8 * 1024 * 1024
LANES = 128
SUBLANES = 8

MM_BLOCK_M = 1024
MM_BLOCK_N = 512
FFN_BLOCK_N = 256
DOWN_BLOCK_M = 512
DOWN_BLOCK_N = 512
RMS_BLOCK_M = 256
CAST_ROWS = 256


def _params(sem):
    return pltpu.CompilerParams(dimension_semantics=sem, vmem_limit_bytes=VMEM_LIMIT_BYTES)


def _rms_kernel(x_ref, g_ref, *o_refs):
    xf = x_ref[...]
    r = lax.rsqrt(jnp.mean(xf * xf, axis=-1, keepdims=True) + RMS_EPS)
    xr = xf * r
    for k, o_ref in enumerate(o_refs):
        o_ref[...] = (xr * g_ref[k:k + 1, :]).astype(o_ref.dtype)


def _rms_norm(x, gains, out_dtype):
    m, d = x.shape
    n_g = gains.shape[0]
    bm = min(RMS_BLOCK_M, m)
    assert m % bm == 0
    outs = pl.pallas_call(
        _rms_kernel,
        grid=(m // bm,),
        in_specs=[pl.BlockSpec((bm, d), lambda i: (i, 0)),
                  pl.BlockSpec((n_g, d), lambda i: (0, 0))],
        out_specs=[pl.BlockSpec((bm, d), lambda i: (i, 0)) for _ in range(n_g)],
        out_shape=[jax.ShapeDtypeStruct((m, d), out_dtype) for _ in range(n_g)],
        compiler_params=_params(("arbitrary",)),
        name="rms_norm",
    )(x, gains)
    return outs


def _cast_weight(w_ref, wbf_ref, col0, ncols, transposed):
    k = wbf_ref.shape[0]
    rows_per = min(CAST_ROWS, k)
    assert k % rows_per == 0

    def body(r, carry):
        rows = pl.ds(pl.multiple_of(r * rows_per, rows_per), rows_per)
        chunk = w_ref[:, rows].T if transposed else w_ref[rows, :]
        wbf_ref[rows, col0:col0 + ncols] = chunk.astype(BF16)
        return carry

    lax.fori_loop(0, k // rows_per, body, 0)


def _mm_kernel(*refs, n_w, counts, epilogue, cast, transposed):
    n_extra, n_extra2, n_out, n_out2 = counts
    x_ref, x2_ref = refs[0], refs[1]
    refs = refs[2:]
    w_refs, refs = refs[:n_w], refs[n_w:]
    extra, refs = refs[:n_extra], refs[n_extra:]
    extra2, refs = refs[:n_extra2], refs[n_extra2:]
    outs, refs = refs[:n_out], refs[n_out:]
    outs2, scratch = refs[:n_out2], refs[n_out2:]
    j = pl.program_id(0)
    i = pl.program_id(1)
    bm = x_ref.shape[0]
    if cast:
        wbf_ref = scratch[0]
        scratch = scratch[1:]
    w_ref = wbf_ref if cast else w_refs[0]

    @pl.when(i == 0)
    def _():
        epilogue.init_scratch(scratch)
        if cast:
            bn = wbf_ref.shape[1] // n_w
            for t, wt_ref in enumerate(w_refs):
                _cast_weight(wt_ref, wbf_ref, t * bn, bn, transposed)
        rows = jnp.concatenate([x_ref[...], x2_ref[...]], axis=0)
        acc = jnp.dot(rows, w_ref[...], preferred_element_type=F32)
        epilogue.tile(acc[:bm], i, j, extra, outs, scratch)
        epilogue.rows(acc[bm:], j, extra, extra2, outs2)

    @pl.when(i != 0)
    def _():
        acc = jnp.dot(x_ref[...], w_ref[...], preferred_element_type=F32)
        epilogue.tile(acc, i, j, extra, outs, scratch)


class _Epilogue:
    def scratch_shapes(self, bm, bn):
        return []

    def init_scratch(self, scratch):
        pass


class _Weight(NamedTuple):
    arr: jax.Array
    layer: int = 0
    transposed: bool = False

    @property
    def k(self):
        return self.arr.shape[2 if self.transposed else 1]


def _matmul(x, x2, w, epilogue, *, w_col_blocks, bn, bm=MM_BLOCK_M, extra=(), extra_specs=(),
            extra2=(), extra2_specs=(), out_shapes, out_specs, out2_shapes, out2_specs, name):
    m, k = x.shape
    m2 = x2.shape[0]
    bm = min(bm, m)
    assert m % bm == 0 and w.k == k and x2.shape[1] == k
    cast = w.arr.dtype != BF16
    assert cast or not w.transposed
    n_w = len(w_col_blocks)
    n_j = epilogue.n_col_steps
    in_specs = [pl.BlockSpec((bm, k), lambda j, i: (i, 0)),
                pl.BlockSpec((m2, k), lambda j, i: (0, 0))]
    for f in w_col_blocks:
        if w.transposed:
            in_specs.append(pl.BlockSpec((None, bn, k), functools.partial(
                lambda j, i, f: (w.layer, f(j), 0), f=f)))
        else:
            in_specs.append(pl.BlockSpec((None, k, bn), functools.partial(
                lambda j, i, f: (w.layer, 0, f(j)), f=f)))
    in_specs.extend(extra_specs)
    in_specs.extend(extra2_specs)
    scratch = []
    if cast:
        scratch.append(pltpu.VMEM((k, n_w * bn), BF16))
    scratch.extend(epilogue.scratch_shapes(bm, bn))
    counts = (len(extra), len(extra2), len(out_shapes), len(out2_shapes))
    kern = functools.partial(_mm_kernel, n_w=n_w, counts=counts, epilogue=epilogue, cast=cast,
                             transposed=w.transposed)
    res = pl.pallas_call(
        kern,
        grid=(n_j, m // bm),
        in_specs=in_specs,
        out_specs=list(out_specs) + list(out2_specs),
        out_shape=list(out_shapes) + list(out2_shapes),
        scratch_shapes=scratch,
        compiler_params=_params(("arbitrary", "arbitrary")),
        name=name,
    )(x, x2, *([w.arr] * n_w), *extra, *extra2)
    return res[:len(out_shapes)], res[len(out_shapes):]


class _StoreEpilogue(_Epilogue):
    def __init__(self, n_col_steps, scaled_col_steps=0, scale=1.0, residual=False):
        self.n_col_steps = n_col_steps
        self.scaled_col_steps = scaled_col_steps
        self.scale = scale
        self.residual = residual

    def _store(self, acc, j, res_refs, out_ref):
        if self.scaled_col_steps:
            acc = acc * jnp.where(j < self.scaled_col_steps, self.scale, 1.0).astype(F32)
        if self.residual:
            acc = res_refs[0][...] + acc
        out_ref[...] = acc.astype(out_ref.dtype)

    def tile(self, acc, i, j, extra, outs, scratch):
        self._store(acc, j, extra, outs[0])

    def rows(self, acc, j, extra, extra2, outs2):
        self._store(acc, j, extra2, outs2[0])


def _linear(x, x2, w, *, col0, n, out_dtype, bn=MM_BLOCK_N, scaled_cols=0, scale=1.0,
            residual=None, residual2=None, bm=MM_BLOCK_M, name):
    m, m2 = x.shape[0], x2.shape[0]
    bm = min(bm, m)
    assert n % bn == 0 and col0 % bn == 0 and scaled_cols % bn == 0
    assert (residual is None) == (residual2 is None)
    j0 = col0 // bn
    ep = _StoreEpilogue(n // bn, scaled_cols // bn, scale, residual is not None)
    tile_spec = pl.BlockSpec((bm, bn), lambda j, i: (i, j))
    rows_spec = pl.BlockSpec((m2, bn), lambda j, i: (0, j))
    has_res = residual is not None
    (out,), (out2,) = _matmul(
        x, x2, w, ep, w_col_blocks=[lambda j: j + j0], bn=bn, bm=bm,
        extra=(residual,) if has_res else (), extra_specs=(tile_spec,) if has_res else (),
        extra2=(residual2,) if has_res else (), extra2_specs=(rows_spec,) if has_res else (),
        out_shapes=[jax.ShapeDtypeStruct((m, n), out_dtype)], out_specs=[tile_spec],
        out2_shapes=[jax.ShapeDtypeStruct((m2, n), out_dtype)], out2_specs=[rows_spec],
        name=name)
    return out, out2


class _RopeEpilogue(_Epilogue):
    def __init__(self, n_col_steps, rope_col_steps, n_out):
        self.n_col_steps = n_col_steps
        self.rope_col_steps = rope_col_steps
        self.n_out = n_out

    def _store(self, acc, j, tables, outs):
        cos_ref, sin_ref = tables

        def rotated():
            cos, sin = cos_ref[...], sin_ref[...]
            heads = []
            for h in range(acc.shape[1] // A_HEAD_DIM):
                xh = acc[:, h * A_HEAD_DIM:(h + 1) * A_HEAD_DIM]
                heads.append(xh * cos + pltpu.roll(xh, A_HEAD_DIM // 2, axis=1) * sin)
            return jnp.concatenate(heads, axis=1)

        if self.rope_col_steps >= self.n_col_steps:
            val = rotated()
        else:
            val = jnp.where(j < self.rope_col_steps, rotated(), acc)
        for o_ref in outs:
            o_ref[...] = val.astype(o_ref.dtype)

    def tile(self, acc, i, j, extra, outs, scratch):
        self._store(acc, j, extra, outs)

    def rows(self, acc, j, extra, extra2, outs2):
        self._store(acc, j, extra2, outs2)


def _rope_linear(x, x2, w, tables, tables2, *, n, rope_cols, out_dtypes, rows_per_seq, name,
                 bn=MM_BLOCK_N):
    m, m2 = x.shape[0], x2.shape[0]
    bm = min(MM_BLOCK_M, m, rows_per_seq)
    assert rows_per_seq % bm == 0 and n % bn == 0 and rope_cols % bn == 0
    tiles_per_seq = rows_per_seq // bm
    ep = _RopeEpilogue(n // bn, rope_cols // bn, len(out_dtypes))
    tab_spec = pl.BlockSpec((bm, A_HEAD_DIM), lambda j, i: (i % tiles_per_seq, 0))
    tab2_spec = pl.BlockSpec((m2, A_HEAD_DIM), lambda j, i: (0, 0))
    return _matmul(
        x, x2, w, ep, w_col_blocks=[lambda j: j], bn=bn, bm=bm,
        extra=tables, extra_specs=(tab_spec, tab_spec),
        extra2=tables2, extra2_specs=(tab2_spec, tab2_spec),
        out_shapes=[jax.ShapeDtypeStruct((m, n), dt) for dt in out_dtypes],
        out_specs=[pl.BlockSpec((bm, bn), lambda j, i: (i, j)) for _ in out_dtypes],
        out2_shapes=[jax.ShapeDtypeStruct((m2, n), dt) for dt in out_dtypes],
        out2_specs=[pl.BlockSpec((m2, bn), lambda j, i: (0, j)) for _ in out_dtypes],
        name=name)


def _gates_kernel(x_ref, wt_ref, b_ref, o_ref):
    g = lax.dot_general(x_ref[...], wt_ref[...].astype(BF16), (((1,), (1,)), ((), ())),
                        preferred_element_type=F32) + b_ref[...]
    ig = GATE_SOFT_CAP * jnp.tanh(g / GATE_SOFT_CAP)
    lf = jnp.minimum(g, 0.0) - jnp.log1p(jnp.exp(-jnp.abs(g)))
    lane = lax.broadcasted_iota(jnp.int32, g.shape, 1)
    o_ref[...] = jnp.where(lane < M_HEADS, ig, lf)


def _mlstm_gates(xn, wt_gates, b_gates):
    m, d = xn.shape
    bm = min(MM_BLOCK_M, m)
    ng = 2 * M_HEADS
    return pl.pallas_call(
        _gates_kernel,
        grid=(m // bm,),
        in_specs=[pl.BlockSpec((bm, d), lambda i: (i, 0)),
                  pl.BlockSpec((ng, d), lambda i: (0, 0)),
                  pl.BlockSpec((1, ng), lambda i: (0, 0))],
        out_specs=pl.BlockSpec((bm, ng), lambda i: (i, 0)),
        out_shape=jax.ShapeDtypeStruct((m, ng), F32),
        compiler_params=_params(("arbitrary",)),
        name="mlstm_gates",
    )(xn, wt_gates, b_gates.reshape(1, ng).astype(F32))


def _mlstm_chunk_kernel(q_ref, k_ref, v_ref, o_ref, gcol_ref, grow_ref, c0_ref, n0_ref, m0_ref,
                        h_ref, c_ref, n_ref, m_ref):
    L = q_ref.shape[0]

    @pl.when(pl.program_id(1) == 0)
    def _():
        c_ref[...] = c0_ref[...]
        n_ref[...] = n0_ref[...]
        m_ref[...] = m0_ref[...]

    t_idx = lax.broadcasted_iota(jnp.int32, (L, L), 0)
    s_idx = lax.broadcasted_iota(jnp.int32, (L, L), 1)
    causal = s_idx <= t_idx
    for h in range(M_HEADS):
        q = q_ref[:, h * M_DK:(h + 1) * M_DK]
        k = k_ref[:, h * M_DK:(h + 1) * M_DK]
        v = v_ref[:, h * M_DV:(h + 1) * M_DV]
        ig_col = gcol_ref[:, h:h + 1]
        lf_col = gcol_ref[:, M_HEADS + h:M_HEADS + h + 1]
        ig_row = grow_ref[h:h + 1, :]
        lf_row = grow_ref[M_HEADS + h:M_HEADS + h + 1, :]
        c_prev = c_ref[0, h]
        n_prev = n_ref[0, h:h + 1, :]
        m_prev = m_ref[0, h:h + 1, 0:1]

        b_col = jnp.sum(jnp.where(causal, lf_row, 0.0), axis=1, keepdims=True)
        b_row = jnp.sum(jnp.where(t_idx <= s_idx, lf_col, 0.0), axis=0, keepdims=True)
        dlog = jnp.where(causal, b_col - b_row + ig_row, -jnp.inf)
        g = b_col + m_prev
        m_t = jnp.maximum(g, jnp.max(dlog, axis=1, keepdims=True))
        w_intra = jnp.exp(dlog - m_t)
        w_inter = jnp.exp(g - m_t)

        qk = lax.dot_general(q, k, (((1,), (1,)), ((), ())), preferred_element_type=F32)
        s = qk * w_intra
        num = (jnp.dot(s.astype(BF16), v, preferred_element_type=F32)
               + w_inter * jnp.dot(q, c_prev.astype(BF16), preferred_element_type=F32))
        qn = jnp.sum(q.astype(F32) * n_prev, axis=1, keepdims=True)
        den = jnp.sum(s, axis=1, keepdims=True) + w_inter * qn
        hh = num / jnp.maximum(jnp.abs(den), jnp.exp(-m_t))
        gate = jax.nn.sigmoid(o_ref[:, h * M_DV:(h + 1) * M_DV])
        h_ref[:, h * M_DV:(h + 1) * M_DV] = (gate * hh).astype(h_ref.dtype)

        b_last = b_col[L - 1:L, :]
        m_last = m_t[L - 1:L, :]
        decay = w_inter[L - 1:L, :]
        w_end = jnp.exp(b_last - b_col + ig_col - m_last)
        kw = k.astype(F32) * w_end
        c_ref[0, h] = decay * c_prev + lax.dot_general(
            kw.astype(BF16), v, (((0,), (0,)), ((), ())), preferred_element_type=F32)
        n_ref[0, h:h + 1, :] = decay * n_prev + jnp.sum(kw, axis=0, keepdims=True)
        m_ref[0, h:h + 1, :] = jnp.broadcast_to(m_last, (1, LANES))


def _mlstm_chunks(qkv, o_pre, gates, c0, n0, m0, batch, seq):
    L = M_CHUNK
    assert seq % L == 0
    nc = seq // L
    qk_w = M_HEADS * M_DK
    v_w = M_HEADS * M_DV
    m_rows = batch * seq
    grow = gates.T
    m0b = jnp.broadcast_to(m0[:, :, None], (batch, M_HEADS, LANES))
    row = lambda b, c: b * nc + c
    hg, c_new, n_new, m_new = pl.pallas_call(
        _mlstm_chunk_kernel,
        grid=(batch, nc),
        in_specs=[
            pl.BlockSpec((L, qk_w), lambda b, c: (row(b, c), 0)),
            pl.BlockSpec((L, qk_w), lambda b, c: (row(b, c), 1)),
            pl.BlockSpec((L, v_w), lambda b, c: (row(b, c), 1)),
            pl.BlockSpec((L, v_w), lambda b, c: (row(b, c), 0)),
            pl.BlockSpec((L, 2 * M_HEADS), lambda b, c: (row(b, c), 0)),
            pl.BlockSpec((2 * M_HEADS, L), lambda b, c: (0, row(b, c))),
            pl.BlockSpec((1, M_HEADS, M_DK, M_DV), lambda b, c: (b, 0, 0, 0)),
            pl.BlockSpec((1, M_HEADS, M_DK), lambda b, c: (b, 0, 0)),
            pl.BlockSpec((1, M_HEADS, LANES), lambda b, c: (b, 0, 0)),
        ],
        out_specs=[
            pl.BlockSpec((L, v_w), lambda b, c: (row(b, c), 0)),
            pl.BlockSpec((1, M_HEADS, M_DK, M_DV), lambda b, c: (b, 0, 0, 0)),
            pl.BlockSpec((1, M_HEADS, M_DK), lambda b, c: (b, 0, 0)),
            pl.BlockSpec((1, M_HEADS, LANES), lambda b, c: (b, 0, 0)),
        ],
        out_shape=[
            jax.ShapeDtypeStruct((m_rows, v_w), BF16),
            jax.ShapeDtypeStruct((batch, M_HEADS, M_DK, M_DV), F32),
            jax.ShapeDtypeStruct((batch, M_HEADS, M_DK), F32),
            jax.ShapeDtypeStruct((batch, M_HEADS, LANES), F32),
        ],
        compiler_params=_params(("arbitrary", "arbitrary")),
        name="mlstm_chunks",
    )(qkv, qkv, qkv, o_pre, gates, grow, c0, n0, m0b)
    return hg, c_new, n_new, m_new[:, :, 0]


def _mlstm_step_kernel(q_ref, k_ref, v_ref, o_ref, g_ref, c0_ref, n0_ref, m0_ref,
                       h_ref, c_ref, n_ref, m_ref):
    dk_idx = lax.broadcasted_iota(jnp.int32, (M_DK, M_DK), 0)
    dk_lane = lax.broadcasted_iota(jnp.int32, (M_DK, M_DK), 1)
    eye = dk_idx == dk_lane
    m_rows = []
    for h in range(M_HEADS):
        q = q_ref[0, :, h * M_DK:(h + 1) * M_DK]
        k = k_ref[0, :, h * M_DK:(h + 1) * M_DK]
        v = v_ref[0, :, h * M_DV:(h + 1) * M_DV]
        ig = g_ref[0, :, h:h + 1]
        lf = g_ref[0, :, M_HEADS + h:M_HEADS + h + 1]
        c_prev = c0_ref[0, h]
        n_prev = n0_ref[0, h:h + 1, :]
        m_prev = m0_ref[0, h:h + 1, 0:1]
        qf, kf, vf = q.astype(F32), k.astype(F32), v.astype(F32)

        g = lf + m_prev
        m_t = jnp.maximum(g, ig)
        w_intra = jnp.exp(ig - m_t)
        w_inter = jnp.exp(g - m_t)
        s = jnp.sum(qf * kf, axis=1, keepdims=True) * w_intra
        q_col = jnp.sum(jnp.where(eye, qf, 0.0), axis=1, keepdims=True)
        qc = jnp.sum(q_col * c_prev, axis=0, keepdims=True)
        num = s * vf + w_inter * qc
        den = s + w_inter * jnp.sum(qf * n_prev, axis=1, keepdims=True)
        hh = num / jnp.maximum(jnp.abs(den), jnp.exp(-m_t))
        gate = jax.nn.sigmoid(o_ref[0, :, h * M_DV:(h + 1) * M_DV])
        h_ref[0, :, h * M_DV:(h + 1) * M_DV] = (gate * hh).astype(h_ref.dtype)

        kw = kf * w_intra
        kw_col = jnp.sum(jnp.where(eye, kw, 0.0), axis=1, keepdims=True)
        c_ref[0, h] = w_inter * c_prev + kw_col * vf
        n_ref[0, h:h + 1, :] = w_inter * n_prev + kw
        m_rows.append(jnp.broadcast_to(m_t, (1, LANES)))
    m_ref[0] = jnp.concatenate(m_rows, axis=0)


def _mlstm_step(qkv, o_pre, gates, c0, n0, m0):
    batch = qkv.shape[0]
    qk_w = M_HEADS * M_DK
    v_w = M_HEADS * M_DV
    qkv3 = qkv.reshape(batch, 1, 2 * qk_w + v_w)
    m0b = jnp.broadcast_to(m0[:, :, None], (batch, M_HEADS, LANES))
    hg, c_new, n_new, m_new = pl.pallas_call(
        _mlstm_step_kernel,
        grid=(batch,),
        in_specs=[
            pl.BlockSpec((1, 1, qk_w), lambda b: (b, 0, 0)),
            pl.BlockSpec((1, 1, qk_w), lambda b: (b, 0, 1)),
            pl.BlockSpec((1, 1, v_w), lambda b: (b, 0, 1)),
            pl.BlockSpec((1, 1, v_w), lambda b: (b, 0, 0)),
            pl.BlockSpec((1, 1, 2 * M_HEADS), lambda b: (b, 0, 0)),
            pl.BlockSpec((1, M_HEADS, M_DK, M_DV), lambda b: (b, 0, 0, 0)),
            pl.BlockSpec((1, M_HEADS, M_DK), lambda b: (b, 0, 0)),
            pl.BlockSpec((1, M_HEADS, LANES), lambda b: (b, 0, 0)),
        ],
        out_specs=[
            pl.BlockSpec((1, 1, v_w), lambda b: (b, 0, 0)),
            pl.BlockSpec((1, M_HEADS, M_DK, M_DV), lambda b: (b, 0, 0, 0)),
            pl.BlockSpec((1, M_HEADS, M_DK), lambda b: (b, 0, 0)),
            pl.BlockSpec((1, M_HEADS, LANES), lambda b: (b, 0, 0)),
        ],
        out_shape=[
            jax.ShapeDtypeStruct((batch, 1, v_w), BF16),
            jax.ShapeDtypeStruct((batch, M_HEADS, M_DK, M_DV), F32),
            jax.ShapeDtypeStruct((batch, M_HEADS, M_DK), F32),
            jax.ShapeDtypeStruct((batch, M_HEADS, LANES), F32),
        ],
        compiler_params=_params(("arbitrary",)),
        name="mlstm_step",
    )(qkv3, qkv3, qkv3, o_pre.reshape(batch, 1, v_w), gates.reshape(batch, 1, 2 * M_HEADS),
      c0, n0, m0b)
    return hg.reshape(batch, v_w), c_new, n_new, m_new[:, :, 0]


def _softmax_with_sink(scores, sink):
    mx = jnp.maximum(jnp.max(scores, axis=-1, keepdims=True), sink)
    e = jnp.exp(scores - mx)
    return e / (jnp.sum(e, axis=-1, keepdims=True) + jnp.exp(sink - mx))


def _swa_kernel(sink_ref, q_ref, kp_ref, ko_ref, vp_ref, vo_ref, o_ref):
    W = WINDOW
    D = A_HEAD_DIM
    blk = pl.program_id(1)
    i_idx = lax.broadcasted_iota(jnp.int32, (W, 2 * W), 0)
    j_idx = lax.broadcasted_iota(jnp.int32, (W, 2 * W), 1)
    first_key = jnp.where(blk > 0, 0, W)
    mask = (j_idx >= jnp.maximum(i_idx, first_key)) & (j_idx <= i_idx + W)
    for kv in range(A_KV_HEADS):
        cols = slice(kv * D, (kv + 1) * D)
        k_cat = jnp.concatenate([kp_ref[:, cols], ko_ref[:, cols]], axis=0)
        v_cat = jnp.concatenate([vp_ref[:, cols], vo_ref[:, cols]], axis=0)
        q = jnp.concatenate([q_ref[:, (kv * A_GROUP + g) * D:(kv * A_GROUP + g + 1) * D]
                             for g in range(A_GROUP)], axis=0)
        scores = lax.dot_general(q, k_cat, (((1,), (1,)), ((), ())), preferred_element_type=F32)
        scores = scores * (A_HEAD_DIM ** -0.5)
        probs = []
        for g in range(A_GROUP):
            sc = jnp.where(mask, scores[g * W:(g + 1) * W, :], NEG_INF)
            probs.append(_softmax_with_sink(sc, sink_ref[kv * A_GROUP + g]).astype(BF16))
        p = jnp.concatenate(probs, axis=0)
        o = jnp.dot(p, v_cat, preferred_element_type=F32)
        for g in range(A_GROUP):
            hq = kv * A_GROUP + g
            o_ref[:, hq * D:(hq + 1) * D] = o[g * W:(g + 1) * W, :].astype(o_ref.dtype)


def _swa_attention(q, kv, sinks, batch, seq):
    W = WINDOW
    assert seq % W == 0
    nb = seq // W
    kvw = A_KV_HEADS * A_HEAD_DIM
    qw = A_HEADS * A_HEAD_DIM
    row = lambda b, n: b * nb + n
    prev = lambda b, n: jnp.maximum(b * nb + n - 1, 0)
    return pl.pallas_call(
        _swa_kernel,
        grid=(batch, nb),
        in_specs=[
            pl.BlockSpec(memory_space=pltpu.SMEM),
            pl.BlockSpec((W, qw), lambda b, n: (row(b, n), 0)),
            pl.BlockSpec((W, kvw), lambda b, n: (prev(b, n), 0)),
            pl.BlockSpec((W, kvw), lambda b, n: (row(b, n), 0)),
            pl.BlockSpec((W, kvw), lambda b, n: (prev(b, n), 1)),
            pl.BlockSpec((W, kvw), lambda b, n: (row(b, n), 1)),
        ],
        out_specs=pl.BlockSpec((W, qw), lambda b, n: (row(b, n), 0)),
        out_shape=jax.ShapeDtypeStruct((batch * seq, qw), BF16),
        compiler_params=_params(("arbitrary", "arbitrary")),
        name="swa_attention",
    )(sinks.astype(F32), q, kv, kv, kv, kv)


def _decode_attn_kernel(sink_ref, q_ref, kn_ref, vn_ref, ck_ref, cv_ref, o_ref, kw_ref, vw_ref):
    W = WINDOW
    D = A_HEAD_DIM
    R = SUBLANES
    row_idx = lax.broadcasted_iota(jnp.int32, (R, 1), 0)
    for kv in range(A_KV_HEADS):
        kc = ck_ref[0, :, kv * D:(kv + 1) * D].astype(BF16)
        vc = cv_ref[0, :, kv * D:(kv + 1) * D].astype(BF16)
        kn = kn_ref[0, :, kv * D:(kv + 1) * D].astype(BF16).astype(F32)
        vn = vn_ref[0, :, kv * D:(kv + 1) * D].astype(BF16).astype(F32)
        qf = jnp.zeros((R, D), F32)
        sink = jnp.zeros((R, 1), F32)
        for g in range(A_GROUP):
            hq = kv * A_GROUP + g
            qf = jnp.where(row_idx == g, q_ref[0, :, hq * D:(hq + 1) * D].astype(F32), qf)
            sink = jnp.where(row_idx == g, sink_ref[hq], sink)
        scale = A_HEAD_DIM ** -0.5
        s_c = lax.dot_general(qf.astype(BF16), kc, (((1,), (1,)), ((), ())),
                              preferred_element_type=F32) * scale
        s_n = jnp.sum(qf * kn, axis=1, keepdims=True) * scale
        mx = jnp.maximum(jnp.maximum(jnp.max(s_c, axis=1, keepdims=True), s_n), sink)
        e_c = jnp.exp(s_c - mx)
        e_n = jnp.exp(s_n - mx)
        den = jnp.sum(e_c, axis=1, keepdims=True) + e_n + jnp.exp(sink - mx)
        p_c = (e_c / den).astype(BF16)
        p_n = (e_n / den).astype(BF16).astype(F32)
        o = jnp.dot(p_c, vc, preferred_element_type=F32) + p_n * vn
        for g in range(A_GROUP):
            o_ref[0, :, (kv * A_GROUP + g) * D:(kv * A_GROUP + g + 1) * D] = o[g:g + 1, :].astype(o_ref.dtype)
    last = lax.broadcasted_iota(jnp.int32, (W, 1), 0) == W - 1
    kw_ref[0] = jnp.where(last, kn_ref[0], pltpu.roll(ck_ref[0], W - 1, axis=0))
    vw_ref[0] = jnp.where(last, vn_ref[0], pltpu.roll(cv_ref[0], W - 1, axis=0))


def _decode_attention(q, k_new, v_new, cache_k, cache_v, sinks):
    batch = q.shape[0]
    kvw = A_KV_HEADS * A_HEAD_DIM
    qw = A_HEADS * A_HEAD_DIM
    o, k_win, v_win = pl.pallas_call(
        _decode_attn_kernel,
        grid=(batch,),
        in_specs=[
            pl.BlockSpec(memory_space=pltpu.SMEM),
            pl.BlockSpec((1, 1, qw), lambda b: (b, 0, 0)),
            pl.BlockSpec((1, 1, kvw), lambda b: (b, 0, 0)),
            pl.BlockSpec((1, 1, kvw), lambda b: (b, 0, 0)),
            pl.BlockSpec((1, WINDOW, kvw), lambda b: (b, 0, 0)),
            pl.BlockSpec((1, WINDOW, kvw), lambda b: (b, 0, 0)),
        ],
        out_specs=[
            pl.BlockSpec((1, 1, qw), lambda b: (b, 0, 0)),
            pl.BlockSpec((1, WINDOW, kvw), lambda b: (b, 0, 0)),
            pl.BlockSpec((1, WINDOW, kvw), lambda b: (b, 0, 0)),
        ],
        out_shape=[
            jax.ShapeDtypeStruct((batch, 1, qw), BF16),
            jax.ShapeDtypeStruct((batch, WINDOW, kvw), F32),
            jax.ShapeDtypeStruct((batch, WINDOW, kvw), F32),
        ],
        compiler_params=_params(("arbitrary",)),
        name="decode_attention",
    )(sinks.astype(F32), q.reshape(batch, 1, qw), k_new.reshape(batch, 1, kvw),
      v_new.reshape(batch, 1, kvw), cache_k, cache_v)
    return o.reshape(batch, qw), k_win, v_win


def _conv_act(g2, g1, g0, val, cw_ref, cb_ref):
    conv = cb_ref[...] + cw_ref[0:1, :] * g2
    conv = conv + cw_ref[1:2, :] * g1
    conv = conv + cw_ref[2:3, :] * g0
    return conv * jax.nn.sigmoid(conv) * val


class _ConvFfnEpilogue(_Epilogue):
    def __init__(self, n_col_steps, tiles_per_seq):
        self.n_col_steps = n_col_steps
        self.tiles_per_seq = tiles_per_seq

    def scratch_shapes(self, bm, bn):
        return [pltpu.VMEM((SUBLANES, bn), F32)]

    def init_scratch(self, scratch):
        scratch[0][...] = jnp.zeros_like(scratch[0])

    def rows(self, acc, j, extra, extra2, outs2):
        cw_ref, cb_ref, _ = extra
        g2_ref, g1_ref = extra2
        act_ref, gate_ref = outs2
        bn = acc.shape[1] // 2
        gate, val = acc[:, :bn], acc[:, bn:]
        act_ref[...] = _conv_act(g2_ref[...], g1_ref[...], gate, val, cw_ref, cb_ref).astype(act_ref.dtype)
        gate_ref[...] = gate

    def tile(self, acc, i, j, extra, outs, scratch):
        cw_ref, cb_ref, halo0_ref = extra
        act_ref, tail_ref = outs
        carry_ref, = scratch
        bm = acc.shape[0]
        bn = acc.shape[1] // 2
        gate, val = acc[:, :bn], acc[:, bn:]
        seq_start = (i % self.tiles_per_seq) == 0
        halo = jnp.where(seq_start, halo0_ref[0], carry_ref[...])
        ext = jnp.concatenate([halo, gate], axis=0)
        g1 = ext[SUBLANES - 1:SUBLANES - 1 + bm, :]
        g2 = ext[SUBLANES - 2:SUBLANES - 2 + bm, :]
        act_ref[...] = _conv_act(g2, g1, gate, val, cw_ref, cb_ref).astype(act_ref.dtype)
        tail = gate[bm - SUBLANES:, :]
        carry_ref[...] = tail
        tail_ref[0] = tail


def _conv_ffn_up(xn, xn2, w_up, conv_w, conv_b, conv0, conv0_2, batch, seq):
    m, m2 = xn.shape[0], xn2.shape[0]
    bn = FFN_BLOCK_N
    assert D_FF % bn == 0
    n_j = D_FF // bn
    cw = conv_w.astype(F32)
    cb = conv_b.reshape(1, D_FF).astype(F32)
    col_spec = lambda rows: pl.BlockSpec((rows, bn), lambda j, i: (0, j))
    bm = min(MM_BLOCK_M, seq)
    assert seq % bm == 0 and bm % SUBLANES == 0
    tiles_per_seq = seq // bm
    n_i = m // bm
    halo0 = jnp.concatenate(
        [jnp.zeros((batch, SUBLANES - (CONV_W - 1), D_FF), F32), conv0.astype(F32)], axis=1)
    g2, g1 = conv0_2[:, 0, :], conv0_2[:, 1, :]
    rows_spec = pl.BlockSpec((m2, bn), lambda j, i: (0, j))
    (act, tails), (act2, gate2) = _matmul(
        xn, xn2, w_up, _ConvFfnEpilogue(n_j, tiles_per_seq),
        w_col_blocks=[lambda j: j, lambda j: j + n_j], bn=bn, bm=bm,
        extra=(cw, cb, halo0),
        extra_specs=(col_spec(CONV_W), col_spec(1),
                     pl.BlockSpec((1, SUBLANES, bn), lambda j, i: (i // tiles_per_seq, 0, j))),
        extra2=(g2, g1), extra2_specs=(rows_spec, rows_spec),
        out_shapes=[jax.ShapeDtypeStruct((m, D_FF), BF16),
                    jax.ShapeDtypeStruct((n_i, SUBLANES, D_FF), F32)],
        out_specs=[pl.BlockSpec((bm, bn), lambda j, i: (i, j)),
                   pl.BlockSpec((1, SUBLANES, bn), lambda j, i: (i, 0, j))],
        out2_shapes=[jax.ShapeDtypeStruct((m2, D_FF), BF16), jax.ShapeDtypeStruct((m2, D_FF), F32)],
        out2_specs=[rows_spec, rows_spec],
        name="conv_ffn_up")
    last_tiles = tails.reshape(batch, tiles_per_seq, SUBLANES, D_FF)[:, -1]
    return ((act, last_tiles[:, SUBLANES - (CONV_W - 1):, :]),
            (act2, jnp.stack([g1, gate2], axis=1)))


def _conv_ffn(h, h2, gain, w_up, conv_w, conv_b, w_down_bf16, conv0, conv0_2, batch, seq):
    gain = gain.reshape(1, D_MODEL)
    xn, = _rms_norm(h, gain, BF16)
    xn2, = _rms_norm(h2, gain, BF16)
    (act, conv_new), (act2, conv_new2) = _conv_ffn_up(xn, xn2, w_up, conv_w, conv_b, conv0, conv0_2,
                                                      batch, seq)
    out, out2 = _linear(act, act2, w_down_bf16, col0=0, n=D_MODEL, out_dtype=F32, bn=DOWN_BLOCK_N,
                        bm=DOWN_BLOCK_M, residual=h, residual2=h2, name="ffn_down")
    return (out, conv_new), (out2, conv_new2)


def _rope_tables(pos):
    inv_freq = ROPE_THETA ** (-jnp.arange(0, A_HEAD_DIM, 2, dtype=F32) / A_HEAD_DIM)
    ang = pos.astype(F32)[:, None] * inv_freq[None, :]
    cos, sin = jnp.cos(ang), jnp.sin(ang)
    return jnp.concatenate([cos, cos], axis=1), jnp.concatenate([-sin, sin], axis=1)


def _model(x_p, x_s, state_p, state_s, k_win0, v_win0, w):
    batch, seq, _ = x_p.shape
    batch2 = x_s.shape[0]
    assert x_s.shape[1] == 1
    hp = x_p.reshape(batch * seq, D_MODEL)
    hs = x_s.reshape(batch2, D_MODEL)
    c0_p, n0_p, m0_p, conv0_p = state_p
    c0_s, n0_s, m0_s, conv0_s = state_s
    qk_w = M_HEADS * M_DK
    v_w = M_HEADS * M_DV
    kv_w = A_KV_HEADS * A_HEAD_DIM
    q_w = A_HEADS * A_HEAD_DIM

    g_mix0 = w["g_mix"][0:1]
    hn_p, = _rms_norm(hp, g_mix0, BF16)
    hn_s, = _rms_norm(hs, g_mix0, BF16)
    wt_in = _Weight(w["wt_mlstm_in"], 0, transposed=True)
    qkv_p, qkv_s = _linear(hn_p, hn_s, wt_in, col0=0, n=2 * qk_w + v_w, out_dtype=BF16,
                           scaled_cols=qk_w, scale=M_DK ** -0.5, name="mlstm_qkv")
    o_p, o_s = _linear(hn_p, hn_s, wt_in, col0=2 * qk_w + v_w, n=v_w, out_dtype=F32, name="mlstm_ogate")
    wt_gates = w["wt_mlstm_in"][0, 2 * qk_w + 2 * v_w:, :]
    gates_p = _mlstm_gates(hn_p, wt_gates, w["b_mlstm_gates"][0])
    gates_s = _mlstm_gates(hn_s, wt_gates, w["b_mlstm_gates"][0])
    hg_p, c_p, n_p, m_p = _mlstm_chunks(qkv_p, o_p, gates_p, c0_p[0], n0_p[0], m0_p[0], batch, seq)
    hg_s, c_s, n_s, m_s = _mlstm_step(qkv_s, o_s, gates_s, c0_s[0], n0_s[0], m0_s[0])
    hp, hs = _linear(hg_p, hg_s, _Weight(w["w_mlstm_out"], 0), col0=0, n=D_MODEL, out_dtype=F32,
                     residual=hp, residual2=hs, name="mlstm_out")
    (hp, conv_a_p), (hs, conv_a_s) = _conv_ffn(
        hp, hs, w["g_ffn"][0], _Weight(w["w_ffn_up"], 0), w["ffn_conv_w"][0], w["ffn_conv_b"][0],
        _Weight(w["w_ffn_down_bf16"], 0), conv0_p[0], conv0_s[0], batch, seq)

    g_kv_mix = jnp.stack([w["g_kv"], w["g_mix"][1]])
    kvn_p, hn_p = _rms_norm(hp, g_kv_mix, BF16)
    kvn_s, hn_s = _rms_norm(hs, g_kv_mix, BF16)
    tab_p = _rope_tables(jnp.arange(seq))
    tab_s = tuple(jnp.broadcast_to(t, (batch2, A_HEAD_DIM)) for t in _rope_tables(PAST_LEN + jnp.arange(1)))
    (kv_bf_p, kv_f32_p), (_, kv_f32_s) = _rope_linear(
        kvn_p, kvn_s, _Weight(w["w_kv"][None], 0), tab_p, tab_s, n=2 * kv_w, rope_cols=kv_w,
        out_dtypes=(BF16, F32), rows_per_seq=seq, name="kv_proj")
    (q_p,), (q_s,) = _rope_linear(
        hn_p, hn_s, _Weight(w["w_attn_q"], 0), tab_p, tab_s, n=q_w, rope_cols=q_w,
        out_dtypes=(BF16,), rows_per_seq=seq, name="q_proj")
    sinks = w["attn_sinks"][0]
    att_p = _swa_attention(q_p, kv_bf_p, sinks, batch, seq)
    kv4 = kv_f32_p.reshape(batch, seq, 2, A_KV_HEADS, A_HEAD_DIM)
    k_win_p, v_win_p = kv4[:, -WINDOW:, 0], kv4[:, -WINDOW:, 1]
    att_s, k_win_s, v_win_s = _decode_attention(
        q_s, kv_f32_s[:, :kv_w], kv_f32_s[:, kv_w:], k_win0.reshape(batch2, WINDOW, kv_w),
        v_win0.reshape(batch2, WINDOW, kv_w), sinks)
    k_win_s = k_win_s.reshape(batch2, WINDOW, A_KV_HEADS, A_HEAD_DIM)
    v_win_s = v_win_s.reshape(batch2, WINDOW, A_KV_HEADS, A_HEAD_DIM)
    hp, hs = _linear(att_p, att_s, _Weight(w["w_attn_o"], 0), col0=0, n=D_MODEL, out_dtype=F32,
                     residual=hp, residual2=hs, name="attn_out")
    (hp, conv_b_p), (hs, conv_b_s) = _conv_ffn(
        hp, hs, w["g_ffn"][1], _Weight(w["w_ffn_up"], 1), w["ffn_conv_w"][1], w["ffn_conv_b"][1],
        _Weight(w["w_ffn_down_bf16"], 1), conv0_p[1], conv0_s[1], batch, seq)
    g_final = w["g_final"].reshape(1, D_MODEL)
    y_p, = _rms_norm(hp, g_final, F32)
    y_s, = _rms_norm(hs, g_final, F32)
    out_p = (y_p.reshape(batch, seq, D_MODEL), c_p[None], n_p[None], m_p[None], k_win_p, v_win_p,
             jnp.stack([conv_a_p, conv_b_p]))
    out_s = (y_s.reshape(batch2, 1, D_MODEL), c_s[None], n_s[None], m_s[None], k_win_s, v_win_s,
             jnp.stack([conv_a_s, conv_b_s]))
    return out_p, out_s


def kernel(x_prompt, x_sample, state_mlstm_C, state_mlstm_n, state_mlstm_m, cache_k_win, cache_v_win,
           state_ffn_conv, g_mix, g_ffn, w_mlstm_in, b_mlstm_gates, w_mlstm_out, g_kv, w_kv, w_attn_q,
           attn_sinks, w_attn_o, w_ffn_up, ffn_conv_w, ffn_conv_b, w_ffn_down, g_final):
    w = dict(g_mix=g_mix, g_ffn=g_ffn, wt_mlstm_in=jnp.swapaxes(w_mlstm_in, 1, 2),
             b_mlstm_gates=b_mlstm_gates,
             w_mlstm_out=w_mlstm_out, g_kv=g_kv, w_kv=w_kv, w_attn_q=w_attn_q, attn_sinks=attn_sinks,
             w_attn_o=w_attn_o, w_ffn_up=w_ffn_up, ffn_conv_w=ffn_conv_w, ffn_conv_b=ffn_conv_b,
             w_ffn_down_bf16=w_ffn_down.astype(BF16), g_final=g_final)
    bp = x_prompt.shape[0]
    n_a = state_mlstm_C.shape[0]
    depth = state_ffn_conv.shape[0]
    state_p = (jnp.zeros((n_a, bp, M_HEADS, M_DK, M_DV), F32), jnp.zeros((n_a, bp, M_HEADS, M_DK), F32),
               jnp.zeros((n_a, bp, M_HEADS), F32), jnp.zeros((depth, bp, CONV_W - 1, D_FF), F32))
    state_s = (state_mlstm_C, state_mlstm_n, state_mlstm_m, state_ffn_conv)
    prompt, sample = _model(x_prompt, x_sample, state_p, state_s, cache_k_win, cache_v_win, w)
    return (prompt[0], sample[0]) + prompt[1:] + sample[1:]
```

```python
import functools
import math
from typing import NamedTuple

import jax
import jax.numpy as jnp
from jax import lax
from jax.experimental import pallas as pl
from jax.experimental.pallas import tpu as pltpu

F32 = jnp.float32
BF16 = jnp.bfloat16

D_MODEL = 4096
PAST_LEN = 16384
M_HEADS = 8
M_DK = D_MODEL // (2 * M_HEADS)
M_DV = D_MODEL // M_HEADS
M_CHUNK = 128
GATE_SOFT_CAP = 15.0
A_HEADS = 32
A_KV_HEADS = 8
A_GROUP = A_HEADS // A_KV_HEADS
A_HEAD_DIM = D_MODEL // A_HEADS
WINDOW = 128
ROPE_THETA = 10000.0
NEG_INF = -1e30
D_FF = 11008
CONV_W = 3
RMS_EPS = 1e-6

V7X_VMEM_BYTES = 64 * 1024 * 1024
VMEM_LIMIT_BYTES = V7X_VMEM_BYTES - 8 * 1024 * 1024
LANES = 128
SUBLANES = 8

MM_BLOCK_M = 1024
MM_BLOCK_N = 512
WIDE_BLOCK_M = 512
WIDE_BLOCK_N = 1024
FFN_BLOCK_N = 256
DOWN_BLOCK_M = 512
DOWN_BLOCK_N = 512
RMS_BLOCK_M = 512
DECODE_ATTN_BLOCK_B = 4
CAST_ROWS = 256


def _params(sem):
    return pltpu.CompilerParams(dimension_semantics=sem, vmem_limit_bytes=VMEM_LIMIT_BYTES)


def _rms_kernel(x_ref, g_ref, *o_refs):
    xf = x_ref[...]
    r = lax.rsqrt(jnp.mean(xf * xf, axis=-1, keepdims=True) + RMS_EPS)
    xr = xf * r
    for k, o_ref in enumerate(o_refs):
        o_ref[...] = (xr * g_ref[k:k + 1, :]).astype(o_ref.dtype)


def _rms_norm(x, gains, out_dtype):
    m, d = x.shape
    n_g = gains.shape[0]
    bm = min(RMS_BLOCK_M, m)
    assert m % bm == 0
    outs = pl.pallas_call(
        _rms_kernel,
        grid=(m // bm,),
        in_specs=[pl.BlockSpec((bm, d), lambda i: (i, 0)),
                  pl.BlockSpec((n_g, d), lambda i: (0, 0))],
        out_specs=[pl.BlockSpec((bm, d), lambda i: (i, 0)) for _ in range(n_g)],
        out_shape=[jax.ShapeDtypeStruct((m, d), out_dtype) for _ in range(n_g)],
        compiler_params=_params(("arbitrary",)),
        name="rms_norm",
    )(x, gains)
    return outs


def _cast_weight(w_ref, wbf_ref, col0, ncols, transposed):
    k = wbf_ref.shape[0]
    rows_per = min(CAST_ROWS, k)
    assert k % rows_per == 0

    def body(r, carry):
        rows = pl.ds(pl.multiple_of(r * rows_per, rows_per), rows_per)
        chunk = w_ref[:, rows].T if transposed else w_ref[rows, :]
        wbf_ref[rows, col0:col0 + ncols] = chunk.astype(BF16)
        return carry

    lax.fori_loop(0, k // rows_per, body, 0)


def _mm_kernel(*refs, n_w, counts, epilogue, cast, transposed):
    n_extra, n_extra2, n_out, n_out2 = counts
    x_ref, x2_ref = refs[0], refs[1]
    refs = refs[2:]
    w_refs, refs = refs[:n_w], refs[n_w:]
    extra, refs = refs[:n_extra], refs[n_extra:]
    extra2, refs = refs[:n_extra2], refs[n_extra2:]
    outs, refs = refs[:n_out], refs[n_out:]
    outs2, scratch = refs[:n_out2], refs[n_out2:]
    j = pl.program_id(0)
    i = pl.program_id(1)
    bm = x_ref.shape[0]
    if cast:
        wbf_ref = scratch[0]
        scratch = scratch[1:]
    w_ref = wbf_ref if cast else w_refs[0]

    @pl.when(i == 0)
    def _():
        epilogue.init_scratch(scratch)
        if cast:
            bn = wbf_ref.shape[1] // n_w
            for t, wt_ref in enumerate(w_refs):
                _cast_weight(wt_ref, wbf_ref, t * bn, bn, transposed)
        rows = jnp.concatenate([x_ref[...], x2_ref[...]], axis=0)
        acc = jnp.dot(rows, w_ref[...], preferred_element_type=F32)
        epilogue.tile(acc[:bm], i, j, extra, outs, scratch)
        epilogue.rows(acc[bm:], j, extra, extra2, outs2)

    @pl.when(i != 0)
    def _():
        acc = jnp.dot(x_ref[...], w_ref[...], preferred_element_type=F32)
        epilogue.tile(acc, i, j, extra, outs, scratch)


class _Epilogue:
    def scratch_shapes(self, bm, bn):
        return []

    def init_scratch(self, scratch):
        pass


class _Weight(NamedTuple):
    arr: jax.Array
    layer: int = 0
    transposed: bool = False

    @property
    def k(self):
        return self.arr.shape[2 if self.transposed else 1]


def _matmul(x, x2, w, epilogue, *, w_col_blocks, bn, bm=MM_BLOCK_M, extra=(), extra_specs=(),
            extra2=(), extra2_specs=(), out_shapes, out_specs, out2_shapes, out2_specs, name,
            w_buffers=None):
    m, k = x.shape
    m2 = x2.shape[0]
    bm = min(bm, m)
    assert m % bm == 0 and w.k == k and x2.shape[1] == k
    cast = w.arr.dtype != BF16
    assert cast or not w.transposed
    n_w = len(w_col_blocks)
    n_j = epilogue.n_col_steps
    in_specs = [pl.BlockSpec((bm, k), lambda j, i: (i, 0)),
                pl.BlockSpec((m2, k), lambda j, i: (0, 0))]
    w_mode = {} if w_buffers is None else dict(pipeline_mode=pl.Buffered(w_buffers))
    for f in w_col_blocks:
        if w.transposed:
            in_specs.append(pl.BlockSpec((None, bn, k), functools.partial(
                lambda j, i, f: (w.layer, f(j), 0), f=f), **w_mode))
        else:
            in_specs.append(pl.BlockSpec((None, k, bn), functools.partial(
                lambda j, i, f: (w.layer, 0, f(j)), f=f), **w_mode))
    in_specs.extend(extra_specs)
    in_specs.extend(extra2_specs)
    scratch = []
    if cast:
        scratch.append(pltpu.VMEM((k, n_w * bn), BF16))
    scratch.extend(epilogue.scratch_shapes(bm, bn))
    counts = (len(extra), len(extra2), len(out_shapes), len(out2_shapes))
    kern = functools.partial(_mm_kernel, n_w=n_w, counts=counts, epilogue=epilogue, cast=cast,
                             transposed=w.transposed)
    res = pl.pallas_call(
        kern,
        grid=(n_j, m // bm),
        in_specs=in_specs,
        out_specs=list(out_specs) + list(out2_specs),
        out_shape=list(out_shapes) + list(out2_shapes),
        scratch_shapes=scratch,
        compiler_params=_params(("arbitrary", "arbitrary")),
        name=name,
    )(x, x2, *([w.arr] * n_w), *extra, *extra2)
    return res[:len(out_shapes)], res[len(out_shapes):]


class _StoreEpilogue(_Epilogue):
    def __init__(self, n_col_steps, scaled_col_steps=0, scale=1.0, residual=False):
        self.n_col_steps = n_col_steps
        self.scaled_col_steps = scaled_col_steps
        self.scale = scale
        self.residual = residual

    def _store(self, acc, j, res_refs, out_ref):
        if self.scaled_col_steps:
            acc = acc * jnp.where(j < self.scaled_col_steps, self.scale, 1.0).astype(F32)
        if self.residual:
            acc = res_refs[0][...] + acc
        out_ref[...] = acc.astype(out_ref.dtype)

    def tile(self, acc, i, j, extra, outs, scratch):
        self._store(acc, j, extra, outs[0])

    def rows(self, acc, j, extra, extra2, outs2):
        self._store(acc, j, extra2, outs2[0])


def _linear(x, x2, w, *, col0, n, out_dtype, bn=MM_BLOCK_N, scaled_cols=0, scale=1.0,
            residual=None, residual2=None, bm=MM_BLOCK_M, w_buffers=None, name):
    m, m2 = x.shape[0], x2.shape[0]
    bm = min(bm, m)
    assert n % bn == 0 and col0 % bn == 0 and scaled_cols % bn == 0
    assert (residual is None) == (residual2 is None)
    j0 = col0 // bn
    ep = _StoreEpilogue(n // bn, scaled_cols // bn, scale, residual is not None)
    tile_spec = pl.BlockSpec((bm, bn), lambda j, i: (i, j))
    rows_spec = pl.BlockSpec((m2, bn), lambda j, i: (0, j))
    has_res = residual is not None
    (out,), (out2,) = _matmul(
        x, x2, w, ep, w_col_blocks=[lambda j: j + j0], bn=bn, bm=bm,
        extra=(residual,) if has_res else (), extra_specs=(tile_spec,) if has_res else (),
        extra2=(residual2,) if has_res else (), extra2_specs=(rows_spec,) if has_res else (),
        out_shapes=[jax.ShapeDtypeStruct((m, n), out_dtype)], out_specs=[tile_spec],
        out2_shapes=[jax.ShapeDtypeStruct((m2, n), out_dtype)], out2_specs=[rows_spec],
        name=name, w_buffers=w_buffers)
    return out, out2


class _RopeEpilogue(_Epilogue):
    def __init__(self, n_col_steps, rope_col_steps, n_out):
        self.n_col_steps = n_col_steps
        self.rope_col_steps = rope_col_steps
        self.n_out = n_out

    def _store(self, acc, j, tables, outs):
        cos_ref, sin_ref = tables

        def rotated():
            cos, sin = cos_ref[...], sin_ref[...]
            heads = []
            for h in range(acc.shape[1] // A_HEAD_DIM):
                xh = acc[:, h * A_HEAD_DIM:(h + 1) * A_HEAD_DIM]
                heads.append(xh * cos + pltpu.roll(xh, A_HEAD_DIM // 2, axis=1) * sin)
            return jnp.concatenate(heads, axis=1)

        if self.rope_col_steps >= self.n_col_steps:
            val = rotated()
        else:
            val = jnp.where(j < self.rope_col_steps, rotated(), acc)
        for o_ref in outs:
            o_ref[...] = val.astype(o_ref.dtype)

    def tile(self, acc, i, j, extra, outs, scratch):
        self._store(acc, j, extra, outs)

    def rows(self, acc, j, extra, extra2, outs2):
        self._store(acc, j, extra2, outs2)


def _rope_linear(x, x2, w, tables, tables2, *, n, rope_cols, out_dtypes, rows_per_seq, name,
                 bn=MM_BLOCK_N):
    m, m2 = x.shape[0], x2.shape[0]
    bm = min(MM_BLOCK_M, m, rows_per_seq)
    assert rows_per_seq % bm == 0 and n % bn == 0 and rope_cols % bn == 0
    tiles_per_seq = rows_per_seq // bm
    ep = _RopeEpilogue(n // bn, rope_cols // bn, len(out_dtypes))
    tab_spec = pl.BlockSpec((bm, A_HEAD_DIM), lambda j, i: (i % tiles_per_seq, 0))
    tab2_spec = pl.BlockSpec((m2, A_HEAD_DIM), lambda j, i: (0, 0))
    return _matmul(
        x, x2, w, ep, w_col_blocks=[lambda j: j], bn=bn, bm=bm,
        extra=tables, extra_specs=(tab_spec, tab_spec),
        extra2=tables2, extra2_specs=(tab2_spec, tab2_spec),
        out_shapes=[jax.ShapeDtypeStruct((m, n), dt) for dt in out_dtypes],
        out_specs=[pl.BlockSpec((bm, bn), lambda j, i: (i, j)) for _ in out_dtypes],
        out2_shapes=[jax.ShapeDtypeStruct((m2, n), dt) for dt in out_dtypes],
        out2_specs=[pl.BlockSpec((m2, bn), lambda j, i: (0, j)) for _ in out_dtypes],
        name=name)


def _gates_kernel(x_ref, wt_ref, b_ref, o_ref):
    g = lax.dot_general(x_ref[...], wt_ref[...].astype(BF16), (((1,), (1,)), ((), ())),
                        preferred_element_type=F32) + b_ref[...]
    ig = GATE_SOFT_CAP * jnp.tanh(g / GATE_SOFT_CAP)
    lf = jnp.minimum(g, 0.0) - jnp.log1p(jnp.exp(-jnp.abs(g)))
    lane = lax.broadcasted_iota(jnp.int32, g.shape, 1)
    o_ref[...] = jnp.where(lane < M_HEADS, ig, lf)


def _mlstm_gates(xn, wt_gates, b_gates):
    m, d = xn.shape
    bm = min(MM_BLOCK_M, m)
    ng = 2 * M_HEADS
    return pl.pallas_call(
        _gates_kernel,
        grid=(m // bm,),
        in_specs=[pl.BlockSpec((bm, d), lambda i: (i, 0)),
                  pl.BlockSpec((ng, d), lambda i: (0, 0)),
                  pl.BlockSpec((1, ng), lambda i: (0, 0))],
        out_specs=pl.BlockSpec((bm, ng), lambda i: (i, 0)),
        out_shape=jax.ShapeDtypeStruct((m, ng), F32),
        compiler_params=_params(("arbitrary",)),
        name="mlstm_gates",
    )(xn, wt_gates, b_gates.reshape(1, ng).astype(F32))


def _mlstm_chunk_kernel(q_ref, k_ref, v_ref, o_ref, gcol_ref, grow_ref, c0_ref, n0_ref, m0_ref,
                        h_ref, c_ref, n_ref, m_ref):
    L = q_ref.shape[0]

    @pl.when(pl.program_id(1) == 0)
    def _():
        c_ref[...] = c0_ref[...]
        n_ref[...] = n0_ref[...]
        m_ref[...] = m0_ref[...]

    t_idx = lax.broadcasted_iota(jnp.int32, (L, L), 0)
    s_idx = lax.broadcasted_iota(jnp.int32, (L, L), 1)
    causal = s_idx <= t_idx
    heads = range(M_HEADS)
    qs = [q_ref[:, h * M_DK:(h + 1) * M_DK] for h in heads]
    ks = [k_ref[:, h * M_DK:(h + 1) * M_DK] for h in heads]
    vs = [v_ref[:, h * M_DV:(h + 1) * M_DV] for h in heads]

    gw = []
    for h in heads:
        ig_col = gcol_ref[:, h:h + 1]
        lf_col = gcol_ref[:, M_HEADS + h:M_HEADS + h + 1]
        ig_row = grow_ref[h:h + 1, :]
        lf_row = grow_ref[M_HEADS + h:M_HEADS + h + 1, :]
        m_prev = m_ref[0, h:h + 1, 0:1]
        b_col = jnp.sum(jnp.where(causal, lf_row, 0.0), axis=1, keepdims=True)
        b_row = jnp.sum(jnp.where(t_idx <= s_idx, lf_col, 0.0), axis=0, keepdims=True)
        dlog = jnp.where(causal, b_col - b_row + ig_row, -jnp.inf)
        g = b_col + m_prev
        m_t = jnp.maximum(g, jnp.max(dlog, axis=1, keepdims=True))
        w_intra = jnp.exp(dlog - m_t)
        w_inter = jnp.exp(g - m_t)
        b_last = b_col[L - 1:L, :]
        m_last = m_t[L - 1:L, :]
        w_end = jnp.exp(b_last - b_col + ig_col - m_last)
        gw.append((w_intra, w_inter, m_t, w_end))

    for h in heads:
        w_intra, w_inter, m_t, _ = gw[h]
        q, k, v = qs[h], ks[h], vs[h]
        c_prev = c_ref[0, h]
        n_prev = n_ref[0, h:h + 1, :]
        qk = lax.dot_general(q, k, (((1,), (1,)), ((), ())), preferred_element_type=F32)
        s = qk * w_intra
        num = (jnp.dot(s.astype(BF16), v, preferred_element_type=F32)
               + w_inter * jnp.dot(q, c_prev.astype(BF16), preferred_element_type=F32))
        qn = jnp.sum(q.astype(F32) * n_prev, axis=1, keepdims=True)
        den = jnp.sum(s, axis=1, keepdims=True) + w_inter * qn
        inv = 1.0 / jnp.maximum(jnp.abs(den), jnp.exp(-m_t))
        gate = jax.nn.sigmoid(o_ref[:, h * M_DV:(h + 1) * M_DV])
        h_ref[:, h * M_DV:(h + 1) * M_DV] = (gate * (num * inv)).astype(h_ref.dtype)

    for h in heads:
        _, w_inter, m_t, w_end = gw[h]
        decay = w_inter[L - 1:L, :]
        kw = ks[h].astype(F32) * w_end
        c_ref[0, h] = decay * c_ref[0, h] + lax.dot_general(
            kw.astype(BF16), vs[h], (((0,), (0,)), ((), ())), preferred_element_type=F32)
        n_ref[0, h:h + 1, :] = decay * n_ref[0, h:h + 1, :] + jnp.sum(kw, axis=0, keepdims=True)
        m_ref[0, h:h + 1, :] = jnp.broadcast_to(m_t[L - 1:L, :], (1, LANES))


def _mlstm_chunks(qkv, o_pre, gates, c0, n0, m0, batch, seq):
    L = M_CHUNK
    assert seq % L == 0
    nc = seq // L
    qk_w = M_HEADS * M_DK
    v_w = M_HEADS * M_DV
    m_rows = batch * seq
    grow = gates.T
    m0b = jnp.broadcast_to(m0[:, :, None], (batch, M_HEADS, LANES))
    row = lambda b, c: b * nc + c
    hg, c_new, n_new, m_new = pl.pallas_call(
        _mlstm_chunk_kernel,
        grid=(batch, nc),
        in_specs=[
            pl.BlockSpec((L, qk_w), lambda b, c: (row(b, c), 0)),
            pl.BlockSpec((L, qk_w), lambda b, c: (row(b, c), 1)),
            pl.BlockSpec((L, v_w), lambda b, c: (row(b, c), 1)),
            pl.BlockSpec((L, v_w), lambda b, c: (row(b, c), 0)),
            pl.BlockSpec((L, 2 * M_HEADS), lambda b, c: (row(b, c), 0)),
            pl.BlockSpec((2 * M_HEADS, L), lambda b, c: (0, row(b, c))),
            pl.BlockSpec((1, M_HEADS, M_DK, M_DV), lambda b, c: (b, 0, 0, 0)),
            pl.BlockSpec((1, M_HEADS, M_DK), lambda b, c: (b, 0, 0)),
            pl.BlockSpec((1, M_HEADS, LANES), lambda b, c: (b, 0, 0)),
        ],
        out_specs=[
            pl.BlockSpec((L, v_w), lambda b, c: (row(b, c), 0)),
            pl.BlockSpec((1, M_HEADS, M_DK, M_DV), lambda b, c: (b, 0, 0, 0)),
            pl.BlockSpec((1, M_HEADS, M_DK), lambda b, c: (b, 0, 0)),
            pl.BlockSpec((1, M_HEADS, LANES), lambda b, c: (b, 0, 0)),
        ],
        out_shape=[
            jax.ShapeDtypeStruct((m_rows, v_w), BF16),
            jax.ShapeDtypeStruct((batch, M_HEADS, M_DK, M_DV), F32),
            jax.ShapeDtypeStruct((batch, M_HEADS, M_DK), F32),
            jax.ShapeDtypeStruct((batch, M_HEADS, LANES), F32),
        ],
        compiler_params=_params(("arbitrary", "arbitrary")),
        name="mlstm_chunks",
    )(qkv, qkv, qkv, o_pre, gates, grow, c0, n0, m0b)
    return hg, c_new, n_new, m_new[:, :, 0]


def _mlstm_step_kernel(q_ref, k_ref, v_ref, o_ref, g_ref, c0_ref, n0_ref, m0_ref,
                       h_ref, c_ref, n_ref, m_ref):
    dk_idx = lax.broadcasted_iota(jnp.int32, (M_DK, M_DK), 0)
    dk_lane = lax.broadcasted_iota(jnp.int32, (M_DK, M_DK), 1)
    eye = dk_idx == dk_lane
    m_rows = []
    for h in range(M_HEADS):
        q = q_ref[0, :, h * M_DK:(h + 1) * M_DK]
        k = k_ref[0, :, h * M_DK:(h + 1) * M_DK]
        v = v_ref[0, :, h * M_DV:(h + 1) * M_DV]
        ig = g_ref[0, :, h:h + 1]
        lf = g_ref[0, :, M_HEADS + h:M_HEADS + h + 1]
        c_prev = c0_ref[0, h]
        n_prev = n0_ref[0, h:h + 1, :]
        m_prev = m0_ref[0, h:h + 1, 0:1]
        qf, kf, vf = q.astype(F32), k.astype(F32), v.astype(F32)

        g = lf + m_prev
        m_t = jnp.maximum(g, ig)
        w_intra = jnp.exp(ig - m_t)
        w_inter = jnp.exp(g - m_t)
        s = jnp.sum(qf * kf, axis=1, keepdims=True) * w_intra
        q_rows = jnp.broadcast_to(q, (2 * SUBLANES, M_DK))
        qc = jnp.dot(q_rows, c_prev.astype(BF16), preferred_element_type=F32)[0:1, :]
        num = s * vf + w_inter * qc
        den = s + w_inter * jnp.sum(qf * n_prev, axis=1, keepdims=True)
        hh = num / jnp.maximum(jnp.abs(den), jnp.exp(-m_t))
        gate = jax.nn.sigmoid(o_ref[0, :, h * M_DV:(h + 1) * M_DV])
        h_ref[0, :, h * M_DV:(h + 1) * M_DV] = (gate * hh).astype(h_ref.dtype)

        kw = kf * w_intra
        kw_col = jnp.sum(jnp.where(eye, kw, 0.0), axis=1, keepdims=True)
        c_ref[0, h] = w_inter * c_prev + kw_col * vf
        n_ref[0, h:h + 1, :] = w_inter * n_prev + kw
        m_rows.append(jnp.broadcast_to(m_t, (1, LANES)))
    m_ref[0] = jnp.concatenate(m_rows, axis=0)


def _mlstm_step(qkv, o_pre, gates, c0, n0, m0):
    batch = qkv.shape[0]
    qk_w = M_HEADS * M_DK
    v_w = M_HEADS * M_DV
    qkv3 = qkv.reshape(batch, 1, 2 * qk_w + v_w)
    m0b = jnp.broadcast_to(m0[:, :, None], (batch, M_HEADS, LANES))
    hg, c_new, n_new, m_new = pl.pallas_call(
        _mlstm_step_kernel,
        grid=(batch,),
        in_specs=[
            pl.BlockSpec((1, 1, qk_w), lambda b: (b, 0, 0)),
            pl.BlockSpec((1, 1, qk_w), lambda b: (b, 0, 1)),
            pl.BlockSpec((1, 1, v_w), lambda b: (b, 0, 1)),
            pl.BlockSpec((1, 1, v_w), lambda b: (b, 0, 0)),
            pl.BlockSpec((1, 1, 2 * M_HEADS), lambda b: (b, 0, 0)),
            pl.BlockSpec((1, M_HEADS, M_DK, M_DV), lambda b: (b, 0, 0, 0)),
            pl.BlockSpec((1, M_HEADS, M_DK), lambda b: (b, 0, 0)),
            pl.BlockSpec((1, M_HEADS, LANES), lambda b: (b, 0, 0)),
        ],
        out_specs=[
            pl.BlockSpec((1, 1, v_w), lambda b: (b, 0, 0)),
            pl.BlockSpec((1, M_HEADS, M_DK, M_DV), lambda b: (b, 0, 0, 0)),
            pl.BlockSpec((1, M_HEADS, M_DK), lambda b: (b, 0, 0)),
            pl.BlockSpec((1, M_HEADS, LANES), lambda b: (b, 0, 0)),
        ],
        out_shape=[
            jax.ShapeDtypeStruct((batch, 1, v_w), BF16),
            jax.ShapeDtypeStruct((batch, M_HEADS, M_DK, M_DV), F32),
            jax.ShapeDtypeStruct((batch, M_HEADS, M_DK), F32),
            jax.ShapeDtypeStruct((batch, M_HEADS, LANES), F32),
        ],
        compiler_params=_params(("arbitrary",)),
        name="mlstm_step",
    )(qkv3, qkv3, qkv3, o_pre.reshape(batch, 1, v_w), gates.reshape(batch, 1, 2 * M_HEADS),
      c0, n0, m0b)
    return hg.reshape(batch, v_w), c_new, n_new, m_new[:, :, 0]


def _softmax_with_sink(scores, sink):
    mx = jnp.maximum(jnp.max(scores, axis=-1, keepdims=True), sink)
    e = jnp.exp(scores - mx)
    return e / (jnp.sum(e, axis=-1, keepdims=True) + jnp.exp(sink - mx))


def _swa_kernel(sink_ref, q_ref, kp_ref, ko_ref, vp_ref, vo_ref, o_ref):
    W = WINDOW
    D = A_HEAD_DIM
    blk = pl.program_id(1)
    i_idx = lax.broadcasted_iota(jnp.int32, (W, 2 * W), 0)
    j_idx = lax.broadcasted_iota(jnp.int32, (W, 2 * W), 1)
    first_key = jnp.where(blk > 0, 0, W)
    mask = (j_idx >= jnp.maximum(i_idx, first_key)) & (j_idx <= i_idx + W)
    for kv in range(A_KV_HEADS):
        cols = slice(kv * D, (kv + 1) * D)
        k_cat = jnp.concatenate([kp_ref[:, cols], ko_ref[:, cols]], axis=0)
        v_cat = jnp.concatenate([vp_ref[:, cols], vo_ref[:, cols]], axis=0)
        q = jnp.concatenate([q_ref[:, (kv * A_GROUP + g) * D:(kv * A_GROUP + g + 1) * D]
                             for g in range(A_GROUP)], axis=0)
        scores = lax.dot_general(q, k_cat, (((1,), (1,)), ((), ())), preferred_element_type=F32)
        scores = scores * (A_HEAD_DIM ** -0.5)
        probs = []
        for g in range(A_GROUP):
            sc = jnp.where(mask, scores[g * W:(g + 1) * W, :], NEG_INF)
            probs.append(_softmax_with_sink(sc, sink_ref[kv * A_GROUP + g]).astype(BF16))
        p = jnp.concatenate(probs, axis=0)
        o = jnp.dot(p, v_cat, preferred_element_type=F32)
        for g in range(A_GROUP):
            hq = kv * A_GROUP + g
            o_ref[:, hq * D:(hq + 1) * D] = o[g * W:(g + 1) * W, :].astype(o_ref.dtype)


def _swa_attention(q, kv, sinks, batch, seq):
    W = WINDOW
    assert seq % W == 0
    nb = seq // W
    kvw = A_KV_HEADS * A_HEAD_DIM
    qw = A_HEADS * A_HEAD_DIM
    row = lambda b, n: b * nb + n
    prev = lambda b, n: jnp.maximum(b * nb + n - 1, 0)
    return pl.pallas_call(
        _swa_kernel,
        grid=(batch, nb),
        in_specs=[
            pl.BlockSpec(memory_space=pltpu.SMEM),
            pl.BlockSpec((W, qw), lambda b, n: (row(b, n), 0)),
            pl.BlockSpec((W, kvw), lambda b, n: (prev(b, n), 0)),
            pl.BlockSpec((W, kvw), lambda b, n: (row(b, n), 0)),
            pl.BlockSpec((W, kvw), lambda b, n: (prev(b, n), 1)),
            pl.BlockSpec((W, kvw), lambda b, n: (row(b, n), 1)),
        ],
        out_specs=pl.BlockSpec((W, qw), lambda b, n: (row(b, n), 0)),
        out_shape=jax.ShapeDtypeStruct((batch * seq, qw), BF16),
        compiler_params=_params(("arbitrary", "arbitrary")),
        name="swa_attention",
    )(sinks.astype(F32), q, kv, kv, kv, kv)


def _decode_attn_kernel(sink_ref, q_ref, kn_ref, vn_ref, ck_ref, cv_ref, o_ref, kw_ref, vw_ref):
    W, D, KV, H = WINDOW, A_HEAD_DIM, A_KV_HEADS, A_HEADS
    scale = A_HEAD_DIM ** -0.5
    head_kv = lax.broadcasted_iota(jnp.int32, (H, W * KV), 0) // A_GROUP
    own_c = (lax.broadcasted_iota(jnp.int32, (H, W * KV), 1) % KV) == head_kv
    own_n = lax.broadcasted_iota(jnp.int32, (H, KV), 1) == head_kv[:, :KV]
    sink = sink_ref[:, 0:1]
    nt = (((1,), (1,)), ((), ()))
    for b in range(q_ref.shape[0]):
        q = q_ref[b]
        kc = ck_ref[b].reshape(W * KV, D).astype(BF16)
        vc = cv_ref[b].reshape(W * KV, D).astype(BF16)
        kn = kn_ref[b].astype(BF16)
        vn = vn_ref[b].astype(BF16)
        s_c = lax.dot_general(q, kc, nt, preferred_element_type=F32) * scale
        s_n = lax.dot_general(q, kn, nt, preferred_element_type=F32) * scale
        s_c = jnp.where(own_c, s_c, NEG_INF)
        s_n = jnp.where(own_n, s_n, NEG_INF)
        mx = jnp.maximum(jnp.maximum(jnp.max(s_c, axis=1, keepdims=True),
                                     jnp.max(s_n, axis=1, keepdims=True)), sink)
        e_c = jnp.exp(s_c - mx)
        e_n = jnp.exp(s_n - mx)
        den = (jnp.sum(e_c, axis=1, keepdims=True) + jnp.sum(e_n, axis=1, keepdims=True)
               + jnp.exp(sink - mx))
        o = (jnp.dot((e_c / den).astype(BF16), vc, preferred_element_type=F32)
             + jnp.dot((e_n / den).astype(BF16), vn, preferred_element_type=F32))
        o_ref[b] = o.astype(o_ref.dtype)
        kw_ref[b, 0:W - 1] = ck_ref[b, 1:W]
        kw_ref[b, W - 1] = kn_ref[b]
        vw_ref[b, 0:W - 1] = cv_ref[b, 1:W]
        vw_ref[b, W - 1] = vn_ref[b]


def _decode_attention(q, k_new, v_new, cache_k, cache_v, sinks):
    batch = q.shape[0]
    bb = math.gcd(batch, DECODE_ATTN_BLOCK_B)
    H, KV, D, W = A_HEADS, A_KV_HEADS, A_HEAD_DIM, WINDOW
    new_spec = pl.BlockSpec((bb, KV, D), lambda b: (b, 0, 0))
    win_spec = pl.BlockSpec((bb, W, KV, D), lambda b: (b, 0, 0, 0))
    o, k_win, v_win = pl.pallas_call(
        _decode_attn_kernel,
        grid=(batch // bb,),
        in_specs=[pl.BlockSpec((H, LANES), lambda b: (0, 0)),
                  pl.BlockSpec((bb, H, D), lambda b: (b, 0, 0)),
                  new_spec, new_spec, win_spec, win_spec],
        out_specs=[pl.BlockSpec((bb, H, D), lambda b: (b, 0, 0)), win_spec, win_spec],
        out_shape=[jax.ShapeDtypeStruct((batch, H, D), BF16),
                   jax.ShapeDtypeStruct((batch, W, KV, D), F32),
                   jax.ShapeDtypeStruct((batch, W, KV, D), F32)],
        compiler_params=_params(("arbitrary",)),
        name="decode_attention",
    )(jnp.broadcast_to(sinks.astype(F32)[:, None], (H, LANES)), q.reshape(batch, H, D),
      k_new.reshape(batch, KV, D), v_new.reshape(batch, KV, D), cache_k, cache_v)
    return o.reshape(batch, H * D), k_win, v_win


def _conv_act(g2, g1, g0, val, cw_ref, cb_ref):
    conv = cb_ref[...] + cw_ref[0:1, :] * g2
    conv = conv + cw_ref[1:2, :] * g1
    conv = conv + cw_ref[2:3, :] * g0
    return conv * jax.nn.sigmoid(conv) * val


class _ConvFfnEpilogue(_Epilogue):
    def __init__(self, n_col_steps, tiles_per_seq):
        self.n_col_steps = n_col_steps
        self.tiles_per_seq = tiles_per_seq

    def scratch_shapes(self, bm, bn):
        return [pltpu.VMEM((SUBLANES, bn), F32)]

    def init_scratch(self, scratch):
        scratch[0][...] = jnp.zeros_like(scratch[0])

    def rows(self, acc, j, extra, extra2, outs2):
        cw_ref, cb_ref = extra[:2]
        g2_ref, g1_ref = extra2
        act_ref, gate_ref = outs2
        bn = acc.shape[1] // 2
        gate, val = acc[:, :bn], acc[:, bn:]
        act_ref[...] = _conv_act(g2_ref[...], g1_ref[...], gate, val, cw_ref, cb_ref).astype(act_ref.dtype)
        gate_ref[...] = gate

    def tile(self, acc, i, j, extra, outs, scratch):
        cw_ref, cb_ref, halo0_ref, wd_ref = extra
        act_ref, tail_ref, wd_bf_ref = outs
        wd_bf_ref[...] = wd_ref[...].astype(wd_bf_ref.dtype)
        carry_ref, = scratch
        bm = acc.shape[0]
        bn = acc.shape[1] // 2
        gate, val = acc[:, :bn], acc[:, bn:]
        seq_start = (i % self.tiles_per_seq) == 0
        halo = jnp.where(seq_start, halo0_ref[0], carry_ref[...])
        ext = jnp.concatenate([halo, gate], axis=0)
        g1 = ext[SUBLANES - 1:SUBLANES - 1 + bm, :]
        g2 = ext[SUBLANES - 2:SUBLANES - 2 + bm, :]
        act_ref[...] = _conv_act(g2, g1, gate, val, cw_ref, cb_ref).astype(act_ref.dtype)
        tail = gate[bm - SUBLANES:, :]
        carry_ref[...] = tail
        tail_ref[0] = tail


def _conv_ffn_up(xn, xn2, w_up, w_down, conv_w, conv_b, conv0, conv0_2, batch, seq):
    m, m2 = xn.shape[0], xn2.shape[0]
    bn = FFN_BLOCK_N
    assert D_FF % bn == 0
    n_j = D_FF // bn
    cw = conv_w.astype(F32)
    cb = conv_b.reshape(1, D_FF).astype(F32)
    col_spec = lambda rows: pl.BlockSpec((rows, bn), lambda j, i: (0, j))
    bm = min(MM_BLOCK_M, seq)
    assert seq % bm == 0 and bm % SUBLANES == 0
    tiles_per_seq = seq // bm
    n_i = m // bm
    halo0 = jnp.concatenate(
        [jnp.zeros((batch, SUBLANES - (CONV_W - 1), D_FF), F32), conv0.astype(F32)], axis=1)
    g2, g1 = conv0_2[:, 0, :], conv0_2[:, 1, :]
    rows_spec = pl.BlockSpec((m2, bn), lambda j, i: (0, j))
    k_down, n_down = w_down.arr.shape[1:]
    wd_rows = k_down // (n_j * n_i)
    assert wd_rows * n_j * n_i == k_down and wd_rows % (2 * SUBLANES) == 0 and not w_down.transposed
    (act, tails, wd_bf16), (act2, gate2) = _matmul(
        xn, xn2, w_up, _ConvFfnEpilogue(n_j, tiles_per_seq),
        w_col_blocks=[lambda j: j, lambda j: j + n_j], bn=bn, bm=bm,
        extra=(cw, cb, halo0, w_down.arr),
        extra_specs=(col_spec(CONV_W), col_spec(1),
                     pl.BlockSpec((1, SUBLANES, bn), lambda j, i: (i // tiles_per_seq, 0, j)),
                     pl.BlockSpec((None, wd_rows, n_down), lambda j, i: (w_down.layer, j * n_i + i, 0))),
        extra2=(g2, g1), extra2_specs=(rows_spec, rows_spec),
        out_shapes=[jax.ShapeDtypeStruct((m, D_FF), BF16),
                    jax.ShapeDtypeStruct((n_i, SUBLANES, D_FF), F32),
                    jax.ShapeDtypeStruct((k_down, n_down), BF16)],
        out_specs=[pl.BlockSpec((bm, bn), lambda j, i: (i, j)),
                   pl.BlockSpec((1, SUBLANES, bn), lambda j, i: (i, 0, j)),
                   pl.BlockSpec((wd_rows, n_down), lambda j, i: (j * n_i + i, 0))],
        out2_shapes=[jax.ShapeDtypeStruct((m2, D_FF), BF16), jax.ShapeDtypeStruct((m2, D_FF), F32)],
        out2_specs=[rows_spec, rows_spec],
        name="conv_ffn_up")
    last_tiles = tails.reshape(batch, tiles_per_seq, SUBLANES, D_FF)[:, -1]
    return ((act, last_tiles[:, SUBLANES - (CONV_W - 1):, :]),
            (act2, jnp.stack([g1, gate2], axis=1)), wd_bf16)


def _conv_ffn(h, h2, gain, w_up, conv_w, conv_b, w_down, conv0, conv0_2, batch, seq):
    gain = gain.reshape(1, D_MODEL)
    xn, = _rms_norm(h, gain, BF16)
    xn2, = _rms_norm(h2, gain, BF16)
    (act, conv_new), (act2, conv_new2), wd_bf16 = _conv_ffn_up(
        xn, xn2, w_up, w_down, conv_w, conv_b, conv0, conv0_2, batch, seq)
    out, out2 = _linear(act, act2, _Weight(wd_bf16[None], 0), col0=0, n=D_MODEL, out_dtype=F32,
                        bn=DOWN_BLOCK_N, bm=DOWN_BLOCK_M, residual=h, residual2=h2, name="ffn_down")
    return (out, conv_new), (out2, conv_new2)


def _rope_tables(pos):
    inv_freq = ROPE_THETA ** (-jnp.arange(0, A_HEAD_DIM, 2, dtype=F32) / A_HEAD_DIM)
    ang = pos.astype(F32)[:, None] * inv_freq[None, :]
    cos, sin = jnp.cos(ang), jnp.sin(ang)
    return jnp.concatenate([cos, cos], axis=1), jnp.concatenate([-sin, sin], axis=1)


def _model(x_p, x_s, state_p, state_s, k_win0, v_win0, w):
    batch, seq, _ = x_p.shape
    batch2 = x_s.shape[0]
    assert x_s.shape[1] == 1
    hp = x_p.reshape(batch * seq, D_MODEL)
    hs = x_s.reshape(batch2, D_MODEL)
    c0_p, n0_p, m0_p, conv0_p = state_p
    c0_s, n0_s, m0_s, conv0_s = state_s
    qk_w = M_HEADS * M_DK
    v_w = M_HEADS * M_DV
    kv_w = A_KV_HEADS * A_HEAD_DIM
    q_w = A_HEADS * A_HEAD_DIM

    g_mix0 = w["g_mix"][0:1]
    hn_p, = _rms_norm(hp, g_mix0, BF16)
    hn_s, = _rms_norm(hs, g_mix0, BF16)
    wt_in = _Weight(w["wt_mlstm_in"], 0, transposed=True)
    qkv_p, qkv_s = _linear(hn_p, hn_s, wt_in, col0=0, n=2 * qk_w + v_w, out_dtype=BF16,
                           scaled_cols=qk_w, scale=M_DK ** -0.5, name="mlstm_qkv")
    o_p, o_s = _linear(hn_p, hn_s, wt_in, col0=2 * qk_w + v_w, n=v_w, out_dtype=F32, name="mlstm_ogate")
    wt_gates = w["wt_mlstm_in"][0, 2 * qk_w + 2 * v_w:, :]
    gates_p = _mlstm_gates(hn_p, wt_gates, w["b_mlstm_gates"][0])
    gates_s = _mlstm_gates(hn_s, wt_gates, w["b_mlstm_gates"][0])
    hg_p, c_p, n_p, m_p = _mlstm_chunks(qkv_p, o_p, gates_p, c0_p[0], n0_p[0], m0_p[0], batch, seq)
    hg_s, c_s, n_s, m_s = _mlstm_step(qkv_s, o_s, gates_s, c0_s[0], n0_s[0], m0_s[0])
    hp, hs = _linear(hg_p, hg_s, _Weight(w["w_mlstm_out"], 0), col0=0, n=D_MODEL, out_dtype=F32,
                     residual=hp, residual2=hs, bm=WIDE_BLOCK_M, bn=WIDE_BLOCK_N, w_buffers=1,
                     name="mlstm_out")
    (hp, conv_a_p), (hs, conv_a_s) = _conv_ffn(
        hp, hs, w["g_ffn"][0], _Weight(w["w_ffn_up"], 0), w["ffn_conv_w"][0], w["ffn_conv_b"][0],
        _Weight(w["w_ffn_down"], 0), conv0_p[0], conv0_s[0], batch, seq)

    g_kv_mix = jnp.stack([w["g_kv"], w["g_mix"][1]])
    kvn_p, hn_p = _rms_norm(hp, g_kv_mix, BF16)
    kvn_s, hn_s = _rms_norm(hs, g_kv_mix, BF16)
    tab_p = _rope_tables(jnp.arange(seq))
    tab_s = tuple(jnp.broadcast_to(t, (batch2, A_HEAD_DIM)) for t in _rope_tables(PAST_LEN + jnp.arange(1)))
    (kv_bf_p, kv_f32_p), (_, kv_f32_s) = _rope_linear(
        kvn_p, kvn_s, _Weight(w["w_kv"][None], 0), tab_p, tab_s, n=2 * kv_w, rope_cols=kv_w,
        out_dtypes=(BF16, F32), rows_per_seq=seq, name="kv_proj")
    (q_p,), (q_s,) = _rope_linear(
        hn_p, hn_s, _Weight(w["w_attn_q"], 0), tab_p, tab_s, n=q_w, rope_cols=q_w,
        out_dtypes=(BF16,), rows_per_seq=seq, name="q_proj")
    sinks = w["attn_sinks"][0]
    att_p = _swa_attention(q_p, kv_bf_p, sinks, batch, seq)
    kv4 = kv_f32_p.reshape(batch, seq, 2, A_KV_HEADS, A_HEAD_DIM)
    k_win_p, v_win_p = kv4[:, -WINDOW:, 0], kv4[:, -WINDOW:, 1]
    att_s, k_win_s, v_win_s = _decode_attention(
        q_s, kv_f32_s[:, :kv_w], kv_f32_s[:, kv_w:], k_win0, v_win0, sinks)
    hp, hs = _linear(att_p, att_s, _Weight(w["w_attn_o"], 0), col0=0, n=D_MODEL, out_dtype=F32,
                     residual=hp, residual2=hs, name="attn_out")
    (hp, conv_b_p), (hs, conv_b_s) = _conv_ffn(
        hp, hs, w["g_ffn"][1], _Weight(w["w_ffn_up"], 1), w["ffn_conv_w"][1], w["ffn_conv_b"][1],
        _Weight(w["w_ffn_down"], 1), conv0_p[1], conv0_s[1], batch, seq)
    g_final = w["g_final"].reshape(1, D_MODEL)
    y_p, = _rms_norm(hp, g_final, F32)
    y_s, = _rms_norm(hs, g_final, F32)
    out_p = (y_p.reshape(batch, seq, D_MODEL), c_p[None], n_p[None], m_p[None], k_win_p, v_win_p,
             jnp.stack([conv_a_p, conv_b_p]))
    out_s = (y_s.reshape(batch2, 1, D_MODEL), c_s[None], n_s[None], m_s[None], k_win_s, v_win_s,
             jnp.stack([conv_a_s, conv_b_s]))
    return out_p, out_s


def kernel(x_prompt, x_sample, state_mlstm_C, state_mlstm_n, state_mlstm_m, cache_k_win, cache_v_win,
           state_ffn_conv, g_mix, g_ffn, w_mlstm_in, b_mlstm_gates, w_mlstm_out, g_kv, w_kv, w_attn_q,
           attn_sinks, w_attn_o, w_ffn_up, ffn_conv_w, ffn_conv_b, w_ffn_down, g_final):
    w = dict(g_mix=g_mix, g_ffn=g_ffn, wt_mlstm_in=jnp.swapaxes(w_mlstm_in, 1, 2),
             b_mlstm_gates=b_mlstm_gates,
             w_mlstm_out=w_mlstm_out, g_kv=g_kv, w_kv=w_kv, w_attn_q=w_attn_q, attn_sinks=attn_sinks,
             w_attn_o=w_attn_o, w_ffn_up=w_ffn_up, ffn_conv_w=ffn_conv_w, ffn_conv_b=ffn_conv_b,
             w_ffn_down=w_ffn_down, g_final=g_final)
    bp = x_prompt.shape[0]
    n_a = state_mlstm_C.shape[0]
    depth = state_ffn_conv.shape[0]
    state_p = (jnp.zeros((n_a, bp, M_HEADS, M_DK, M_DV), F32), jnp.zeros((n_a, bp, M_HEADS, M_DK), F32),
               jnp.zeros((n_a, bp, M_HEADS), F32), jnp.zeros((depth, bp, CONV_W - 1, D_FF), F32))
    state_s = (state_mlstm_C, state_mlstm_n, state_mlstm_m, state_ffn_conv)
    prompt, sample = _model(x_prompt, x_sample, state_p, state_s, cache_k_win, cache_v_win, w)
    return (prompt[0], sample[0]) + prompt[1:] + sample[1:]
```

```python
import functools
import math
from typing import NamedTuple

import jax
import jax.numpy as jnp
from jax import lax
from jax.experimental import pallas as pl
from jax.experimental.pallas import tpu as pltpu

F32 = jnp.float32
BF16 = jnp.bfloat16

D_MODEL = 4096
PAST_LEN = 16384
M_HEADS = 8
M_DK = D_MODEL // (2 * M_HEADS)
M_DV = D_MODEL // M_HEADS
M_CHUNK = 128
GATE_SOFT_CAP = 15.0
A_HEADS = 32
A_KV_HEADS = 8
A_GROUP = A_HEADS // A_KV_HEADS
A_HEAD_DIM = D_MODEL // A_HEADS
WINDOW = 128
ROPE_THETA = 10000.0
NEG_INF = -1e30
D_FF = 11008
CONV_W = 3
RMS_EPS = 1e-6

V7X_VMEM_BYTES = 64 * 1024 * 1024
VMEM_LIMIT_BYTES = V7X_VMEM_BYTES - 8 * 1024 * 1024
LANES = 128
SUBLANES = 8

MM_BLOCK_M = 1024
MM_BLOCK_M_F32_OUT = 512
MM_BLOCK_N = 1024
FFN_BLOCK_N = 256
DOWN_BLOCK_M = 512
DOWN_BLOCK_N = 512
RMS_BLOCK_M = 512
DECODE_ATTN_BLOCK_B = 4
CAST_ROWS = 256


def _params(sem):
    return pltpu.CompilerParams(dimension_semantics=sem, vmem_limit_bytes=VMEM_LIMIT_BYTES)


def _rms_kernel(x_ref, g_ref, *o_refs):
    xf = x_ref[...]
    r = lax.rsqrt(jnp.mean(xf * xf, axis=-1, keepdims=True) + RMS_EPS)
    xr = xf * r
    for k, o_ref in enumerate(o_refs):
        o_ref[...] = (xr * g_ref[k:k + 1, :]).astype(o_ref.dtype)


def _rms_norm(x, gains, out_dtype):
    m, d = x.shape
    n_g = gains.shape[0]
    bm = min(RMS_BLOCK_M, m)
    assert m % bm == 0
    outs = pl.pallas_call(
        _rms_kernel,
        grid=(m // bm,),
        in_specs=[pl.BlockSpec((bm, d), lambda i: (i, 0)),
                  pl.BlockSpec((n_g, d), lambda i: (0, 0))],
        out_specs=[pl.BlockSpec((bm, d), lambda i: (i, 0)) for _ in range(n_g)],
        out_shape=[jax.ShapeDtypeStruct((m, d), out_dtype) for _ in range(n_g)],
        compiler_params=_params(("arbitrary",)),
        name="rms_norm",
    )(x, gains)
    return outs


def _cast_weight(w_ref, wbf_ref, col0, ncols, transposed):
    k = wbf_ref.shape[0]
    rows_per = min(CAST_ROWS, k)
    assert k % rows_per == 0

    def body(r, carry):
        rows = pl.ds(pl.multiple_of(r * rows_per, rows_per), rows_per)
        chunk = w_ref[:, rows].T if transposed else w_ref[rows, :]
        wbf_ref[rows, col0:col0 + ncols] = chunk.astype(BF16)
        return carry

    lax.fori_loop(0, k // rows_per, body, 0)


class _Prefetch(NamedTuple):
    col_blocks: tuple
    layer: int
    bn: int
    n_j: int


def _mm_kernel(*refs, n_w, n_w_refs, counts, epilogue, cast, transposed, prefetch):
    n_extra, n_extra2, n_out, n_out2 = counts
    x_ref, x2_ref = refs[0], refs[1]
    refs = refs[2:]
    w_refs, refs = refs[:n_w_refs], refs[n_w_refs:]
    extra, refs = refs[:n_extra], refs[n_extra:]
    extra2, refs = refs[:n_extra2], refs[n_extra2:]
    outs, refs = refs[:n_out], refs[n_out:]
    outs2, scratch = refs[:n_out2], refs[n_out2:]
    j = pl.program_id(0)
    i = pl.program_id(1)
    bm = x_ref.shape[0]
    if cast:
        wbf_ref = scratch[0]
        scratch = scratch[1:]
    if prefetch is not None:
        w_hbm = w_refs[0]
        stage_ref, sem_ref = scratch[0], scratch[1]
        scratch = scratch[2:]
        w_refs = [stage_ref.at[t] for t in range(n_w)]

        def slab_copy(t, jj):
            start = pl.multiple_of(prefetch.col_blocks[t](jj) * prefetch.bn, prefetch.bn)
            cols = pl.ds(start, prefetch.bn)
            src = w_hbm.at[prefetch.layer, cols, :] if transposed else w_hbm.at[prefetch.layer, :, cols]
            return pltpu.make_async_copy(src, stage_ref.at[t], sem_ref.at[t])

    w_ref = wbf_ref if cast else w_refs[0]

    @pl.when(i == 0)
    def _():
        epilogue.init_scratch(scratch)
        if prefetch is not None:
            @pl.when(j == 0)
            def _():
                for t in range(n_w):
                    slab_copy(t, j).start()

            for t in range(n_w):
                slab_copy(t, j).wait()
        if cast:
            bn = wbf_ref.shape[1] // n_w
            for t, wt_ref in enumerate(w_refs):
                _cast_weight(wt_ref, wbf_ref, t * bn, bn, transposed)
        if prefetch is not None:
            @pl.when(j + 1 < prefetch.n_j)
            def _():
                for t in range(n_w):
                    slab_copy(t, j + 1).start()

        rows = jnp.concatenate([x_ref[...], x2_ref[...]], axis=0)
        acc = jnp.dot(rows, w_ref[...], preferred_element_type=F32)
        epilogue.tile(acc[:bm], i, j, extra, outs, scratch)
        epilogue.rows(acc[bm:], j, extra, extra2, outs2)

    @pl.when(i != 0)
    def _():
        acc = jnp.dot(x_ref[...], w_ref[...], preferred_element_type=F32)
        epilogue.tile(acc, i, j, extra, outs, scratch)


class _Epilogue:
    def scratch_shapes(self, bm, bn):
        return []

    def init_scratch(self, scratch):
        pass


class _Weight(NamedTuple):
    arr: jax.Array
    layer: int = 0
    transposed: bool = False

    @property
    def k(self):
        return self.arr.shape[2 if self.transposed else 1]


def _matmul(x, x2, w, epilogue, *, w_col_blocks, bn, bm=MM_BLOCK_M, extra=(), extra_specs=(),
            extra2=(), extra2_specs=(), out_shapes, out_specs, out2_shapes, out2_specs, name,
            prefetch=False):
    m, k = x.shape
    m2 = x2.shape[0]
    bm = min(bm, m)
    assert m % bm == 0 and w.k == k and x2.shape[1] == k
    cast = w.arr.dtype != BF16
    assert cast or not w.transposed
    n_w = len(w_col_blocks)
    n_j = epilogue.n_col_steps
    in_specs = [pl.BlockSpec((bm, k), lambda j, i: (i, 0)),
                pl.BlockSpec((m2, k), lambda j, i: (0, 0))]
    slab_shape = (bn, k) if w.transposed else (k, bn)
    if prefetch:
        assert cast
        in_specs.append(pl.BlockSpec(memory_space=pl.ANY))
        w_args = [w.arr]
    else:
        for f in w_col_blocks:
            if w.transposed:
                index_map = functools.partial(lambda j, i, f: (w.layer, f(j), 0), f=f)
            else:
                index_map = functools.partial(lambda j, i, f: (w.layer, 0, f(j)), f=f)
            in_specs.append(pl.BlockSpec((None,) + slab_shape, index_map))
        w_args = [w.arr] * n_w
    in_specs.extend(extra_specs)
    in_specs.extend(extra2_specs)
    scratch = []
    if cast:
        scratch.append(pltpu.VMEM((k, n_w * bn), BF16))
    if prefetch:
        scratch.append(pltpu.VMEM((n_w,) + slab_shape, F32))
        scratch.append(pltpu.SemaphoreType.DMA((n_w,)))
    scratch.extend(epilogue.scratch_shapes(bm, bn))
    counts = (len(extra), len(extra2), len(out_shapes), len(out2_shapes))
    kern = functools.partial(
        _mm_kernel, n_w=n_w, n_w_refs=len(w_args), counts=counts, epilogue=epilogue, cast=cast,
        transposed=w.transposed,
        prefetch=_Prefetch(tuple(w_col_blocks), w.layer, bn, n_j) if prefetch else None)
    res = pl.pallas_call(
        kern,
        grid=(n_j, m // bm),
        in_specs=in_specs,
        out_specs=list(out_specs) + list(out2_specs),
        out_shape=list(out_shapes) + list(out2_shapes),
        scratch_shapes=scratch,
        compiler_params=_params(("arbitrary", "arbitrary")),
        name=name,
    )(x, x2, *w_args, *extra, *extra2)
    return res[:len(out_shapes)], res[len(out_shapes):]


class _StoreEpilogue(_Epilogue):
    def __init__(self, n_col_steps, scaled_col_steps=0, scale=1.0, residual=False):
        self.n_col_steps = n_col_steps
        self.scaled_col_steps = scaled_col_steps
        self.scale = scale
        self.residual = residual

    def _store(self, acc, j, res_refs, out_ref):
        if self.scaled_col_steps:
            acc = acc * jnp.where(j < self.scaled_col_steps, self.scale, 1.0).astype(F32)
        if self.residual:
            acc = res_refs[0][...] + acc
        out_ref[...] = acc.astype(out_ref.dtype)

    def tile(self, acc, i, j, extra, outs, scratch):
        self._store(acc, j, extra, outs[0])

    def rows(self, acc, j, extra, extra2, outs2):
        self._store(acc, j, extra2, outs2[0])


def _linear(x, x2, w, *, col0, n, out_dtype, bn=MM_BLOCK_N, scaled_cols=0, scale=1.0,
            residual=None, residual2=None, bm=None, name):
    m, m2 = x.shape[0], x2.shape[0]
    if bm is None:
        bm = MM_BLOCK_M if out_dtype == BF16 else MM_BLOCK_M_F32_OUT
    bm = min(bm, m)
    assert n % bn == 0 and col0 % bn == 0 and scaled_cols % bn == 0
    assert (residual is None) == (residual2 is None)
    j0 = col0 // bn
    ep = _StoreEpilogue(n // bn, scaled_cols // bn, scale, residual is not None)
    tile_spec = pl.BlockSpec((bm, bn), lambda j, i: (i, j))
    rows_spec = pl.BlockSpec((m2, bn), lambda j, i: (0, j))
    has_res = residual is not None
    (out,), (out2,) = _matmul(
        x, x2, w, ep, w_col_blocks=[lambda j: j + j0], bn=bn, bm=bm,
        extra=(residual,) if has_res else (), extra_specs=(tile_spec,) if has_res else (),
        extra2=(residual2,) if has_res else (), extra2_specs=(rows_spec,) if has_res else (),
        out_shapes=[jax.ShapeDtypeStruct((m, n), out_dtype)], out_specs=[tile_spec],
        out2_shapes=[jax.ShapeDtypeStruct((m2, n), out_dtype)], out2_specs=[rows_spec],
        name=name, prefetch=w.arr.dtype != BF16)
    return out, out2


class _RopeEpilogue(_Epilogue):
    def __init__(self, n_col_steps, rope_col_steps, n_out):
        self.n_col_steps = n_col_steps
        self.rope_col_steps = rope_col_steps
        self.n_out = n_out

    def _store(self, acc, j, tables, outs):
        cos_ref, sin_ref = tables

        def rotated():
            cos, sin = cos_ref[...], sin_ref[...]
            heads = []
            for h in range(acc.shape[1] // A_HEAD_DIM):
                xh = acc[:, h * A_HEAD_DIM:(h + 1) * A_HEAD_DIM]
                heads.append(xh * cos + pltpu.roll(xh, A_HEAD_DIM // 2, axis=1) * sin)
            return jnp.concatenate(heads, axis=1)

        if self.rope_col_steps >= self.n_col_steps:
            val = rotated()
        else:
            val = jnp.where(j < self.rope_col_steps, rotated(), acc)
        for o_ref in outs:
            o_ref[...] = val.astype(o_ref.dtype)

    def tile(self, acc, i, j, extra, outs, scratch):
        self._store(acc, j, extra, outs)

    def rows(self, acc, j, extra, extra2, outs2):
        self._store(acc, j, extra2, outs2)


def _rope_linear(x, x2, w, tables, tables2, *, n, rope_cols, out_dtypes, rows_per_seq, name,
                 bn=MM_BLOCK_N):
    m, m2 = x.shape[0], x2.shape[0]
    bm = min(MM_BLOCK_M_F32_OUT, m, rows_per_seq)
    assert rows_per_seq % bm == 0 and n % bn == 0 and rope_cols % bn == 0
    tiles_per_seq = rows_per_seq // bm
    ep = _RopeEpilogue(n // bn, rope_cols // bn, len(out_dtypes))
    tab_spec = pl.BlockSpec((bm, A_HEAD_DIM), lambda j, i: (i % tiles_per_seq, 0))
    tab2_spec = pl.BlockSpec((m2, A_HEAD_DIM), lambda j, i: (0, 0))
    return _matmul(
        x, x2, w, ep, w_col_blocks=[lambda j: j], bn=bn, bm=bm,
        extra=tables, extra_specs=(tab_spec, tab_spec),
        extra2=tables2, extra2_specs=(tab2_spec, tab2_spec),
        out_shapes=[jax.ShapeDtypeStruct((m, n), dt) for dt in out_dtypes],
        out_specs=[pl.BlockSpec((bm, bn), lambda j, i: (i, j)) for _ in out_dtypes],
        out2_shapes=[jax.ShapeDtypeStruct((m2, n), dt) for dt in out_dtypes],
        out2_specs=[pl.BlockSpec((m2, bn), lambda j, i: (0, j)) for _ in out_dtypes],
        name=name, prefetch=True)


def _gates_kernel(x_ref, wt_ref, b_ref, o_ref):
    g = lax.dot_general(x_ref[...], wt_ref[...].astype(BF16), (((1,), (1,)), ((), ())),
                        preferred_element_type=F32) + b_ref[...]
    ig = GATE_SOFT_CAP * jnp.tanh(g / GATE_SOFT_CAP)
    lf = jnp.minimum(g, 0.0) - jnp.log1p(jnp.exp(-jnp.abs(g)))
    lane = lax.broadcasted_iota(jnp.int32, g.shape, 1)
    o_ref[...] = jnp.where(lane < M_HEADS, ig, lf)


def _mlstm_gates(xn, wt_gates, b_gates):
    m, d = xn.shape
    bm = min(MM_BLOCK_M, m)
    ng = 2 * M_HEADS
    return pl.pallas_call(
        _gates_kernel,
        grid=(m // bm,),
        in_specs=[pl.BlockSpec((bm, d), lambda i: (i, 0)),
                  pl.BlockSpec((ng, d), lambda i: (0, 0)),
                  pl.BlockSpec((1, ng), lambda i: (0, 0))],
        out_specs=pl.BlockSpec((bm, ng), lambda i: (i, 0)),
        out_shape=jax.ShapeDtypeStruct((m, ng), F32),
        compiler_params=_params(("arbitrary",)),
        name="mlstm_gates",
    )(xn, wt_gates, b_gates.reshape(1, ng).astype(F32))


def _mlstm_chunk_kernel(q_ref, k_ref, v_ref, o_ref, gcol_ref, grow_ref, c0_ref, n0_ref, m0_ref,
                        h_ref, c_ref, n_ref, m_ref):
    L = q_ref.shape[0]

    @pl.when(pl.program_id(1) == 0)
    def _():
        c_ref[...] = c0_ref[...]
        n_ref[...] = n0_ref[...]
        m_ref[...] = m0_ref[...]

    t_idx = lax.broadcasted_iota(jnp.int32, (L, L), 0)
    s_idx = lax.broadcasted_iota(jnp.int32, (L, L), 1)
    causal = s_idx <= t_idx
    heads = range(M_HEADS)
    qs = [q_ref[:, h * M_DK:(h + 1) * M_DK] for h in heads]
    ks = [k_ref[:, h * M_DK:(h + 1) * M_DK] for h in heads]
    vs = [v_ref[:, h * M_DV:(h + 1) * M_DV] for h in heads]

    gw = []
    for h in heads:
        ig_col = gcol_ref[:, h:h + 1]
        lf_col = gcol_ref[:, M_HEADS + h:M_HEADS + h + 1]
        ig_row = grow_ref[h:h + 1, :]
        lf_row = grow_ref[M_HEADS + h:M_HEADS + h + 1, :]
        m_prev = m_ref[0, h:h + 1, 0:1]
        b_col = jnp.sum(jnp.where(causal, lf_row, 0.0), axis=1, keepdims=True)
        b_row = jnp.sum(jnp.where(t_idx <= s_idx, lf_col, 0.0), axis=0, keepdims=True)
        dlog = jnp.where(causal, b_col - b_row + ig_row, -jnp.inf)
        g = b_col + m_prev
        m_t = jnp.maximum(g, jnp.max(dlog, axis=1, keepdims=True))
        w_intra = jnp.exp(dlog - m_t)
        w_inter = jnp.exp(g - m_t)
        b_last = b_col[L - 1:L, :]
        m_last = m_t[L - 1:L, :]
        w_end = jnp.exp(b_last - b_col + ig_col - m_last)
        gw.append((w_intra, w_inter, m_t, w_end))

    for h in heads:
        w_intra, w_inter, m_t, _ = gw[h]
        q, k, v = qs[h], ks[h], vs[h]
        c_prev = c_ref[0, h]
        n_prev = n_ref[0, h:h + 1, :]
        qk = lax.dot_general(q, k, (((1,), (1,)), ((), ())), preferred_element_type=F32)
        s = qk * w_intra
        num = (jnp.dot(s.astype(BF16), v, preferred_element_type=F32)
               + w_inter * jnp.dot(q, c_prev.astype(BF16), preferred_element_type=F32))
        qn = jnp.sum(q.astype(F32) * n_prev, axis=1, keepdims=True)
        den = jnp.sum(s, axis=1, keepdims=True) + w_inter * qn
        inv = 1.0 / jnp.maximum(jnp.abs(den), jnp.exp(-m_t))
        gate = jax.nn.sigmoid(o_ref[:, h * M_DV:(h + 1) * M_DV])
        h_ref[:, h * M_DV:(h + 1) * M_DV] = (gate * (num * inv)).astype(h_ref.dtype)

    for h in heads:
        _, w_inter, m_t, w_end = gw[h]
        decay = w_inter[L - 1:L, :]
        kw = ks[h].astype(F32) * w_end
        c_ref[0, h] = decay * c_ref[0, h] + lax.dot_general(
            kw.astype(BF16), vs[h], (((0,), (0,)), ((), ())), preferred_element_type=F32)
        n_ref[0, h:h + 1, :] = decay * n_ref[0, h:h + 1, :] + jnp.sum(kw, axis=0, keepdims=True)
        m_ref[0, h:h + 1, :] = jnp.broadcast_to(m_t[L - 1:L, :], (1, LANES))


def _mlstm_chunks(qkv, o_pre, gates, c0, n0, m0, batch, seq):
    L = M_CHUNK
    assert seq % L == 0
    nc = seq // L
    qk_w = M_HEADS * M_DK
    v_w = M_HEADS * M_DV
    m_rows = batch * seq
    grow = gates.T
    m0b = jnp.broadcast_to(m0[:, :, None], (batch, M_HEADS, LANES))
    row = lambda b, c: b * nc + c
    hg, c_new, n_new, m_new = pl.pallas_call(
        _mlstm_chunk_kernel,
        grid=(batch, nc),
        in_specs=[
            pl.BlockSpec((L, qk_w), lambda b, c: (row(b, c), 0)),
            pl.BlockSpec((L, qk_w), lambda b, c: (row(b, c), 1)),
            pl.BlockSpec((L, v_w), lambda b, c: (row(b, c), 1)),
            pl.BlockSpec((L, v_w), lambda b, c: (row(b, c), 0)),
            pl.BlockSpec((L, 2 * M_HEADS), lambda b, c: (row(b, c), 0)),
            pl.BlockSpec((2 * M_HEADS, L), lambda b, c: (0, row(b, c))),
            pl.BlockSpec((1, M_HEADS, M_DK, M_DV), lambda b, c: (b, 0, 0, 0)),
            pl.BlockSpec((1, M_HEADS, M_DK), lambda b, c: (b, 0, 0)),
            pl.BlockSpec((1, M_HEADS, LANES), lambda b, c: (b, 0, 0)),
        ],
        out_specs=[
            pl.BlockSpec((L, v_w), lambda b, c: (row(b, c), 0)),
            pl.BlockSpec((1, M_HEADS, M_DK, M_DV), lambda b, c: (b, 0, 0, 0)),
            pl.BlockSpec((1, M_HEADS, M_DK), lambda b, c: (b, 0, 0)),
            pl.BlockSpec((1, M_HEADS, LANES), lambda b, c: (b, 0, 0)),
        ],
        out_shape=[
            jax.ShapeDtypeStruct((m_rows, v_w), BF16),
            jax.ShapeDtypeStruct((batch, M_HEADS, M_DK, M_DV), F32),
            jax.ShapeDtypeStruct((batch, M_HEADS, M_DK), F32),
            jax.ShapeDtypeStruct((batch, M_HEADS, LANES), F32),
        ],
        compiler_params=_params(("arbitrary", "arbitrary")),
        name="mlstm_chunks",
    )(qkv, qkv, qkv, o_pre, gates, grow, c0, n0, m0b)
    return hg, c_new, n_new, m_new[:, :, 0]


def _mlstm_step_kernel(q_ref, k_ref, v_ref, o_ref, g_ref, c0_ref, n0_ref, m0_ref,
                       h_ref, c_ref, n_ref, m_ref):
    dk_idx = lax.broadcasted_iota(jnp.int32, (M_DK, M_DK), 0)
    dk_lane = lax.broadcasted_iota(jnp.int32, (M_DK, M_DK), 1)
    eye = dk_idx == dk_lane
    m_rows = []
    for h in range(M_HEADS):
        q = q_ref[0, :, h * M_DK:(h + 1) * M_DK]
        k = k_ref[0, :, h * M_DK:(h + 1) * M_DK]
        v = v_ref[0, :, h * M_DV:(h + 1) * M_DV]
        ig = g_ref[0, :, h:h + 1]
        lf = g_ref[0, :, M_HEADS + h:M_HEADS + h + 1]
        c_prev = c0_ref[0, h]
        n_prev = n0_ref[0, h:h + 1, :]
        m_prev = m0_ref[0, h:h + 1, 0:1]
        qf, kf, vf = q.astype(F32), k.astype(F32), v.astype(F32)

        g = lf + m_prev
        m_t = jnp.maximum(g, ig)
        w_intra = jnp.exp(ig - m_t)
        w_inter = jnp.exp(g - m_t)
        s = jnp.sum(qf * kf, axis=1, keepdims=True) * w_intra
        q_rows = jnp.broadcast_to(q, (2 * SUBLANES, M_DK))
        qc = jnp.dot(q_rows, c_prev.astype(BF16), preferred_element_type=F32)[0:1, :]
        num = s * vf + w_inter * qc
        den = s + w_inter * jnp.sum(qf * n_prev, axis=1, keepdims=True)
        hh = num / jnp.maximum(jnp.abs(den), jnp.exp(-m_t))
        gate = jax.nn.sigmoid(o_ref[0, :, h * M_DV:(h + 1) * M_DV])
        h_ref[0, :, h * M_DV:(h + 1) * M_DV] = (gate * hh).astype(h_ref.dtype)

        kw = kf * w_intra
        kw_col = jnp.sum(jnp.where(eye, kw, 0.0), axis=1, keepdims=True)
        c_ref[0, h] = w_inter * c_prev + kw_col * vf
        n_ref[0, h:h + 1, :] = w_inter * n_prev + kw
        m_rows.append(jnp.broadcast_to(m_t, (1, LANES)))
    m_ref[0] = jnp.concatenate(m_rows, axis=0)


def _mlstm_step(qkv, o_pre, gates, c0, n0, m0):
    batch = qkv.shape[0]
    qk_w = M_HEADS * M_DK
    v_w = M_HEADS * M_DV
    qkv3 = qkv.reshape(batch, 1, 2 * qk_w + v_w)
    m0b = jnp.broadcast_to(m0[:, :, None], (batch, M_HEADS, LANES))
    hg, c_new, n_new, m_new = pl.pallas_call(
        _mlstm_step_kernel,
        grid=(batch,),
        in_specs=[
            pl.BlockSpec((1, 1, qk_w), lambda b: (b, 0, 0)),
            pl.BlockSpec((1, 1, qk_w), lambda b: (b, 0, 1)),
            pl.BlockSpec((1, 1, v_w), lambda b: (b, 0, 1)),
            pl.BlockSpec((1, 1, v_w), lambda b: (b, 0, 0)),
            pl.BlockSpec((1, 1, 2 * M_HEADS), lambda b: (b, 0, 0)),
            pl.BlockSpec((1, M_HEADS, M_DK, M_DV), lambda b: (b, 0, 0, 0)),
            pl.BlockSpec((1, M_HEADS, M_DK), lambda b: (b, 0, 0)),
            pl.BlockSpec((1, M_HEADS, LANES), lambda b: (b, 0, 0)),
        ],
        out_specs=[
            pl.BlockSpec((1, 1, v_w), lambda b: (b, 0, 0)),
            pl.BlockSpec((1, M_HEADS, M_DK, M_DV), lambda b: (b, 0, 0, 0)),
            pl.BlockSpec((1, M_HEADS, M_DK), lambda b: (b, 0, 0)),
            pl.BlockSpec((1, M_HEADS, LANES), lambda b: (b, 0, 0)),
        ],
        out_shape=[
            jax.ShapeDtypeStruct((batch, 1, v_w), BF16),
            jax.ShapeDtypeStruct((batch, M_HEADS, M_DK, M_DV), F32),
            jax.ShapeDtypeStruct((batch, M_HEADS, M_DK), F32),
            jax.ShapeDtypeStruct((batch, M_HEADS, LANES), F32),
        ],
        compiler_params=_params(("arbitrary",)),
        name="mlstm_step",
    )(qkv3, qkv3, qkv3, o_pre.reshape(batch, 1, v_w), gates.reshape(batch, 1, 2 * M_HEADS),
      c0, n0, m0b)
    return hg.reshape(batch, v_w), c_new, n_new, m_new[:, :, 0]


def _softmax_with_sink(scores, sink):
    mx = jnp.maximum(jnp.max(scores, axis=-1, keepdims=True), sink)
    e = jnp.exp(scores - mx)
    return e / (jnp.sum(e, axis=-1, keepdims=True) + jnp.exp(sink - mx))


def _swa_kernel(sink_ref, q_ref, kp_ref, ko_ref, vp_ref, vo_ref, o_ref):
    W = WINDOW
    D = A_HEAD_DIM
    blk = pl.program_id(1)
    i_idx = lax.broadcasted_iota(jnp.int32, (W, 2 * W), 0)
    j_idx = lax.broadcasted_iota(jnp.int32, (W, 2 * W), 1)
    first_key = jnp.where(blk > 0, 0, W)
    mask = (j_idx >= jnp.maximum(i_idx, first_key)) & (j_idx <= i_idx + W)
    for kv in range(A_KV_HEADS):
        cols = slice(kv * D, (kv + 1) * D)
        k_cat = jnp.concatenate([kp_ref[:, cols], ko_ref[:, cols]], axis=0)
        v_cat = jnp.concatenate([vp_ref[:, cols], vo_ref[:, cols]], axis=0)
        q = jnp.concatenate([q_ref[:, (kv * A_GROUP + g) * D:(kv * A_GROUP + g + 1) * D]
                             for g in range(A_GROUP)], axis=0)
        scores = lax.dot_general(q, k_cat, (((1,), (1,)), ((), ())), preferred_element_type=F32)
        scores = scores * (A_HEAD_DIM ** -0.5)
        probs = []
        for g in range(A_GROUP):
            sc = jnp.where(mask, scores[g * W:(g + 1) * W, :], NEG_INF)
            probs.append(_softmax_with_sink(sc, sink_ref[kv * A_GROUP + g]).astype(BF16))
        p = jnp.concatenate(probs, axis=0)
        o = jnp.dot(p, v_cat, preferred_element_type=F32)
        for g in range(A_GROUP):
            hq = kv * A_GROUP + g
            o_ref[:, hq * D:(hq + 1) * D] = o[g * W:(g + 1) * W, :].astype(o_ref.dtype)


def _swa_attention(q, kv, sinks, batch, seq):
    W = WINDOW
    assert seq % W == 0
    nb = seq // W
    kvw = A_KV_HEADS * A_HEAD_DIM
    qw = A_HEADS * A_HEAD_DIM
    row = lambda b, n: b * nb + n
    prev = lambda b, n: jnp.maximum(b * nb + n - 1, 0)
    return pl.pallas_call(
        _swa_kernel,
        grid=(batch, nb),
        in_specs=[
            pl.BlockSpec(memory_space=pltpu.SMEM),
            pl.BlockSpec((W, qw), lambda b, n: (row(b, n), 0)),
            pl.BlockSpec((W, kvw), lambda b, n: (prev(b, n), 0)),
            pl.BlockSpec((W, kvw), lambda b, n: (row(b, n), 0)),
            pl.BlockSpec((W, kvw), lambda b, n: (prev(b, n), 1)),
            pl.BlockSpec((W, kvw), lambda b, n: (row(b, n), 1)),
        ],
        out_specs=pl.BlockSpec((W, qw), lambda b, n: (row(b, n), 0)),
        out_shape=jax.ShapeDtypeStruct((batch * seq, qw), BF16),
        compiler_params=_params(("arbitrary", "arbitrary")),
        name="swa_attention",
    )(sinks.astype(F32), q, kv, kv, kv, kv)


def _decode_attn_kernel(sink_ref, q_ref, kn_ref, vn_ref, ck_ref, cv_ref, o_ref, kw_ref, vw_ref):
    W, D, KV, H = WINDOW, A_HEAD_DIM, A_KV_HEADS, A_HEADS
    scale = A_HEAD_DIM ** -0.5
    head_kv = lax.broadcasted_iota(jnp.int32, (H, W * KV), 0) // A_GROUP
    own_c = (lax.broadcasted_iota(jnp.int32, (H, W * KV), 1) % KV) == head_kv
    own_n = lax.broadcasted_iota(jnp.int32, (H, KV), 1) == head_kv[:, :KV]
    sink = sink_ref[:, 0:1]
    nt = (((1,), (1,)), ((), ()))
    for b in range(q_ref.shape[0]):
        q = q_ref[b]
        kc = ck_ref[b].reshape(W * KV, D).astype(BF16)
        vc = cv_ref[b].reshape(W * KV, D).astype(BF16)
        kn = kn_ref[b].astype(BF16)
        vn = vn_ref[b].astype(BF16)
        s_c = lax.dot_general(q, kc, nt, preferred_element_type=F32) * scale
        s_n = lax.dot_general(q, kn, nt, preferred_element_type=F32) * scale
        s_c = jnp.where(own_c, s_c, NEG_INF)
        s_n = jnp.where(own_n, s_n, NEG_INF)
        mx = jnp.maximum(jnp.maximum(jnp.max(s_c, axis=1, keepdims=True),
                                     jnp.max(s_n, axis=1, keepdims=True)), sink)
        e_c = jnp.exp(s_c - mx)
        e_n = jnp.exp(s_n - mx)
        den = (jnp.sum(e_c, axis=1, keepdims=True) + jnp.sum(e_n, axis=1, keepdims=True)
               + jnp.exp(sink - mx))
        o = (jnp.dot((e_c / den).astype(BF16), vc, preferred_element_type=F32)
             + jnp.dot((e_n / den).astype(BF16), vn, preferred_element_type=F32))
        o_ref[b] = o.astype(o_ref.dtype)
        kw_ref[b, 0:W - 1] = ck_ref[b, 1:W]
        kw_ref[b, W - 1] = kn_ref[b]
        vw_ref[b, 0:W - 1] = cv_ref[b, 1:W]
        vw_ref[b, W - 1] = vn_ref[b]


def _decode_attention(q, k_new, v_new, cache_k, cache_v, sinks):
    batch = q.shape[0]
    bb = math.gcd(batch, DECODE_ATTN_BLOCK_B)
    H, KV, D, W = A_HEADS, A_KV_HEADS, A_HEAD_DIM, WINDOW
    new_spec = pl.BlockSpec((bb, KV, D), lambda b: (b, 0, 0))
    win_spec = pl.BlockSpec((bb, W, KV, D), lambda b: (b, 0, 0, 0))
    o, k_win, v_win = pl.pallas_call(
        _decode_attn_kernel,
        grid=(batch // bb,),
        in_specs=[pl.BlockSpec((H, LANES), lambda b: (0, 0)),
                  pl.BlockSpec((bb, H, D), lambda b: (b, 0, 0)),
                  new_spec, new_spec, win_spec, win_spec],
        out_specs=[pl.BlockSpec((bb, H, D), lambda b: (b, 0, 0)), win_spec, win_spec],
        out_shape=[jax.ShapeDtypeStruct((batch, H, D), BF16),
                   jax.ShapeDtypeStruct((batch, W, KV, D), F32),
                   jax.ShapeDtypeStruct((batch, W, KV, D), F32)],
        compiler_params=_params(("arbitrary",)),
        name="decode_attention",
    )(jnp.broadcast_to(sinks.astype(F32)[:, None], (H, LANES)), q.reshape(batch, H, D),
      k_new.reshape(batch, KV, D), v_new.reshape(batch, KV, D), cache_k, cache_v)
    return o.reshape(batch, H * D), k_win, v_win


def _conv_act(g2, g1, g0, val, cw_ref, cb_ref):
    conv = cb_ref[...] + cw_ref[0:1, :] * g2
    conv = conv + cw_ref[1:2, :] * g1
    conv = conv + cw_ref[2:3, :] * g0
    return conv * jax.nn.sigmoid(conv) * val


class _ConvFfnEpilogue(_Epilogue):
    def __init__(self, n_col_steps, tiles_per_seq):
        self.n_col_steps = n_col_steps
        self.tiles_per_seq = tiles_per_seq

    def scratch_shapes(self, bm, bn):
        return [pltpu.VMEM((SUBLANES, bn), F32)]

    def init_scratch(self, scratch):
        scratch[0][...] = jnp.zeros_like(scratch[0])

    def rows(self, acc, j, extra, extra2, outs2):
        cw_ref, cb_ref = extra[:2]
        g2_ref, g1_ref = extra2
        act_ref, gate_ref = outs2
        bn = acc.shape[1] // 2
        gate, val = acc[:, :bn], acc[:, bn:]
        act_ref[...] = _conv_act(g2_ref[...], g1_ref[...], gate, val, cw_ref, cb_ref).astype(act_ref.dtype)
        gate_ref[...] = gate

    def tile(self, acc, i, j, extra, outs, scratch):
        cw_ref, cb_ref, halo0_ref, wd_ref = extra
        act_ref, tail_ref, wd_bf_ref = outs
        wd_bf_ref[...] = wd_ref[...].astype(wd_bf_ref.dtype)
        carry_ref, = scratch
        bm = acc.shape[0]
        bn = acc.shape[1] // 2
        gate, val = acc[:, :bn], acc[:, bn:]
        seq_start = (i % self.tiles_per_seq) == 0
        halo = jnp.where(seq_start, halo0_ref[0], carry_ref[...])
        ext = jnp.concatenate([halo, gate], axis=0)
        g1 = ext[SUBLANES - 1:SUBLANES - 1 + bm, :]
        g2 = ext[SUBLANES - 2:SUBLANES - 2 + bm, :]
        act_ref[...] = _conv_act(g2, g1, gate, val, cw_ref, cb_ref).astype(act_ref.dtype)
        tail = gate[bm - SUBLANES:, :]
        carry_ref[...] = tail
        tail_ref[0] = tail


def _conv_ffn_up(xn, xn2, w_up, w_down, conv_w, conv_b, conv0, conv0_2, batch, seq):
    m, m2 = xn.shape[0], xn2.shape[0]
    bn = FFN_BLOCK_N
    assert D_FF % bn == 0
    n_j = D_FF // bn
    cw = conv_w.astype(F32)
    cb = conv_b.reshape(1, D_FF).astype(F32)
    col_spec = lambda rows: pl.BlockSpec((rows, bn), lambda j, i: (0, j))
    bm = min(MM_BLOCK_M, seq)
    assert seq % bm == 0 and bm % SUBLANES == 0
    tiles_per_seq = seq // bm
    n_i = m // bm
    halo0 = jnp.concatenate(
        [jnp.zeros((batch, SUBLANES - (CONV_W - 1), D_FF), F32), conv0.astype(F32)], axis=1)
    g2, g1 = conv0_2[:, 0, :], conv0_2[:, 1, :]
    rows_spec = pl.BlockSpec((m2, bn), lambda j, i: (0, j))
    k_down, n_down = w_down.arr.shape[1:]
    wd_rows = k_down // (n_j * n_i)
    assert wd_rows * n_j * n_i == k_down and wd_rows % (2 * SUBLANES) == 0 and not w_down.transposed
    (act, tails, wd_bf16), (act2, gate2) = _matmul(
        xn, xn2, w_up, _ConvFfnEpilogue(n_j, tiles_per_seq),
        w_col_blocks=[lambda j: j, lambda j: j + n_j], bn=bn, bm=bm,
        extra=(cw, cb, halo0, w_down.arr),
        extra_specs=(col_spec(CONV_W), col_spec(1),
                     pl.BlockSpec((1, SUBLANES, bn), lambda j, i: (i // tiles_per_seq, 0, j)),
                     pl.BlockSpec((None, wd_rows, n_down), lambda j, i: (w_down.layer, j * n_i + i, 0))),
        extra2=(g2, g1), extra2_specs=(rows_spec, rows_spec),
        out_shapes=[jax.ShapeDtypeStruct((m, D_FF), BF16),
                    jax.ShapeDtypeStruct((n_i, SUBLANES, D_FF), F32),
                    jax.ShapeDtypeStruct((k_down, n_down), BF16)],
        out_specs=[pl.BlockSpec((bm, bn), lambda j, i: (i, j)),
                   pl.BlockSpec((1, SUBLANES, bn), lambda j, i: (i, 0, j)),
                   pl.BlockSpec((wd_rows, n_down), lambda j, i: (j * n_i + i, 0))],
        out2_shapes=[jax.ShapeDtypeStruct((m2, D_FF), BF16), jax.ShapeDtypeStruct((m2, D_FF), F32)],
        out2_specs=[rows_spec, rows_spec],
        name="conv_ffn_up")
    last_tiles = tails.reshape(batch, tiles_per_seq, SUBLANES, D_FF)[:, -1]
    return ((act, last_tiles[:, SUBLANES - (CONV_W - 1):, :]),
            (act2, jnp.stack([g1, gate2], axis=1)), wd_bf16)


def _conv_ffn(h, h2, gain, w_up, conv_w, conv_b, w_down, conv0, conv0_2, batch, seq):
    gain = gain.reshape(1, D_MODEL)
    xn, = _rms_norm(h, gain, BF16)
    xn2, = _rms_norm(h2, gain, BF16)
    (act, conv_new), (act2, conv_new2), wd_bf16 = _conv_ffn_up(
        xn, xn2, w_up, w_down, conv_w, conv_b, conv0, conv0_2, batch, seq)
    out, out2 = _linear(act, act2, _Weight(wd_bf16[None], 0), col0=0, n=D_MODEL, out_dtype=F32,
                        bn=DOWN_BLOCK_N, bm=DOWN_BLOCK_M, residual=h, residual2=h2, name="ffn_down")
    return (out, conv_new), (out2, conv_new2)


def _rope_tables(pos):
    inv_freq = ROPE_THETA ** (-jnp.arange(0, A_HEAD_DIM, 2, dtype=F32) / A_HEAD_DIM)
    ang = pos.astype(F32)[:, None] * inv_freq[None, :]
    cos, sin = jnp.cos(ang), jnp.sin(ang)
    return jnp.concatenate([cos, cos], axis=1), jnp.concatenate([-sin, sin], axis=1)


def _model(x_p, x_s, state_p, state_s, k_win0, v_win0, w):
    batch, seq, _ = x_p.shape
    batch2 = x_s.shape[0]
    assert x_s.shape[1] == 1
    hp = x_p.reshape(batch * seq, D_MODEL)
    hs = x_s.reshape(batch2, D_MODEL)
    c0_p, n0_p, m0_p, conv0_p = state_p
    c0_s, n0_s, m0_s, conv0_s = state_s
    qk_w = M_HEADS * M_DK
    v_w = M_HEADS * M_DV
    kv_w = A_KV_HEADS * A_HEAD_DIM
    q_w = A_HEADS * A_HEAD_DIM

    g_mix0 = w["g_mix"][0:1]
    hn_p, = _rms_norm(hp, g_mix0, BF16)
    hn_s, = _rms_norm(hs, g_mix0, BF16)
    wt_in = _Weight(w["wt_mlstm_in"], 0, transposed=True)
    qkv_p, qkv_s = _linear(hn_p, hn_s, wt_in, col0=0, n=2 * qk_w + v_w, out_dtype=BF16,
                           scaled_cols=qk_w, scale=M_DK ** -0.5, name="mlstm_qkv")
    o_p, o_s = _linear(hn_p, hn_s, wt_in, col0=2 * qk_w + v_w, n=v_w, out_dtype=F32, name="mlstm_ogate")
    wt_gates = w["wt_mlstm_in"][0, 2 * qk_w + 2 * v_w:, :]
    gates_p = _mlstm_gates(hn_p, wt_gates, w["b_mlstm_gates"][0])
    gates_s = _mlstm_gates(hn_s, wt_gates, w["b_mlstm_gates"][0])
    hg_p, c_p, n_p, m_p = _mlstm_chunks(qkv_p, o_p, gates_p, c0_p[0], n0_p[0], m0_p[0], batch, seq)
    hg_s, c_s, n_s, m_s = _mlstm_step(qkv_s, o_s, gates_s, c0_s[0], n0_s[0], m0_s[0])
    hp, hs = _linear(hg_p, hg_s, _Weight(w["w_mlstm_out"], 0), col0=0, n=D_MODEL, out_dtype=F32,
                     residual=hp, residual2=hs, name="mlstm_out")
    (hp, conv_a_p), (hs, conv_a_s) = _conv_ffn(
        hp, hs, w["g_ffn"][0], _Weight(w["w_ffn_up"], 0), w["ffn_conv_w"][0], w["ffn_conv_b"][0],
        _Weight(w["w_ffn_down"], 0), conv0_p[0], conv0_s[0], batch, seq)

    g_kv_mix = jnp.stack([w["g_kv"], w["g_mix"][1]])
    kvn_p, hn_p = _rms_norm(hp, g_kv_mix, BF16)
    kvn_s, hn_s = _rms_norm(hs, g_kv_mix, BF16)
    tab_p = _rope_tables(jnp.arange(seq))
    tab_s = tuple(jnp.broadcast_to(t, (batch2, A_HEAD_DIM)) for t in _rope_tables(PAST_LEN + jnp.arange(1)))
    (kv_bf_p, kv_f32_p), (_, kv_f32_s) = _rope_linear(
        kvn_p, kvn_s, _Weight(w["w_kv"][None], 0), tab_p, tab_s, n=2 * kv_w, rope_cols=kv_w,
        out_dtypes=(BF16, F32), rows_per_seq=seq, name="kv_proj")
    (q_p,), (q_s,) = _rope_linear(
        hn_p, hn_s, _Weight(w["w_attn_q"], 0), tab_p, tab_s, n=q_w, rope_cols=q_w,
        out_dtypes=(BF16,), rows_per_seq=seq, name="q_proj")
    sinks = w["attn_sinks"][0]
    att_p = _swa_attention(q_p, kv_bf_p, sinks, batch, seq)
    kv4 = kv_f32_p.reshape(batch, seq, 2, A_KV_HEADS, A_HEAD_DIM)
    k_win_p, v_win_p = kv4[:, -WINDOW:, 0], kv4[:, -WINDOW:, 1]
    att_s, k_win_s, v_win_s = _decode_attention(
        q_s, kv_f32_s[:, :kv_w], kv_f32_s[:, kv_w:], k_win0, v_win0, sinks)
    hp, hs = _linear(att_p, att_s, _Weight(w["w_attn_o"], 0), col0=0, n=D_MODEL, out_dtype=F32,
                     residual=hp, residual2=hs, name="attn_out")
    (hp, conv_b_p), (hs, conv_b_s) = _conv_ffn(
        hp, hs, w["g_ffn"][1], _Weight(w["w_ffn_up"], 1), w["ffn_conv_w"][1], w["ffn_conv_b"][1],
        _Weight(w["w_ffn_down"], 1), conv0_p[1], conv0_s[1], batch, seq)
    g_final = w["g_final"].reshape(1, D_MODEL)
    y_p, = _rms_norm(hp, g_final, F32)
    y_s, = _rms_norm(hs, g_final, F32)
    out_p = (y_p.reshape(batch, seq, D_MODEL), c_p[None], n_p[None], m_p[None], k_win_p, v_win_p,
             jnp.stack([conv_a_p, conv_b_p]))
    out_s = (y_s.reshape(batch2, 1, D_MODEL), c_s[None], n_s[None], m_s[None], k_win_s, v_win_s,
             jnp.stack([conv_a_s, conv_b_s]))
    return out_p, out_s


def kernel(x_prompt, x_sample, state_mlstm_C, state_mlstm_n, state_mlstm_m, cache_k_win, cache_v_win,
           state_ffn_conv, g_mix, g_ffn, w_mlstm_in, b_mlstm_gates, w_mlstm_out, g_kv, w_kv, w_attn_q,
           attn_sinks, w_attn_o, w_ffn_up, ffn_conv_w, ffn_conv_b, w_ffn_down, g_final):
    w = dict(g_mix=g_mix, g_ffn=g_ffn, wt_mlstm_in=jnp.swapaxes(w_mlstm_in, 1, 2),
             b_mlstm_gates=b_mlstm_gates,
             w_mlstm_out=w_mlstm_out, g_kv=g_kv, w_kv=w_kv, w_attn_q=w_attn_q, attn_sinks=attn_sinks,
             w_attn_o=w_attn_o, w_ffn_up=w_ffn_up, ffn_conv_w=ffn_conv_w, ffn_conv_b=ffn_conv_b,
             w_ffn_down=w_ffn_down, g_final=g_final)
    bp = x_prompt.shape[0]
    n_a = state_mlstm_C.shape[0]
    depth = state_ffn_conv.shape[0]
    state_p = (jnp.zeros((n_a, bp, M_HEADS, M_DK, M_DV), F32), jnp.zeros((n_a, bp, M_HEADS, M_DK), F32),
               jnp.zeros((n_a, bp, M_HEADS), F32), jnp.zeros((depth, bp, CONV_W - 1, D_FF), F32))
    state_s = (state_mlstm_C, state_mlstm_n, state_mlstm_m, state_ffn_conv)
    prompt, sample = _model(x_prompt, x_sample, state_p, state_s, cache_k_win, cache_v_win, w)
    return (prompt[0], sample[0]) + prompt[1:] + sample[1:]
```

```python
import functools
import math
from typing import NamedTuple

import jax
import jax.numpy as jnp
from jax import lax
from jax.experimental import pallas as pl
from jax.experimental.pallas import tpu as pltpu

F32 = jnp.float32
BF16 = jnp.bfloat16

D_MODEL = 4096
PAST_LEN = 16384
M_HEADS = 8
M_DK = D_MODEL // (2 * M_HEADS)
M_DV = D_MODEL // M_HEADS
M_CHUNK = 128
GATE_SOFT_CAP = 15.0
A_HEADS = 32
A_KV_HEADS = 8
A_GROUP = A_HEADS // A_KV_HEADS
A_HEAD_DIM = D_MODEL // A_HEADS
WINDOW = 128
ROPE_THETA = 10000.0
NEG_INF = -1e30
D_FF = 11008
CONV_W = 3
RMS_EPS = 1e-6

V7X_VMEM_BYTES = 64 * 1024 * 1024
VMEM_LIMIT_BYTES = V7X_VMEM_BYTES - 8 * 1024 * 1024
LANES = 128
SUBLANES = 8

MM_BLOCK_M = 1024
MM_BLOCK_M_F32_OUT = 512
MM_BLOCK_N = 1024
FFN_BLOCK_N = 256
DOWN_BLOCK_M = 512
DOWN_BLOCK_N = 512
RMS_BLOCK_M = 512
DECODE_ATTN_BLOCK_B = 4
MLSTM_HEAD_GROUP = 8
CAST_ROWS = 256


def _params(sem):
    return pltpu.CompilerParams(dimension_semantics=sem, vmem_limit_bytes=VMEM_LIMIT_BYTES)


def _rms_kernel(x_ref, g_ref, *o_refs):
    xf = x_ref[...]
    r = lax.rsqrt(jnp.mean(xf * xf, axis=-1, keepdims=True) + RMS_EPS)
    xr = xf * r
    for k, o_ref in enumerate(o_refs):
        o_ref[...] = (xr * g_ref[k:k + 1, :]).astype(o_ref.dtype)


def _rms_norm(x, gains, out_dtype):
    m, d = x.shape
    n_g = gains.shape[0]
    bm = min(RMS_BLOCK_M, m)
    assert m % bm == 0
    outs = pl.pallas_call(
        _rms_kernel,
        grid=(m // bm,),
        in_specs=[pl.BlockSpec((bm, d), lambda i: (i, 0)),
                  pl.BlockSpec((n_g, d), lambda i: (0, 0))],
        out_specs=[pl.BlockSpec((bm, d), lambda i: (i, 0)) for _ in range(n_g)],
        out_shape=[jax.ShapeDtypeStruct((m, d), out_dtype) for _ in range(n_g)],
        compiler_params=_params(("arbitrary",)),
        name="rms_norm",
    )(x, gains)
    return outs


def _cast_weight(w_ref, wbf_ref, col0, ncols, transposed):
    k = wbf_ref.shape[0]
    rows_per = min(CAST_ROWS, k)
    assert k % rows_per == 0

    def body(r, carry):
        rows = pl.ds(pl.multiple_of(r * rows_per, rows_per), rows_per)
        chunk = w_ref[:, rows].T if transposed else w_ref[rows, :]
        wbf_ref[rows, col0:col0 + ncols] = chunk.astype(BF16)
        return carry

    lax.fori_loop(0, k // rows_per, body, 0)


class _Prefetch(NamedTuple):
    col_blocks: tuple
    layer: int
    bn: int
    n_j: int


def _mm_kernel(*refs, n_w, n_w_refs, counts, epilogue, cast, transposed, prefetch):
    n_extra, n_extra2, n_out, n_out2 = counts
    x_ref, x2_ref = refs[0], refs[1]
    refs = refs[2:]
    w_refs, refs = refs[:n_w_refs], refs[n_w_refs:]
    extra, refs = refs[:n_extra], refs[n_extra:]
    extra2, refs = refs[:n_extra2], refs[n_extra2:]
    outs, refs = refs[:n_out], refs[n_out:]
    outs2, scratch = refs[:n_out2], refs[n_out2:]
    j = pl.program_id(0)
    i = pl.program_id(1)
    bm = x_ref.shape[0]
    if cast:
        wbf_ref = scratch[0]
        scratch = scratch[1:]
    if prefetch is not None:
        w_hbm = w_refs[0]
        stage_ref, sem_ref = scratch[0], scratch[1]
        scratch = scratch[2:]
        w_refs = [stage_ref.at[t] for t in range(n_w)]

        def slab_copy(t, jj):
            start = pl.multiple_of(prefetch.col_blocks[t](jj) * prefetch.bn, prefetch.bn)
            cols = pl.ds(start, prefetch.bn)
            src = w_hbm.at[prefetch.layer, cols, :] if transposed else w_hbm.at[prefetch.layer, :, cols]
            return pltpu.make_async_copy(src, stage_ref.at[t], sem_ref.at[t])

    w_ref = wbf_ref if cast else w_refs[0]

    @pl.when(i == 0)
    def _():
        epilogue.init_scratch(scratch)
        if prefetch is not None:
            @pl.when(j == 0)
            def _():
                for t in range(n_w):
                    slab_copy(t, j).start()

            for t in range(n_w):
                slab_copy(t, j).wait()
        if cast:
            bn = wbf_ref.shape[1] // n_w
            for t, wt_ref in enumerate(w_refs):
                _cast_weight(wt_ref, wbf_ref, t * bn, bn, transposed)
        if prefetch is not None:
            @pl.when(j + 1 < prefetch.n_j)
            def _():
                for t in range(n_w):
                    slab_copy(t, j + 1).start()

        rows = jnp.concatenate([x_ref[...], x2_ref[...]], axis=0)
        acc = jnp.dot(rows, w_ref[...], preferred_element_type=F32)
        epilogue.tile(acc[:bm], i, j, extra, outs, scratch)
        epilogue.rows(acc[bm:], j, extra, extra2, outs2)

    @pl.when(i != 0)
    def _():
        acc = jnp.dot(x_ref[...], w_ref[...], preferred_element_type=F32)
        epilogue.tile(acc, i, j, extra, outs, scratch)


class _Epilogue:
    def scratch_shapes(self, bm, bn):
        return []

    def init_scratch(self, scratch):
        pass


class _Weight(NamedTuple):
    arr: jax.Array
    layer: int = 0
    transposed: bool = False

    @property
    def k(self):
        return self.arr.shape[2 if self.transposed else 1]


def _matmul(x, x2, w, epilogue, *, w_col_blocks, bn, bm=MM_BLOCK_M, extra=(), extra_specs=(),
            extra2=(), extra2_specs=(), out_shapes, out_specs, out2_shapes, out2_specs, name,
            prefetch=False):
    m, k = x.shape
    m2 = x2.shape[0]
    bm = min(bm, m)
    assert m % bm == 0 and w.k == k and x2.shape[1] == k
    cast = w.arr.dtype != BF16
    assert cast or not w.transposed
    n_w = len(w_col_blocks)
    n_j = epilogue.n_col_steps
    in_specs = [pl.BlockSpec((bm, k), lambda j, i: (i, 0)),
                pl.BlockSpec((m2, k), lambda j, i: (0, 0))]
    slab_shape = (bn, k) if w.transposed else (k, bn)
    if prefetch:
        assert cast
        in_specs.append(pl.BlockSpec(memory_space=pl.ANY))
        w_args = [w.arr]
    else:
        for f in w_col_blocks:
            if w.transposed:
                index_map = functools.partial(lambda j, i, f: (w.layer, f(j), 0), f=f)
            else:
                index_map = functools.partial(lambda j, i, f: (w.layer, 0, f(j)), f=f)
            in_specs.append(pl.BlockSpec((None,) + slab_shape, index_map))
        w_args = [w.arr] * n_w
    in_specs.extend(extra_specs)
    in_specs.extend(extra2_specs)
    scratch = []
    if cast:
        scratch.append(pltpu.VMEM((k, n_w * bn), BF16))
    if prefetch:
        scratch.append(pltpu.VMEM((n_w,) + slab_shape, F32))
        scratch.append(pltpu.SemaphoreType.DMA((n_w,)))
    scratch.extend(epilogue.scratch_shapes(bm, bn))
    counts = (len(extra), len(extra2), len(out_shapes), len(out2_shapes))
    kern = functools.partial(
        _mm_kernel, n_w=n_w, n_w_refs=len(w_args), counts=counts, epilogue=epilogue, cast=cast,
        transposed=w.transposed,
        prefetch=_Prefetch(tuple(w_col_blocks), w.layer, bn, n_j) if prefetch else None)
    res = pl.pallas_call(
        kern,
        grid=(n_j, m // bm),
        in_specs=in_specs,
        out_specs=list(out_specs) + list(out2_specs),
        out_shape=list(out_shapes) + list(out2_shapes),
        scratch_shapes=scratch,
        compiler_params=_params(("arbitrary", "arbitrary")),
        name=name,
    )(x, x2, *w_args, *extra, *extra2)
    return res[:len(out_shapes)], res[len(out_shapes):]


class _StoreEpilogue(_Epilogue):
    def __init__(self, n_col_steps, scaled_col_steps=0, scale=1.0, residual=False):
        self.n_col_steps = n_col_steps
        self.scaled_col_steps = scaled_col_steps
        self.scale = scale
        self.residual = residual

    def _store(self, acc, j, res_refs, out_ref):
        if self.scaled_col_steps:
            acc = acc * jnp.where(j < self.scaled_col_steps, self.scale, 1.0).astype(F32)
        if self.residual:
            acc = res_refs[0][...] + acc
        out_ref[...] = acc.astype(out_ref.dtype)

    def tile(self, acc, i, j, extra, outs, scratch):
        self._store(acc, j, extra, outs[0])

    def rows(self, acc, j, extra, extra2, outs2):
        self._store(acc, j, extra2, outs2[0])


def _linear(x, x2, w, *, col0, n, out_dtype, bn=MM_BLOCK_N, scaled_cols=0, scale=1.0,
            residual=None, residual2=None, bm=None, name):
    m, m2 = x.shape[0], x2.shape[0]
    if bm is None:
        bm = MM_BLOCK_M if out_dtype == BF16 else MM_BLOCK_M_F32_OUT
    bm = min(bm, m)
    assert n % bn == 0 and col0 % bn == 0 and scaled_cols % bn == 0
    assert (residual is None) == (residual2 is None)
    j0 = col0 // bn
    ep = _StoreEpilogue(n // bn, scaled_cols // bn, scale, residual is not None)
    tile_spec = pl.BlockSpec((bm, bn), lambda j, i: (i, j))
    rows_spec = pl.BlockSpec((m2, bn), lambda j, i: (0, j))
    has_res = residual is not None
    (out,), (out2,) = _matmul(
        x, x2, w, ep, w_col_blocks=[lambda j: j + j0], bn=bn, bm=bm,
        extra=(residual,) if has_res else (), extra_specs=(tile_spec,) if has_res else (),
        extra2=(residual2,) if has_res else (), extra2_specs=(rows_spec,) if has_res else (),
        out_shapes=[jax.ShapeDtypeStruct((m, n), out_dtype)], out_specs=[tile_spec],
        out2_shapes=[jax.ShapeDtypeStruct((m2, n), out_dtype)], out2_specs=[rows_spec],
        name=name, prefetch=w.arr.dtype != BF16)
    return out, out2


class _RopeEpilogue(_Epilogue):
    def __init__(self, n_col_steps, rope_col_steps, n_out):
        self.n_col_steps = n_col_steps
        self.rope_col_steps = rope_col_steps
        self.n_out = n_out

    def _store(self, acc, j, cos, sin, outs):
        def rotated():
            heads = []
            for h in range(acc.shape[1] // A_HEAD_DIM):
                xh = acc[:, h * A_HEAD_DIM:(h + 1) * A_HEAD_DIM]
                heads.append(xh * cos + pltpu.roll(xh, A_HEAD_DIM // 2, axis=1) * sin)
            return jnp.concatenate(heads, axis=1)

        if self.rope_col_steps >= self.n_col_steps:
            val = rotated()
        else:
            val = jnp.where(j < self.rope_col_steps, rotated(), acc)
        for o_ref in outs:
            o_ref[...] = val.astype(o_ref.dtype)

    def tile(self, acc, i, j, extra, outs, scratch):
        cos_ref, sin_ref = extra
        bm = acc.shape[0]
        rows = pl.ds(pl.multiple_of((i % (cos_ref.shape[0] // bm)) * bm, bm), bm)
        self._store(acc, j, cos_ref[rows, :], sin_ref[rows, :], outs)

    def rows(self, acc, j, extra, extra2, outs2):
        self._store(acc, j, extra2[0][...], extra2[1][...], outs2)


def _rope_linear(x, x2, w, tables, tables2, *, n, rope_cols, out_dtypes, rows_per_seq, name,
                 bn=MM_BLOCK_N):
    m, m2 = x.shape[0], x2.shape[0]
    bm = min(MM_BLOCK_M_F32_OUT, m, rows_per_seq)
    assert rows_per_seq % bm == 0 and n % bn == 0 and rope_cols % bn == 0
    tiles_per_seq = rows_per_seq // bm
    ep = _RopeEpilogue(n // bn, rope_cols // bn, len(out_dtypes))
    tab_spec = pl.BlockSpec((rows_per_seq, A_HEAD_DIM), lambda j, i: (0, 0))
    tab2_spec = pl.BlockSpec((m2, A_HEAD_DIM), lambda j, i: (0, 0))
    return _matmul(
        x, x2, w, ep, w_col_blocks=[lambda j: j], bn=bn, bm=bm,
        extra=tables, extra_specs=(tab_spec, tab_spec),
        extra2=tables2, extra2_specs=(tab2_spec, tab2_spec),
        out_shapes=[jax.ShapeDtypeStruct((m, n), dt) for dt in out_dtypes],
        out_specs=[pl.BlockSpec((bm, bn), lambda j, i: (i, j)) for _ in out_dtypes],
        out2_shapes=[jax.ShapeDtypeStruct((m2, n), dt) for dt in out_dtypes],
        out2_specs=[pl.BlockSpec((m2, bn), lambda j, i: (0, j)) for _ in out_dtypes],
        name=name, prefetch=True)


def _gates_kernel(x_ref, wt_ref, b_ref, o_ref):
    g = lax.dot_general(x_ref[...], wt_ref[...].astype(BF16), (((1,), (1,)), ((), ())),
                        preferred_element_type=F32) + b_ref[...]
    ig = GATE_SOFT_CAP * jnp.tanh(g / GATE_SOFT_CAP)
    lf = jnp.minimum(g, 0.0) - jnp.log1p(jnp.exp(-jnp.abs(g)))
    lane = lax.broadcasted_iota(jnp.int32, g.shape, 1)
    o_ref[...] = jnp.where(lane < M_HEADS, ig, lf)


def _mlstm_gates(xn, wt_gates, b_gates):
    m, d = xn.shape
    bm = min(MM_BLOCK_M, m)
    ng = 2 * M_HEADS
    return pl.pallas_call(
        _gates_kernel,
        grid=(m // bm,),
        in_specs=[pl.BlockSpec((bm, d), lambda i: (i, 0)),
                  pl.BlockSpec((ng, d), lambda i: (0, 0)),
                  pl.BlockSpec((1, ng), lambda i: (0, 0))],
        out_specs=pl.BlockSpec((bm, ng), lambda i: (i, 0)),
        out_shape=jax.ShapeDtypeStruct((m, ng), F32),
        compiler_params=_params(("arbitrary",)),
        name="mlstm_gates",
    )(xn, wt_gates, b_gates.reshape(1, ng).astype(F32))


def _mlstm_chunk_kernel(q_ref, k_ref, v_ref, o_ref, gcol_ref, grow_ref, c0_ref, n0_ref, m0_ref,
                        h_ref, c_ref, n_ref, m_ref):
    L = q_ref.shape[0]

    @pl.when(pl.program_id(1) == 0)
    def _():
        c_ref[...] = c0_ref[...]
        n_ref[...] = n0_ref[...]
        m_ref[...] = m0_ref[...]

    t_idx = lax.broadcasted_iota(jnp.int32, (L, L), 0)
    s_idx = lax.broadcasted_iota(jnp.int32, (L, L), 1)
    causal = s_idx <= t_idx
    qs = [q_ref[:, h * M_DK:(h + 1) * M_DK] for h in range(M_HEADS)]
    ks = [k_ref[:, h * M_DK:(h + 1) * M_DK] for h in range(M_HEADS)]
    vs = [v_ref[:, h * M_DV:(h + 1) * M_DV] for h in range(M_HEADS)]

    for g0 in range(0, M_HEADS, MLSTM_HEAD_GROUP):
        heads = range(g0, g0 + MLSTM_HEAD_GROUP)

        gw = {}
        for h in heads:
            ig_col = gcol_ref[:, h:h + 1]
            lf_col = gcol_ref[:, M_HEADS + h:M_HEADS + h + 1]
            ig_row = grow_ref[h:h + 1, :]
            lf_row = grow_ref[M_HEADS + h:M_HEADS + h + 1, :]
            m_prev = m_ref[0, h:h + 1, 0:1]
            b_col = jnp.sum(jnp.where(causal, lf_row, 0.0), axis=1, keepdims=True)
            b_row = jnp.sum(jnp.where(t_idx <= s_idx, lf_col, 0.0), axis=0, keepdims=True)
            dlog = jnp.where(causal, b_col - b_row + ig_row, -jnp.inf)
            g = b_col + m_prev
            m_t = jnp.maximum(g, jnp.max(dlog, axis=1, keepdims=True))
            w_intra = jnp.exp(dlog - m_t)
            w_inter = jnp.exp(g - m_t)
            b_last = b_col[L - 1:L, :]
            m_last = m_t[L - 1:L, :]
            w_end = jnp.exp(b_last - b_col + ig_col - m_last)
            gw[h] = (w_intra, w_inter, m_t, w_end)

        for h in heads:
            w_intra, w_inter, m_t, _ = gw[h]
            q, k, v = qs[h], ks[h], vs[h]
            c_prev = c_ref[0, h]
            n_prev = n_ref[0, h:h + 1, :]
            qk = lax.dot_general(q, k, (((1,), (1,)), ((), ())), preferred_element_type=F32)
            s = qk * w_intra
            num = (jnp.dot(s.astype(BF16), v, preferred_element_type=F32)
                   + w_inter * jnp.dot(q, c_prev.astype(BF16), preferred_element_type=F32))
            qn = jnp.sum(q.astype(F32) * n_prev, axis=1, keepdims=True)
            den = jnp.sum(s, axis=1, keepdims=True) + w_inter * qn
            inv = 1.0 / jnp.maximum(jnp.abs(den), jnp.exp(-m_t))
            gate = jax.nn.sigmoid(o_ref[:, h * M_DV:(h + 1) * M_DV])
            h_ref[:, h * M_DV:(h + 1) * M_DV] = (gate * (num * inv)).astype(h_ref.dtype)

        for h in heads:
            _, w_inter, m_t, w_end = gw[h]
            decay = w_inter[L - 1:L, :]
            kw = ks[h].astype(F32) * w_end
            c_ref[0, h] = decay * c_ref[0, h] + lax.dot_general(
                kw.astype(BF16), vs[h], (((0,), (0,)), ((), ())), preferred_element_type=F32)
            n_ref[0, h:h + 1, :] = decay * n_ref[0, h:h + 1, :] + jnp.sum(kw, axis=0, keepdims=True)
            m_ref[0, h:h + 1, :] = jnp.broadcast_to(m_t[L - 1:L, :], (1, LANES))


def _mlstm_chunks(qkv, o_pre, gates, c0, n0, m0, batch, seq):
    L = M_CHUNK
    assert seq % L == 0
    nc = seq // L
    qk_w = M_HEADS * M_DK
    v_w = M_HEADS * M_DV
    m_rows = batch * seq
    grow = gates.T
    m0b = jnp.broadcast_to(m0[:, :, None], (batch, M_HEADS, LANES))
    row = lambda b, c: b * nc + c
    hg, c_new, n_new, m_new = pl.pallas_call(
        _mlstm_chunk_kernel,
        grid=(batch, nc),
        in_specs=[
            pl.BlockSpec((L, qk_w), lambda b, c: (row(b, c), 0)),
            pl.BlockSpec((L, qk_w), lambda b, c: (row(b, c), 1)),
            pl.BlockSpec((L, v_w), lambda b, c: (row(b, c), 1)),
            pl.BlockSpec((L, v_w), lambda b, c: (row(b, c), 0)),
            pl.BlockSpec((L, 2 * M_HEADS), lambda b, c: (row(b, c), 0)),
            pl.BlockSpec((2 * M_HEADS, L), lambda b, c: (0, row(b, c))),
            pl.BlockSpec((1, M_HEADS, M_DK, M_DV), lambda b, c: (b, 0, 0, 0)),
            pl.BlockSpec((1, M_HEADS, M_DK), lambda b, c: (b, 0, 0)),
            pl.BlockSpec((1, M_HEADS, LANES), lambda b, c: (b, 0, 0)),
        ],
        out_specs=[
            pl.BlockSpec((L, v_w), lambda b, c: (row(b, c), 0)),
            pl.BlockSpec((1, M_HEADS, M_DK, M_DV), lambda b, c: (b, 0, 0, 0)),
            pl.BlockSpec((1, M_HEADS, M_DK), lambda b, c: (b, 0, 0)),
            pl.BlockSpec((1, M_HEADS, LANES), lambda b, c: (b, 0, 0)),
        ],
        out_shape=[
            jax.ShapeDtypeStruct((m_rows, v_w), BF16),
            jax.ShapeDtypeStruct((batch, M_HEADS, M_DK, M_DV), F32),
            jax.ShapeDtypeStruct((batch, M_HEADS, M_DK), F32),
            jax.ShapeDtypeStruct((batch, M_HEADS, LANES), F32),
        ],
        compiler_params=_params(("arbitrary", "arbitrary")),
        name="mlstm_chunks",
    )(qkv, qkv, qkv, o_pre, gates, grow, c0, n0, m0b)
    return hg, c_new, n_new, m_new[:, :, 0]


def _mlstm_step_kernel(q_ref, k_ref, v_ref, o_ref, g_ref, c0_ref, n0_ref, m0_ref,
                       h_ref, c_ref, n_ref, m_ref):
    dk_idx = lax.broadcasted_iota(jnp.int32, (M_DK, M_DK), 0)
    dk_lane = lax.broadcasted_iota(jnp.int32, (M_DK, M_DK), 1)
    eye = dk_idx == dk_lane
    m_rows = []
    for h in range(M_HEADS):
        q = q_ref[0, :, h * M_DK:(h + 1) * M_DK]
        k = k_ref[0, :, h * M_DK:(h + 1) * M_DK]
        v = v_ref[0, :, h * M_DV:(h + 1) * M_DV]
        ig = g_ref[0, :, h:h + 1]
        lf = g_ref[0, :, M_HEADS + h:M_HEADS + h + 1]
        c_prev = c0_ref[0, h]
        n_prev = n0_ref[0, h:h + 1, :]
        m_prev = m0_ref[0, h:h + 1, 0:1]
        qf, kf, vf = q.astype(F32), k.astype(F32), v.astype(F32)

        g = lf + m_prev
        m_t = jnp.maximum(g, ig)
        w_intra = jnp.exp(ig - m_t)
        w_inter = jnp.exp(g - m_t)
        s = jnp.sum(qf * kf, axis=1, keepdims=True) * w_intra
        q_rows = jnp.broadcast_to(q, (2 * SUBLANES, M_DK))
        qc = jnp.dot(q_rows, c_prev.astype(BF16), preferred_element_type=F32)[0:1, :]
        num = s * vf + w_inter * qc
        den = s + w_inter * jnp.sum(qf * n_prev, axis=1, keepdims=True)
        hh = num / jnp.maximum(jnp.abs(den), jnp.exp(-m_t))
        gate = jax.nn.sigmoid(o_ref[0, :, h * M_DV:(h + 1) * M_DV])
        h_ref[0, :, h * M_DV:(h + 1) * M_DV] = (gate * hh).astype(h_ref.dtype)

        kw = kf * w_intra
        kw_col = jnp.sum(jnp.where(eye, kw, 0.0), axis=1, keepdims=True)
        c_ref[0, h] = w_inter * c_prev + kw_col * vf
        n_ref[0, h:h + 1, :] = w_inter * n_prev + kw
        m_rows.append(jnp.broadcast_to(m_t, (1, LANES)))
    m_ref[0] = jnp.concatenate(m_rows, axis=0)


def _mlstm_step(qkv, o_pre, gates, c0, n0, m0):
    batch = qkv.shape[0]
    qk_w = M_HEADS * M_DK
    v_w = M_HEADS * M_DV
    qkv3 = qkv.reshape(batch, 1, 2 * qk_w + v_w)
    m0b = jnp.broadcast_to(m0[:, :, None], (batch, M_HEADS, LANES))
    hg, c_new, n_new, m_new = pl.pallas_call(
        _mlstm_step_kernel,
        grid=(batch,),
        in_specs=[
            pl.BlockSpec((1, 1, qk_w), lambda b: (b, 0, 0)),
            pl.BlockSpec((1, 1, qk_w), lambda b: (b, 0, 1)),
            pl.BlockSpec((1, 1, v_w), lambda b: (b, 0, 1)),
            pl.BlockSpec((1, 1, v_w), lambda b: (b, 0, 0)),
            pl.BlockSpec((1, 1, 2 * M_HEADS), lambda b: (b, 0, 0)),
            pl.BlockSpec((1, M_HEADS, M_DK, M_DV), lambda b: (b, 0, 0, 0)),
            pl.BlockSpec((1, M_HEADS, M_DK), lambda b: (b, 0, 0)),
            pl.BlockSpec((1, M_HEADS, LANES), lambda b: (b, 0, 0)),
        ],
        out_specs=[
            pl.BlockSpec((1, 1, v_w), lambda b: (b, 0, 0)),
            pl.BlockSpec((1, M_HEADS, M_DK, M_DV), lambda b: (b, 0, 0, 0)),
            pl.BlockSpec((1, M_HEADS, M_DK), lambda b: (b, 0, 0)),
            pl.BlockSpec((1, M_HEADS, LANES), lambda b: (b, 0, 0)),
        ],
        out_shape=[
            jax.ShapeDtypeStruct((batch, 1, v_w), BF16),
            jax.ShapeDtypeStruct((batch, M_HEADS, M_DK, M_DV), F32),
            jax.ShapeDtypeStruct((batch, M_HEADS, M_DK), F32),
            jax.ShapeDtypeStruct((batch, M_HEADS, LANES), F32),
        ],
        compiler_params=_params(("arbitrary",)),
        name="mlstm_step",
    )(qkv3, qkv3, qkv3, o_pre.reshape(batch, 1, v_w), gates.reshape(batch, 1, 2 * M_HEADS),
      c0, n0, m0b)
    return hg.reshape(batch, v_w), c_new, n_new, m_new[:, :, 0]


def _softmax_with_sink(scores, sink):
    mx = jnp.maximum(jnp.max(scores, axis=-1, keepdims=True), sink)
    e = jnp.exp(scores - mx)
    return e / (jnp.sum(e, axis=-1, keepdims=True) + jnp.exp(sink - mx))


def _swa_kernel(sink_ref, q_ref, kp_ref, ko_ref, vp_ref, vo_ref, o_ref):
    W = WINDOW
    D = A_HEAD_DIM
    blk = pl.program_id(1)
    i_idx = lax.broadcasted_iota(jnp.int32, (W, 2 * W), 0)
    j_idx = lax.broadcasted_iota(jnp.int32, (W, 2 * W), 1)
    first_key = jnp.where(blk > 0, 0, W)
    mask = (j_idx >= jnp.maximum(i_idx, first_key)) & (j_idx <= i_idx + W)
    for kv in range(A_KV_HEADS):
        cols = slice(kv * D, (kv + 1) * D)
        k_cat = jnp.concatenate([kp_ref[:, cols], ko_ref[:, cols]], axis=0)
        v_cat = jnp.concatenate([vp_ref[:, cols], vo_ref[:, cols]], axis=0)
        q = jnp.concatenate([q_ref[:, (kv * A_GROUP + g) * D:(kv * A_GROUP + g + 1) * D]
                             for g in range(A_GROUP)], axis=0)
        scores = lax.dot_general(q, k_cat, (((1,), (1,)), ((), ())), preferred_element_type=F32)
        scores = scores * (A_HEAD_DIM ** -0.5)
        probs = []
        for g in range(A_GROUP):
            sc = jnp.where(mask, scores[g * W:(g + 1) * W, :], NEG_INF)
            probs.append(_softmax_with_sink(sc, sink_ref[kv * A_GROUP + g]).astype(BF16))
        p = jnp.concatenate(probs, axis=0)
        o = jnp.dot(p, v_cat, preferred_element_type=F32)
        for g in range(A_GROUP):
            hq = kv * A_GROUP + g
            o_ref[:, hq * D:(hq + 1) * D] = o[g * W:(g + 1) * W, :].astype(o_ref.dtype)


def _swa_attention(q, kv, sinks, batch, seq):
    W = WINDOW
    assert seq % W == 0
    nb = seq // W
    kvw = A_KV_HEADS * A_HEAD_DIM
    qw = A_HEADS * A_HEAD_DIM
    row = lambda b, n: b * nb + n
    prev = lambda b, n: jnp.maximum(b * nb + n - 1, 0)
    return pl.pallas_call(
        _swa_kernel,
        grid=(batch, nb),
        in_specs=[
            pl.BlockSpec(memory_space=pltpu.SMEM),
            pl.BlockSpec((W, qw), lambda b, n: (row(b, n), 0)),
            pl.BlockSpec((W, kvw), lambda b, n: (prev(b, n), 0)),
            pl.BlockSpec((W, kvw), lambda b, n: (row(b, n), 0)),
            pl.BlockSpec((W, kvw), lambda b, n: (prev(b, n), 1)),
            pl.BlockSpec((W, kvw), lambda b, n: (row(b, n), 1)),
        ],
        out_specs=pl.BlockSpec((W, qw), lambda b, n: (row(b, n), 0)),
        out_shape=jax.ShapeDtypeStruct((batch * seq, qw), BF16),
        compiler_params=_params(("arbitrary", "arbitrary")),
        name="swa_attention",
    )(sinks.astype(F32), q, kv, kv, kv, kv)


def _decode_attn_kernel(sink_ref, q_ref, kn_ref, vn_ref, ck_ref, cv_ref, o_ref, kw_ref, vw_ref):
    W, D, KV, H = WINDOW, A_HEAD_DIM, A_KV_HEADS, A_HEADS
    scale = A_HEAD_DIM ** -0.5
    head_kv = lax.broadcasted_iota(jnp.int32, (H, W * KV), 0) // A_GROUP
    own_c = (lax.broadcasted_iota(jnp.int32, (H, W * KV), 1) % KV) == head_kv
    own_n = lax.broadcasted_iota(jnp.int32, (H, KV), 1) == head_kv[:, :KV]
    sink = sink_ref[:, 0:1]
    nt = (((1,), (1,)), ((), ()))
    for b in range(q_ref.shape[0]):
        q = q_ref[b]
        kc = ck_ref[b].reshape(W * KV, D).astype(BF16)
        vc = cv_ref[b].reshape(W * KV, D).astype(BF16)
        kn = kn_ref[b].astype(BF16)
        vn = vn_ref[b].astype(BF16)
        s_c = lax.dot_general(q, kc, nt, preferred_element_type=F32) * scale
        s_n = lax.dot_general(q, kn, nt, preferred_element_type=F32) * scale
        s_c = jnp.where(own_c, s_c, NEG_INF)
        s_n = jnp.where(own_n, s_n, NEG_INF)
        mx = jnp.maximum(jnp.maximum(jnp.max(s_c, axis=1, keepdims=True),
                                     jnp.max(s_n, axis=1, keepdims=True)), sink)
        e_c = jnp.exp(s_c - mx)
        e_n = jnp.exp(s_n - mx)
        den = (jnp.sum(e_c, axis=1, keepdims=True) + jnp.sum(e_n, axis=1, keepdims=True)
               + jnp.exp(sink - mx))
        o = (jnp.dot((e_c / den).astype(BF16), vc, preferred_element_type=F32)
             + jnp.dot((e_n / den).astype(BF16), vn, preferred_element_type=F32))
        o_ref[b] = o.astype(o_ref.dtype)
        kw_ref[b, 0:W - 1] = ck_ref[b, 1:W]
        kw_ref[b, W - 1] = kn_ref[b]
        vw_ref[b, 0:W - 1] = cv_ref[b, 1:W]
        vw_ref[b, W - 1] = vn_ref[b]


def _decode_attention(q, k_new, v_new, cache_k, cache_v, sinks):
    batch = q.shape[0]
    bb = math.gcd(batch, DECODE_ATTN_BLOCK_B)
    H, KV, D, W = A_HEADS, A_KV_HEADS, A_HEAD_DIM, WINDOW
    new_spec = pl.BlockSpec((bb, KV, D), lambda b: (b, 0, 0))
    win_spec = pl.BlockSpec((bb, W, KV, D), lambda b: (b, 0, 0, 0))
    o, k_win, v_win = pl.pallas_call(
        _decode_attn_kernel,
        grid=(batch // bb,),
        in_specs=[pl.BlockSpec((H, LANES), lambda b: (0, 0)),
                  pl.BlockSpec((bb, H, D), lambda b: (b, 0, 0)),
                  new_spec, new_spec, win_spec, win_spec],
        out_specs=[pl.BlockSpec((bb, H, D), lambda b: (b, 0, 0)), win_spec, win_spec],
        out_shape=[jax.ShapeDtypeStruct((batch, H, D), BF16),
                   jax.ShapeDtypeStruct((batch, W, KV, D), F32),
                   jax.ShapeDtypeStruct((batch, W, KV, D), F32)],
        compiler_params=_params(("arbitrary",)),
        name="decode_attention",
    )(jnp.broadcast_to(sinks.astype(F32)[:, None], (H, LANES)), q.reshape(batch, H, D),
      k_new.reshape(batch, KV, D), v_new.reshape(batch, KV, D), cache_k, cache_v)
    return o.reshape(batch, H * D), k_win, v_win


def _conv_act(g2, g1, g0, val, cw_ref, cb_ref):
    conv = cb_ref[...] + cw_ref[0:1, :] * g2
    conv = conv + cw_ref[1:2, :] * g1
    conv = conv + cw_ref[2:3, :] * g0
    return conv * jax.nn.sigmoid(conv) * val


class _ConvFfnEpilogue(_Epilogue):
    def __init__(self, n_col_steps, tiles_per_seq):
        self.n_col_steps = n_col_steps
        self.tiles_per_seq = tiles_per_seq

    def scratch_shapes(self, bm, bn):
        return [pltpu.VMEM((SUBLANES, bn), F32)]

    def init_scratch(self, scratch):
        scratch[0][...] = jnp.zeros_like(scratch[0])

    def rows(self, acc, j, extra, extra2, outs2):
        cw_ref, cb_ref = extra[:2]
        g2_ref, g1_ref, wd_ref = extra2
        act_ref, gate_ref, wd_bf_ref = outs2
        wd_bf_ref[...] = wd_ref[...].astype(wd_bf_ref.dtype)
        bn = acc.shape[1] // 2
        gate, val = acc[:, :bn], acc[:, bn:]
        act_ref[...] = _conv_act(g2_ref[...], g1_ref[...], gate, val, cw_ref, cb_ref).astype(act_ref.dtype)
        gate_ref[...] = gate

    def tile(self, acc, i, j, extra, outs, scratch):
        cw_ref, cb_ref, halo0_ref = extra
        act_ref, tail_ref = outs
        carry_ref, = scratch
        bm = acc.shape[0]
        bn = acc.shape[1] // 2
        gate, val = acc[:, :bn], acc[:, bn:]
        seq_start = (i % self.tiles_per_seq) == 0
        halo = jnp.where(seq_start, halo0_ref[i // self.tiles_per_seq], carry_ref[...])
        row = lax.broadcasted_iota(jnp.int32, (SUBLANES, bn), 0)

        def shifted(d):
            body = pltpu.roll(gate, d, axis=0)
            top = jnp.where(row < d, pltpu.roll(halo, d, axis=0), body[:SUBLANES])
            return jnp.concatenate([top, body[SUBLANES:]], axis=0)

        g1, g2 = shifted(1), shifted(2)
        act_ref[...] = _conv_act(g2, g1, gate, val, cw_ref, cb_ref).astype(act_ref.dtype)
        tail = gate[bm - SUBLANES:, :]
        carry_ref[...] = tail
        tail_ref[i] = tail


def _conv_ffn_up(xn, xn2, w_up, w_down, conv_w, conv_b, conv0, conv0_2, batch, seq):
    m, m2 = xn.shape[0], xn2.shape[0]
    bn = FFN_BLOCK_N
    assert D_FF % bn == 0
    n_j = D_FF // bn
    cw = conv_w.astype(F32)
    cb = conv_b.reshape(1, D_FF).astype(F32)
    col_spec = lambda rows: pl.BlockSpec((rows, bn), lambda j, i: (0, j))
    bm = min(MM_BLOCK_M, seq)
    assert seq % bm == 0 and bm % SUBLANES == 0
    tiles_per_seq = seq // bm
    n_i = m // bm
    halo0 = jnp.concatenate(
        [jnp.zeros((batch, SUBLANES - (CONV_W - 1), D_FF), F32), conv0.astype(F32)], axis=1)
    g2, g1 = conv0_2[:, 0, :], conv0_2[:, 1, :]
    rows_spec = pl.BlockSpec((m2, bn), lambda j, i: (0, j))
    k_down, n_down = w_down.arr.shape[1:]
    wd_rows = k_down // n_j
    assert wd_rows * n_j == k_down and wd_rows % (2 * SUBLANES) == 0 and not w_down.transposed
    (act, tails), (act2, gate2, wd_bf16) = _matmul(
        xn, xn2, w_up, _ConvFfnEpilogue(n_j, tiles_per_seq),
        w_col_blocks=[lambda j: j, lambda j: j + n_j], bn=bn, bm=bm,
        extra=(cw, cb, halo0),
        extra_specs=(col_spec(CONV_W), col_spec(1),
                     pl.BlockSpec((batch, SUBLANES, bn), lambda j, i: (0, 0, j))),
        extra2=(g2, g1, w_down.arr),
        extra2_specs=(rows_spec, rows_spec,
                      pl.BlockSpec((None, wd_rows, n_down), lambda j, i: (w_down.layer, j, 0))),
        out_shapes=[jax.ShapeDtypeStruct((m, D_FF), BF16),
                    jax.ShapeDtypeStruct((n_i, SUBLANES, D_FF), F32)],
        out_specs=[pl.BlockSpec((bm, bn), lambda j, i: (i, j)),
                   pl.BlockSpec((n_i, SUBLANES, bn), lambda j, i: (0, 0, j))],
        out2_shapes=[jax.ShapeDtypeStruct((m2, D_FF), BF16), jax.ShapeDtypeStruct((m2, D_FF), F32),
                     jax.ShapeDtypeStruct((k_down, n_down), BF16)],
        out2_specs=[rows_spec, rows_spec, pl.BlockSpec((wd_rows, n_down), lambda j, i: (j, 0))],
        name="conv_ffn_up", prefetch=True)
    last_tiles = tails.reshape(batch, tiles_per_seq, SUBLANES, D_FF)[:, -1]
    return ((act, last_tiles[:, SUBLANES - (CONV_W - 1):, :]),
            (act2, jnp.stack([g1, gate2], axis=1)), wd_bf16)


def _conv_ffn(h, h2, gain, w_up, conv_w, conv_b, w_down, conv0, conv0_2, batch, seq):
    gain = gain.reshape(1, D_MODEL)
    xn, = _rms_norm(h, gain, BF16)
    xn2, = _rms_norm(h2, gain, BF16)
    (act, conv_new), (act2, conv_new2), wd_bf16 = _conv_ffn_up(
        xn, xn2, w_up, w_down, conv_w, conv_b, conv0, conv0_2, batch, seq)
    out, out2 = _linear(act, act2, _Weight(wd_bf16[None], 0), col0=0, n=D_MODEL, out_dtype=F32,
                        bn=DOWN_BLOCK_N, bm=DOWN_BLOCK_M, residual=h, residual2=h2, name="ffn_down")
    return (out, conv_new), (out2, conv_new2)


def _rope_tables(pos):
    inv_freq = ROPE_THETA ** (-jnp.arange(0, A_HEAD_DIM, 2, dtype=F32) / A_HEAD_DIM)
    ang = pos.astype(F32)[:, None] * inv_freq[None, :]
    cos, sin = jnp.cos(ang), jnp.sin(ang)
    return jnp.concatenate([cos, cos], axis=1), jnp.concatenate([-sin, sin], axis=1)


def _model(x_p, x_s, state_p, state_s, k_win0, v_win0, w):
    batch, seq, _ = x_p.shape
    batch2 = x_s.shape[0]
    assert x_s.shape[1] == 1
    hp = x_p.reshape(batch * seq, D_MODEL)
    hs = x_s.reshape(batch2, D_MODEL)
    c0_p, n0_p, m0_p, conv0_p = state_p
    c0_s, n0_s, m0_s, conv0_s = state_s
    qk_w = M_HEADS * M_DK
    v_w = M_HEADS * M_DV
    kv_w = A_KV_HEADS * A_HEAD_DIM
    q_w = A_HEADS * A_HEAD_DIM

    g_mix0 = w["g_mix"][0:1]
    hn_p, = _rms_norm(hp, g_mix0, BF16)
    hn_s, = _rms_norm(hs, g_mix0, BF16)
    wt_in = _Weight(w["wt_mlstm_in"], 0, transposed=True)
    qkv_p, qkv_s = _linear(hn_p, hn_s, wt_in, col0=0, n=2 * qk_w + v_w, out_dtype=BF16,
                           scaled_cols=qk_w, scale=M_DK ** -0.5, name="mlstm_qkv")
    o_p, o_s = _linear(hn_p, hn_s, wt_in, col0=2 * qk_w + v_w, n=v_w, out_dtype=F32, name="mlstm_ogate")
    wt_gates = w["wt_mlstm_in"][0, 2 * qk_w + 2 * v_w:, :]
    gates_p = _mlstm_gates(hn_p, wt_gates, w["b_mlstm_gates"][0])
    gates_s = _mlstm_gates(hn_s, wt_gates, w["b_mlstm_gates"][0])
    hg_p, c_p, n_p, m_p = _mlstm_chunks(qkv_p, o_p, gates_p, c0_p[0], n0_p[0], m0_p[0], batch, seq)
    hg_s, c_s, n_s, m_s = _mlstm_step(qkv_s, o_s, gates_s, c0_s[0], n0_s[0], m0_s[0])
    hp, hs = _linear(hg_p, hg_s, _Weight(w["w_mlstm_out"], 0), col0=0, n=D_MODEL, out_dtype=F32,
                     residual=hp, residual2=hs, name="mlstm_out")
    (hp, conv_a_p), (hs, conv_a_s) = _conv_ffn(
        hp, hs, w["g_ffn"][0], _Weight(w["w_ffn_up"], 0), w["ffn_conv_w"][0], w["ffn_conv_b"][0],
        _Weight(w["w_ffn_down"], 0), conv0_p[0], conv0_s[0], batch, seq)

    g_kv_mix = jnp.stack([w["g_kv"], w["g_mix"][1]])
    kvn_p, hn_p = _rms_norm(hp, g_kv_mix, BF16)
    kvn_s, hn_s = _rms_norm(hs, g_kv_mix, BF16)
    tab_p = _rope_tables(jnp.arange(seq))
    tab_s = tuple(jnp.broadcast_to(t, (batch2, A_HEAD_DIM)) for t in _rope_tables(PAST_LEN + jnp.arange(1)))
    (kv_bf_p, kv_f32_p), (_, kv_f32_s) = _rope_linear(
        kvn_p, kvn_s, _Weight(w["w_kv"][None], 0), tab_p, tab_s, n=2 * kv_w, rope_cols=kv_w,
        out_dtypes=(BF16, F32), rows_per_seq=seq, name="kv_proj")
    (q_p,), (q_s,) = _rope_linear(
        hn_p, hn_s, _Weight(w["w_attn_q"], 0), tab_p, tab_s, n=q_w, rope_cols=q_w,
        out_dtypes=(BF16,), rows_per_seq=seq, name="q_proj")
    sinks = w["attn_sinks"][0]
    att_p = _swa_attention(q_p, kv_bf_p, sinks, batch, seq)
    kv4 = kv_f32_p.reshape(batch, seq, 2, A_KV_HEADS, A_HEAD_DIM)
    k_win_p, v_win_p = kv4[:, -WINDOW:, 0], kv4[:, -WINDOW:, 1]
    att_s, k_win_s, v_win_s = _decode_attention(
        q_s, kv_f32_s[:, :kv_w], kv_f32_s[:, kv_w:], k_win0, v_win0, sinks)
    hp, hs = _linear(att_p, att_s, _Weight(w["w_attn_o"], 0), col0=0, n=D_MODEL, out_dtype=F32,
                     residual=hp, residual2=hs, name="attn_out")
    (hp, conv_b_p), (hs, conv_b_s) = _conv_ffn(
        hp, hs, w["g_ffn"][1], _Weight(w["w_ffn_up"], 1), w["ffn_conv_w"][1], w["ffn_conv_b"][1],
        _Weight(w["w_ffn_down"], 1), conv0_p[1], conv0_s[1], batch, seq)
    g_final = w["g_final"].reshape(1, D_MODEL)
    y_p, = _rms_norm(hp, g_final, F32)
    y_s, = _rms_norm(hs, g_final, F32)
    out_p = (y_p.reshape(batch, seq, D_MODEL), c_p[None], n_p[None], m_p[None], k_win_p, v_win_p,
             jnp.stack([conv_a_p, conv_b_p]))
    out_s = (y_s.reshape(batch2, 1, D_MODEL), c_s[None], n_s[None], m_s[None], k_win_s, v_win_s,
             jnp.stack([conv_a_s, conv_b_s]))
    return out_p, out_s


def kernel(x_prompt, x_sample, state_mlstm_C, state_mlstm_n, state_mlstm_m, cache_k_win, cache_v_win,
           state_ffn_conv, g_mix, g_ffn, w_mlstm_in, b_mlstm_gates, w_mlstm_out, g_kv, w_kv, w_attn_q,
           attn_sinks, w_attn_o, w_ffn_up, ffn_conv_w, ffn_conv_b, w_ffn_down, g_final):
    w = dict(g_mix=g_mix, g_ffn=g_ffn, wt_mlstm_in=jnp.swapaxes(w_mlstm_in, 1, 2),
             b_mlstm_gates=b_mlstm_gates,
             w_mlstm_out=w_mlstm_out, g_kv=g_kv, w_kv=w_kv, w_attn_q=w_attn_q, attn_sinks=attn_sinks,
             w_attn_o=w_attn_o, w_ffn_up=w_ffn_up, ffn_conv_w=ffn_conv_w, ffn_conv_b=ffn_conv_b,
             w_ffn_down=w_ffn_down, g_final=g_final)
    bp = x_prompt.shape[0]
    n_a = state_mlstm_C.shape[0]
    depth = state_ffn_conv.shape[0]
    state_p = (jnp.zeros((n_a, bp, M_HEADS, M_DK, M_DV), F32), jnp.zeros((n_a, bp, M_HEADS, M_DK), F32),
               jnp.zeros((n_a, bp, M_HEADS), F32), jnp.zeros((depth, bp, CONV_W - 1, D_FF), F32))
    state_s = (state_mlstm_C, state_mlstm_n, state_mlstm_m, state_ffn_conv)
    prompt, sample = _model(x_prompt, x_sample, state_p, state_s, cache_k_win, cache_v_win, w)
    return (prompt[0], sample[0]) + prompt[1:] + sample[1:]
```

```python
import functools
import math
from typing import NamedTuple

import jax
import jax.numpy as jnp
from jax import lax
from jax.experimental import pallas as pl
from jax.experimental.pallas import tpu as pltpu

F32 = jnp.float32
BF16 = jnp.bfloat16

D_MODEL = 4096
PAST_LEN = 16384
M_HEADS = 8
M_DK = D_MODEL // (2 * M_HEADS)
M_DV = D_MODEL // M_HEADS
M_CHUNK = 128
GATE_SOFT_CAP = 15.0
A_HEADS = 32
A_KV_HEADS = 8
A_GROUP = A_HEADS // A_KV_HEADS
A_HEAD_DIM = D_MODEL // A_HEADS
WINDOW = 128
ROPE_THETA = 10000.0
NEG_INF = -1e30
D_FF = 11008
CONV_W = 3
RMS_EPS = 1e-6

V7X_VMEM_BYTES = 64 * 1024 * 1024
VMEM_LIMIT_BYTES = V7X_VMEM_BYTES - 8 * 1024 * 1024
LANES = 128
SUBLANES = 8

MM_BLOCK_M = 1024
MM_BLOCK_M_F32_OUT = 512
MM_BLOCK_N = 1024
FFN_BLOCK_N = 256
DELAYED_ROW_PIECES = 8
DOWN_BLOCK_M = 512
DOWN_BLOCK_N = 512
RMS_BLOCK_M = 512
DECODE_ATTN_BLOCK_B = 4
MLSTM_HEAD_GROUP = 8
CAST_ROWS = 256


def _params(sem):
    return pltpu.CompilerParams(dimension_semantics=sem, vmem_limit_bytes=VMEM_LIMIT_BYTES)


def _rms_kernel(x_ref, g_ref, *o_refs):
    xf = x_ref[...]
    r = lax.rsqrt(jnp.mean(xf * xf, axis=-1, keepdims=True) + RMS_EPS)
    xr = xf * r
    for k, o_ref in enumerate(o_refs):
        o_ref[...] = (xr * g_ref[k:k + 1, :]).astype(o_ref.dtype)


def _rms_norm(x, gains, out_dtype):
    m, d = x.shape
    n_g = gains.shape[0]
    bm = min(RMS_BLOCK_M, m)
    assert m % bm == 0
    outs = pl.pallas_call(
        _rms_kernel,
        grid=(m // bm,),
        in_specs=[pl.BlockSpec((bm, d), lambda i: (i, 0)),
                  pl.BlockSpec((n_g, d), lambda i: (0, 0))],
        out_specs=[pl.BlockSpec((bm, d), lambda i: (i, 0)) for _ in range(n_g)],
        out_shape=[jax.ShapeDtypeStruct((m, d), out_dtype) for _ in range(n_g)],
        compiler_params=_params(("arbitrary",)),
        name="rms_norm",
    )(x, gains)
    return outs


def _cast_weight(w_ref, wbf_ref, col0, ncols, transposed):
    k = wbf_ref.shape[0]
    rows_per = min(CAST_ROWS, k)
    assert k % rows_per == 0

    def body(r, carry):
        rows = pl.ds(pl.multiple_of(r * rows_per, rows_per), rows_per)
        if transposed:
            wbf_ref[rows, col0:col0 + ncols] = w_ref[:, rows].astype(BF16).T
        else:
            wbf_ref[rows, col0:col0 + ncols] = w_ref[rows, :].astype(BF16)
        return carry

    lax.fori_loop(0, k // rows_per, body, 0)


class _Prefetch(NamedTuple):
    col_blocks: tuple
    layer: int
    bn: int


def _mm_kernel(*refs, n_w, n_w_refs, counts, epilogue, cast, transposed, prefetch, n_i, n_j, delayed):
    n_extra, n_extra2, n_out, n_out2 = counts
    x_ref, x2_ref = refs[0], refs[1]
    refs = refs[2:]
    w_refs, refs = refs[:n_w_refs], refs[n_w_refs:]
    extra, refs = refs[:n_extra], refs[n_extra:]
    extra2, refs = refs[:n_extra2], refs[n_extra2:]
    outs, refs = refs[:n_out], refs[n_out:]
    outs2, scratch = refs[:n_out2], refs[n_out2:]
    bm = x_ref.shape[0]
    if delayed:
        n_tiles = n_j * n_i
        s = pl.program_id(0)
        cur = jnp.minimum(s, n_tiles - 1)
        prev = jnp.maximum(s - 1, 0)
        j, i = cur // n_i, cur % n_i
        j_prev, i_prev = prev // n_i, prev % n_i
    else:
        j = pl.program_id(0)
        i = pl.program_id(1)
    if cast:
        wbf_ref = scratch[0]
        scratch = scratch[1:]
    if prefetch is not None:
        w_hbm = w_refs[0]
        stage_ref, sem_ref = scratch[0], scratch[1]
        scratch = scratch[2:]
        w_refs = [stage_ref.at[t] for t in range(n_w)]

        def slab_copy(t, jj):
            start = pl.multiple_of(prefetch.col_blocks[t](jj) * prefetch.bn, prefetch.bn)
            cols = pl.ds(start, prefetch.bn)
            src = w_hbm.at[prefetch.layer, cols, :] if transposed else w_hbm.at[prefetch.layer, :, cols]
            return pltpu.make_async_copy(src, stage_ref.at[t], sem_ref.at[t])

    if delayed:
        acc_ref, acc2_ref = scratch[0], scratch[1]
        scratch = scratch[2:]
    w_ref = wbf_ref if cast else w_refs[0]

    def load_weights():
        if prefetch is not None:
            @pl.when(j == 0)
            def _():
                for t in range(n_w):
                    slab_copy(t, j).start()

            for t in range(n_w):
                slab_copy(t, j).wait()
        if cast:
            bn = wbf_ref.shape[1] // n_w
            for t, wt_ref in enumerate(w_refs):
                _cast_weight(wt_ref, wbf_ref, t * bn, bn, transposed)
        if prefetch is not None:
            @pl.when(j + 1 < n_j)
            def _():
                for t in range(n_w):
                    slab_copy(t, j + 1).start()

    def tile_dot(with_rows2):
        rows = x_ref[...]
        if with_rows2:
            rows = jnp.concatenate([rows, x2_ref[...]], axis=0)
        acc = jnp.dot(rows, w_ref[...], preferred_element_type=F32)
        return acc[:bm], acc[bm:]

    if not delayed:
        @pl.when(i == 0)
        def _():
            epilogue.init_scratch(scratch)
            load_weights()
            acc, acc2 = tile_dot(True)
            epilogue.tile(acc, i, j, extra, outs, scratch)
            epilogue.rows(acc2, j, extra, extra2, outs2)

        @pl.when(i != 0)
        def _():
            acc, _ = tile_dot(False)
            epilogue.tile(acc, i, j, extra, outs, scratch)

        return

    cm = bm // DELAYED_ROW_PIECES

    def step(multiply, rows2_now, rows2_prev):
        if rows2_prev:
            epilogue.rows(acc2_ref[...], j_prev, extra, extra2, outs2)
        for p in range(DELAYED_ROW_PIECES):
            rows = slice(p * cm, (p + 1) * cm)
            epilogue.piece(p, DELAYED_ROW_PIECES, acc_ref[rows, :], i_prev, j_prev, extra, outs, scratch)
            if multiply:
                lhs = x_ref[rows, :]
                last = p == DELAYED_ROW_PIECES - 1
                if rows2_now and last:
                    lhs = jnp.concatenate([lhs, x2_ref[...]], axis=0)
                acc = jnp.dot(lhs, w_ref[...], preferred_element_type=F32)
                acc_ref[rows, :] = acc[:cm]
                if rows2_now and last:
                    acc2_ref[...] = acc[cm:]

    @pl.when(s == 0)
    def _():
        acc_ref[...] = jnp.zeros_like(acc_ref)
        epilogue.init_scratch(scratch)

    @pl.when((s < n_tiles) & (i == 0))
    def _():
        load_weights()
        step(True, True, False)

    @pl.when((s < n_tiles) & (i == 1))
    def _():
        step(True, False, True)

    @pl.when((s < n_tiles) & (i > 1))
    def _():
        step(True, False, False)

    @pl.when(s == n_tiles)
    def _():
        step(False, False, False)


class _Epilogue:
    def scratch_shapes(self, bm, bn):
        return []

    def init_scratch(self, scratch):
        pass

    def tile(self, acc, i, j, extra, outs, scratch):
        self.piece(0, 1, acc, i, j, extra, outs, scratch)

    def piece(self, p, n_pieces, acc, i, j, extra, outs, scratch):
        raise NotImplementedError


class _Weight(NamedTuple):
    arr: jax.Array
    layer: int = 0
    transposed: bool = False

    @property
    def k(self):
        return self.arr.shape[2 if self.transposed else 1]


def _matmul(x, x2, w, epilogue, *, w_col_blocks, bn, bm=MM_BLOCK_M, extra=(), extra_specs=(),
            extra2=(), extra2_specs=(), out_shapes, out_specs, out2_shapes, out2_specs, name,
            prefetch=False, delayed=False):
    m, k = x.shape
    m2 = x2.shape[0]
    bm = min(bm, m)
    assert m % bm == 0 and w.k == k and x2.shape[1] == k
    cast = w.arr.dtype != BF16
    assert cast or not w.transposed
    n_w = len(w_col_blocks)
    n_j = epilogue.n_col_steps
    n_i = m // bm
    delayed = delayed and n_i >= 2
    x_specs = [pl.BlockSpec((bm, k), lambda j, i: (i, 0)),
               pl.BlockSpec((m2, k), lambda j, i: (0, 0))]
    slab_shape = (bn, k) if w.transposed else (k, bn)
    if prefetch:
        assert cast
        w_specs = [pl.BlockSpec(memory_space=pl.ANY)]
        w_args = [w.arr]
    else:
        w_specs = []
        for f in w_col_blocks:
            if w.transposed:
                index_map = functools.partial(lambda j, i, f: (w.layer, f(j), 0), f=f)
            else:
                index_map = functools.partial(lambda j, i, f: (w.layer, 0, f(j)), f=f)
            w_specs.append(pl.BlockSpec((None,) + slab_shape, index_map))
        w_args = [w.arr] * n_w
    late_specs = list(extra_specs) + list(extra2_specs)
    all_out_specs = list(out_specs) + list(out2_specs)
    if delayed:
        n_tiles = n_j * n_i

        def at_step(spec, lag):
            if spec.block_shape is None:
                return spec

            def index_map(s):
                t = jnp.clip(s - lag, 0, n_tiles - 1)
                return spec.index_map(t // n_i, t % n_i)
            return pl.BlockSpec(spec.block_shape, index_map)

        x_specs = [at_step(sp, 0) for sp in x_specs]
        w_specs = [at_step(sp, 0) for sp in w_specs]
        late_specs = [at_step(sp, 1) for sp in late_specs]
        all_out_specs = [at_step(sp, 1) for sp in all_out_specs]
        grid, semantics = (n_tiles + 1,), ("arbitrary",)
    else:
        grid, semantics = (n_j, n_i), ("arbitrary", "arbitrary")
    scratch = []
    if cast:
        scratch.append(pltpu.VMEM((k, n_w * bn), BF16))
    if prefetch:
        scratch.append(pltpu.VMEM((n_w,) + slab_shape, F32))
        scratch.append(pltpu.SemaphoreType.DMA((n_w,)))
    if delayed:
        scratch.append(pltpu.VMEM((bm, n_w * bn), F32))
        scratch.append(pltpu.VMEM((m2, n_w * bn), F32))
    scratch.extend(epilogue.scratch_shapes(bm, bn))
    counts = (len(extra), len(extra2), len(out_shapes), len(out2_shapes))
    kern = functools.partial(
        _mm_kernel, n_w=n_w, n_w_refs=len(w_args), counts=counts, epilogue=epilogue, cast=cast,
        transposed=w.transposed,
        prefetch=_Prefetch(tuple(w_col_blocks), w.layer, bn) if prefetch else None,
        n_i=n_i, n_j=n_j, delayed=delayed)
    res = pl.pallas_call(
        kern,
        grid=grid,
        in_specs=x_specs + w_specs + late_specs,
        out_specs=all_out_specs,
        out_shape=list(out_shapes) + list(out2_shapes),
        scratch_shapes=scratch,
        compiler_params=_params(semantics),
        name=name,
    )(x, x2, *w_args, *extra, *extra2)
    return res[:len(out_shapes)], res[len(out_shapes):]


class _StoreEpilogue(_Epilogue):
    def __init__(self, n_col_steps, scaled_col_steps=0, scale=1.0, residual=False):
        self.n_col_steps = n_col_steps
        self.scaled_col_steps = scaled_col_steps
        self.scale = scale
        self.residual = residual

    def _store(self, acc, j, res_refs, out_ref):
        if self.scaled_col_steps:
            acc = acc * jnp.where(j < self.scaled_col_steps, self.scale, 1.0).astype(F32)
        if self.residual:
            acc = res_refs[0][...] + acc
        out_ref[...] = acc.astype(out_ref.dtype)

    def tile(self, acc, i, j, extra, outs, scratch):
        self._store(acc, j, extra, outs[0])

    def rows(self, acc, j, extra, extra2, outs2):
        self._store(acc, j, extra2, outs2[0])


def _linear(x, x2, w, *, col0, n, out_dtype, bn=MM_BLOCK_N, scaled_cols=0, scale=1.0,
            residual=None, residual2=None, bm=None, name):
    m, m2 = x.shape[0], x2.shape[0]
    if bm is None:
        bm = MM_BLOCK_M if out_dtype == BF16 else MM_BLOCK_M_F32_OUT
    bm = min(bm, m)
    assert n % bn == 0 and col0 % bn == 0 and scaled_cols % bn == 0
    assert (residual is None) == (residual2 is None)
    j0 = col0 // bn
    ep = _StoreEpilogue(n // bn, scaled_cols // bn, scale, residual is not None)
    tile_spec = pl.BlockSpec((bm, bn), lambda j, i: (i, j))
    rows_spec = pl.BlockSpec((m2, bn), lambda j, i: (0, j))
    has_res = residual is not None
    (out,), (out2,) = _matmul(
        x, x2, w, ep, w_col_blocks=[lambda j: j + j0], bn=bn, bm=bm,
        extra=(residual,) if has_res else (), extra_specs=(tile_spec,) if has_res else (),
        extra2=(residual2,) if has_res else (), extra2_specs=(rows_spec,) if has_res else (),
        out_shapes=[jax.ShapeDtypeStruct((m, n), out_dtype)], out_specs=[tile_spec],
        out2_shapes=[jax.ShapeDtypeStruct((m2, n), out_dtype)], out2_specs=[rows_spec],
        name=name, prefetch=w.arr.dtype != BF16)
    return out, out2


class _RopeEpilogue(_Epilogue):
    def __init__(self, n_col_steps, rope_col_steps, n_out):
        self.n_col_steps = n_col_steps
        self.rope_col_steps = rope_col_steps
        self.n_out = n_out

    def _store(self, acc, j, cos, sin, outs):
        def rotated():
            heads = []
            for h in range(acc.shape[1] // A_HEAD_DIM):
                xh = acc[:, h * A_HEAD_DIM:(h + 1) * A_HEAD_DIM]
                heads.append(xh * cos + pltpu.roll(xh, A_HEAD_DIM // 2, axis=1) * sin)
            return jnp.concatenate(heads, axis=1)

        if self.rope_col_steps >= self.n_col_steps:
            val = rotated()
        else:
            val = jnp.where(j < self.rope_col_steps, rotated(), acc)
        for o_ref in outs:
            o_ref[...] = val.astype(o_ref.dtype)

    def tile(self, acc, i, j, extra, outs, scratch):
        cos_ref, sin_ref = extra
        bm = acc.shape[0]
        rows = pl.ds(pl.multiple_of((i % (cos_ref.shape[0] // bm)) * bm, bm), bm)
        self._store(acc, j, cos_ref[rows, :], sin_ref[rows, :], outs)

    def rows(self, acc, j, extra, extra2, outs2):
        self._store(acc, j, extra2[0][...], extra2[1][...], outs2)


def _rope_linear(x, x2, w, tables, tables2, *, n, rope_cols, out_dtypes, rows_per_seq, name,
                 bn=MM_BLOCK_N):
    m, m2 = x.shape[0], x2.shape[0]
    bm = min(MM_BLOCK_M_F32_OUT, m, rows_per_seq)
    assert rows_per_seq % bm == 0 and n % bn == 0 and rope_cols % bn == 0
    tiles_per_seq = rows_per_seq // bm
    ep = _RopeEpilogue(n // bn, rope_cols // bn, len(out_dtypes))
    tab_spec = pl.BlockSpec((rows_per_seq, A_HEAD_DIM), lambda j, i: (0, 0))
    tab2_spec = pl.BlockSpec((m2, A_HEAD_DIM), lambda j, i: (0, 0))
    return _matmul(
        x, x2, w, ep, w_col_blocks=[lambda j: j], bn=bn, bm=bm,
        extra=tables, extra_specs=(tab_spec, tab_spec),
        extra2=tables2, extra2_specs=(tab2_spec, tab2_spec),
        out_shapes=[jax.ShapeDtypeStruct((m, n), dt) for dt in out_dtypes],
        out_specs=[pl.BlockSpec((bm, bn), lambda j, i: (i, j)) for _ in out_dtypes],
        out2_shapes=[jax.ShapeDtypeStruct((m2, n), dt) for dt in out_dtypes],
        out2_specs=[pl.BlockSpec((m2, bn), lambda j, i: (0, j)) for _ in out_dtypes],
        name=name, prefetch=True)


def _gates_kernel(x_ref, wt_ref, b_ref, o_ref):
    g = lax.dot_general(x_ref[...], wt_ref[...].astype(BF16), (((1,), (1,)), ((), ())),
                        preferred_element_type=F32) + b_ref[...]
    ig = GATE_SOFT_CAP * jnp.tanh(g / GATE_SOFT_CAP)
    lf = jnp.minimum(g, 0.0) - jnp.log1p(jnp.exp(-jnp.abs(g)))
    lane = lax.broadcasted_iota(jnp.int32, g.shape, 1)
    o_ref[...] = jnp.where(lane < M_HEADS, ig, lf)


def _mlstm_gates(xn, wt_gates, b_gates):
    m, d = xn.shape
    bm = min(MM_BLOCK_M, m)
    ng = 2 * M_HEADS
    return pl.pallas_call(
        _gates_kernel,
        grid=(m // bm,),
        in_specs=[pl.BlockSpec((bm, d), lambda i: (i, 0)),
                  pl.BlockSpec((ng, d), lambda i: (0, 0)),
                  pl.BlockSpec((1, ng), lambda i: (0, 0))],
        out_specs=pl.BlockSpec((bm, ng), lambda i: (i, 0)),
        out_shape=jax.ShapeDtypeStruct((m, ng), F32),
        compiler_params=_params(("arbitrary",)),
        name="mlstm_gates",
    )(xn, wt_gates, b_gates.reshape(1, ng).astype(F32))


def _mlstm_chunk_kernel(q_ref, k_ref, v_ref, o_ref, gcol_ref, grow_ref, c0_ref, n0_ref, m0_ref,
                        h_ref, c_ref, n_ref, m_ref):
    L = q_ref.shape[0]

    @pl.when(pl.program_id(1) == 0)
    def _():
        c_ref[...] = c0_ref[...]
        n_ref[...] = n0_ref[...]
        m_ref[...] = m0_ref[...]

    t_idx = lax.broadcasted_iota(jnp.int32, (L, L), 0)
    s_idx = lax.broadcasted_iota(jnp.int32, (L, L), 1)
    causal = s_idx <= t_idx
    qs = [q_ref[:, h * M_DK:(h + 1) * M_DK] for h in range(M_HEADS)]
    ks = [k_ref[:, h * M_DK:(h + 1) * M_DK] for h in range(M_HEADS)]
    vs = [v_ref[:, h * M_DV:(h + 1) * M_DV] for h in range(M_HEADS)]

    for g0 in range(0, M_HEADS, MLSTM_HEAD_GROUP):
        heads = range(g0, g0 + MLSTM_HEAD_GROUP)

        gw = {}
        for h in heads:
            ig_col = gcol_ref[:, h:h + 1]
            lf_col = gcol_ref[:, M_HEADS + h:M_HEADS + h + 1]
            ig_row = grow_ref[h:h + 1, :]
            lf_row = grow_ref[M_HEADS + h:M_HEADS + h + 1, :]
            m_prev = m_ref[0, h:h + 1, 0:1]
            b_col = jnp.sum(jnp.where(causal, lf_row, 0.0), axis=1, keepdims=True)
            b_row = jnp.sum(jnp.where(t_idx <= s_idx, lf_col, 0.0), axis=0, keepdims=True)
            dlog = jnp.where(causal, b_col - b_row + ig_row, -jnp.inf)
            g = b_col + m_prev
            m_t = jnp.maximum(g, jnp.max(dlog, axis=1, keepdims=True))
            w_intra = jnp.exp(dlog - m_t)
            w_inter = jnp.exp(g - m_t)
            b_last = b_col[L - 1:L, :]
            m_last = m_t[L - 1:L, :]
            w_end = jnp.exp(b_last - b_col + ig_col - m_last)
            gw[h] = (w_intra, w_inter, m_t, w_end)

        for h in heads:
            w_intra, w_inter, m_t, _ = gw[h]
            q, k, v = qs[h], ks[h], vs[h]
            c_prev = c_ref[0, h]
            n_prev = n_ref[0, h:h + 1, :]
            qk = lax.dot_general(q, k, (((1,), (1,)), ((), ())), preferred_element_type=F32)
            s = qk * w_intra
            num = (jnp.dot(s.astype(BF16), v, preferred_element_type=F32)
                   + w_inter * jnp.dot(q, c_prev.astype(BF16), preferred_element_type=F32))
            qn = jnp.sum(q.astype(F32) * n_prev, axis=1, keepdims=True)
            den = jnp.sum(s, axis=1, keepdims=True) + w_inter * qn
            inv = 1.0 / jnp.maximum(jnp.abs(den), jnp.exp(-m_t))
            gate = jax.nn.sigmoid(o_ref[:, h * M_DV:(h + 1) * M_DV])
            h_ref[:, h * M_DV:(h + 1) * M_DV] = (gate * (num * inv)).astype(h_ref.dtype)

        for h in heads:
            _, w_inter, m_t, w_end = gw[h]
            decay = w_inter[L - 1:L, :]
            kw = ks[h].astype(F32) * w_end
            c_ref[0, h] = decay * c_ref[0, h] + lax.dot_general(
                kw.astype(BF16), vs[h], (((0,), (0,)), ((), ())), preferred_element_type=F32)
            n_ref[0, h:h + 1, :] = decay * n_ref[0, h:h + 1, :] + jnp.sum(kw, axis=0, keepdims=True)
            m_ref[0, h:h + 1, :] = jnp.broadcast_to(m_t[L - 1:L, :], (1, LANES))


def _mlstm_chunks(qkv, o_pre, gates, c0, n0, m0, batch, seq):
    L = M_CHUNK
    assert seq % L == 0
    nc = seq // L
    qk_w = M_HEADS * M_DK
    v_w = M_HEADS * M_DV
    m_rows = batch * seq
    grow = gates.T
    m0b = jnp.broadcast_to(m0[:, :, None], (batch, M_HEADS, LANES))
    row = lambda b, c: b * nc + c
    hg, c_new, n_new, m_new = pl.pallas_call(
        _mlstm_chunk_kernel,
        grid=(batch, nc),
        in_specs=[
            pl.BlockSpec((L, qk_w), lambda b, c: (row(b, c), 0)),
            pl.BlockSpec((L, qk_w), lambda b, c: (row(b, c), 1)),
            pl.BlockSpec((L, v_w), lambda b, c: (row(b, c), 1)),
            pl.BlockSpec((L, v_w), lambda b, c: (row(b, c), 0)),
            pl.BlockSpec((L, 2 * M_HEADS), lambda b, c: (row(b, c), 0)),
            pl.BlockSpec((2 * M_HEADS, L), lambda b, c: (0, row(b, c))),
            pl.BlockSpec((1, M_HEADS, M_DK, M_DV), lambda b, c: (b, 0, 0, 0)),
            pl.BlockSpec((1, M_HEADS, M_DK), lambda b, c: (b, 0, 0)),
            pl.BlockSpec((1, M_HEADS, LANES), lambda b, c: (b, 0, 0)),
        ],
        out_specs=[
            pl.BlockSpec((L, v_w), lambda b, c: (row(b, c), 0)),
            pl.BlockSpec((1, M_HEADS, M_DK, M_DV), lambda b, c: (b, 0, 0, 0)),
            pl.BlockSpec((1, M_HEADS, M_DK), lambda b, c: (b, 0, 0)),
            pl.BlockSpec((1, M_HEADS, LANES), lambda b, c: (b, 0, 0)),
        ],
        out_shape=[
            jax.ShapeDtypeStruct((m_rows, v_w), BF16),
            jax.ShapeDtypeStruct((batch, M_HEADS, M_DK, M_DV), F32),
            jax.ShapeDtypeStruct((batch, M_HEADS, M_DK), F32),
            jax.ShapeDtypeStruct((batch, M_HEADS, LANES), F32),
        ],
        compiler_params=_params(("arbitrary", "arbitrary")),
        name="mlstm_chunks",
    )(qkv, qkv, qkv, o_pre, gates, grow, c0, n0, m0b)
    return hg, c_new, n_new, m_new[:, :, 0]


def _mlstm_step_kernel(q_ref, k_ref, v_ref, o_ref, g_ref, c0_ref, n0_ref, m0_ref,
                       h_ref, c_ref, n_ref, m_ref):
    dk_idx = lax.broadcasted_iota(jnp.int32, (M_DK, M_DK), 0)
    dk_lane = lax.broadcasted_iota(jnp.int32, (M_DK, M_DK), 1)
    eye = dk_idx == dk_lane
    m_rows = []
    for h in range(M_HEADS):
        q = q_ref[0, :, h * M_DK:(h + 1) * M_DK]
        k = k_ref[0, :, h * M_DK:(h + 1) * M_DK]
        v = v_ref[0, :, h * M_DV:(h + 1) * M_DV]
        ig = g_ref[0, :, h:h + 1]
        lf = g_ref[0, :, M_HEADS + h:M_HEADS + h + 1]
        c_prev = c0_ref[0, h]
        n_prev = n0_ref[0, h:h + 1, :]
        m_prev = m0_ref[0, h:h + 1, 0:1]
        qf, kf, vf = q.astype(F32), k.astype(F32), v.astype(F32)

        g = lf + m_prev
        m_t = jnp.maximum(g, ig)
        w_intra = jnp.exp(ig - m_t)
        w_inter = jnp.exp(g - m_t)
        s = jnp.sum(qf * kf, axis=1, keepdims=True) * w_intra
        q_rows = jnp.broadcast_to(q, (2 * SUBLANES, M_DK))
        qc = jnp.dot(q_rows, c_prev.astype(BF16), preferred_element_type=F32)[0:1, :]
        num = s * vf + w_inter * qc
        den = s + w_inter * jnp.sum(qf * n_prev, axis=1, keepdims=True)
        hh = num / jnp.maximum(jnp.abs(den), jnp.exp(-m_t))
        gate = jax.nn.sigmoid(o_ref[0, :, h * M_DV:(h + 1) * M_DV])
        h_ref[0, :, h * M_DV:(h + 1) * M_DV] = (gate * hh).astype(h_ref.dtype)

        kw = kf * w_intra
        kw_col = jnp.sum(jnp.where(eye, kw, 0.0), axis=1, keepdims=True)
        c_ref[0, h] = w_inter * c_prev + kw_col * vf
        n_ref[0, h:h + 1, :] = w_inter * n_prev + kw
        m_rows.append(jnp.broadcast_to(m_t, (1, LANES)))
    m_ref[0] = jnp.concatenate(m_rows, axis=0)


def _mlstm_step(qkv, o_pre, gates, c0, n0, m0):
    batch = qkv.shape[0]
    qk_w = M_HEADS * M_DK
    v_w = M_HEADS * M_DV
    qkv3 = qkv.reshape(batch, 1, 2 * qk_w + v_w)
    m0b = jnp.broadcast_to(m0[:, :, None], (batch, M_HEADS, LANES))
    hg, c_new, n_new, m_new = pl.pallas_call(
        _mlstm_step_kernel,
        grid=(batch,),
        in_specs=[
            pl.BlockSpec((1, 1, qk_w), lambda b: (b, 0, 0)),
            pl.BlockSpec((1, 1, qk_w), lambda b: (b, 0, 1)),
            pl.BlockSpec((1, 1, v_w), lambda b: (b, 0, 1)),
            pl.BlockSpec((1, 1, v_w), lambda b: (b, 0, 0)),
            pl.BlockSpec((1, 1, 2 * M_HEADS), lambda b: (b, 0, 0)),
            pl.BlockSpec((1, M_HEADS, M_DK, M_DV), lambda b: (b, 0, 0, 0)),
            pl.BlockSpec((1, M_HEADS, M_DK), lambda b: (b, 0, 0)),
            pl.BlockSpec((1, M_HEADS, LANES), lambda b: (b, 0, 0)),
        ],
        out_specs=[
            pl.BlockSpec((1, 1, v_w), lambda b: (b, 0, 0)),
            pl.BlockSpec((1, M_HEADS, M_DK, M_DV), lambda b: (b, 0, 0, 0)),
            pl.BlockSpec((1, M_HEADS, M_DK), lambda b: (b, 0, 0)),
            pl.BlockSpec((1, M_HEADS, LANES), lambda b: (b, 0, 0)),
        ],
        out_shape=[
            jax.ShapeDtypeStruct((batch, 1, v_w), BF16),
            jax.ShapeDtypeStruct((batch, M_HEADS, M_DK, M_DV), F32),
            jax.ShapeDtypeStruct((batch, M_HEADS, M_DK), F32),
            jax.ShapeDtypeStruct((batch, M_HEADS, LANES), F32),
        ],
        compiler_params=_params(("arbitrary",)),
        name="mlstm_step",
    )(qkv3, qkv3, qkv3, o_pre.reshape(batch, 1, v_w), gates.reshape(batch, 1, 2 * M_HEADS),
      c0, n0, m0b)
    return hg.reshape(batch, v_w), c_new, n_new, m_new[:, :, 0]


def _softmax_with_sink(scores, sink):
    mx = jnp.maximum(jnp.max(scores, axis=-1, keepdims=True), sink)
    e = jnp.exp(scores - mx)
    return e / (jnp.sum(e, axis=-1, keepdims=True) + jnp.exp(sink - mx))


def _swa_kernel(sink_ref, q_ref, kp_ref, ko_ref, vp_ref, vo_ref, o_ref):
    W = WINDOW
    D = A_HEAD_DIM
    blk = pl.program_id(1)
    i_idx = lax.broadcasted_iota(jnp.int32, (W, 2 * W), 0)
    j_idx = lax.broadcasted_iota(jnp.int32, (W, 2 * W), 1)
    first_key = jnp.where(blk > 0, 0, W)
    mask = (j_idx >= jnp.maximum(i_idx, first_key)) & (j_idx <= i_idx + W)
    for kv in range(A_KV_HEADS):
        cols = slice(kv * D, (kv + 1) * D)
        k_cat = jnp.concatenate([kp_ref[:, cols], ko_ref[:, cols]], axis=0)
        v_cat = jnp.concatenate([vp_ref[:, cols], vo_ref[:, cols]], axis=0)
        q = jnp.concatenate([q_ref[:, (kv * A_GROUP + g) * D:(kv * A_GROUP + g + 1) * D]
                             for g in range(A_GROUP)], axis=0)
        scores = lax.dot_general(q, k_cat, (((1,), (1,)), ((), ())), preferred_element_type=F32)
        scores = scores * (A_HEAD_DIM ** -0.5)
        probs = []
        for g in range(A_GROUP):
            sc = jnp.where(mask, scores[g * W:(g + 1) * W, :], NEG_INF)
            probs.append(_softmax_with_sink(sc, sink_ref[kv * A_GROUP + g]).astype(BF16))
        p = jnp.concatenate(probs, axis=0)
        o = jnp.dot(p, v_cat, preferred_element_type=F32)
        for g in range(A_GROUP):
            hq = kv * A_GROUP + g
            o_ref[:, hq * D:(hq + 1) * D] = o[g * W:(g + 1) * W, :].astype(o_ref.dtype)


def _swa_attention(q, kv, sinks, batch, seq):
    W = WINDOW
    assert seq % W == 0
    nb = seq // W
    kvw = A_KV_HEADS * A_HEAD_DIM
    qw = A_HEADS * A_HEAD_DIM
    row = lambda b, n: b * nb + n
    prev = lambda b, n: jnp.maximum(b * nb + n - 1, 0)
    return pl.pallas_call(
        _swa_kernel,
        grid=(batch, nb),
        in_specs=[
            pl.BlockSpec(memory_space=pltpu.SMEM),
            pl.BlockSpec((W, qw), lambda b, n: (row(b, n), 0)),
            pl.BlockSpec((W, kvw), lambda b, n: (prev(b, n), 0)),
            pl.BlockSpec((W, kvw), lambda b, n: (row(b, n), 0)),
            pl.BlockSpec((W, kvw), lambda b, n: (prev(b, n), 1)),
            pl.BlockSpec((W, kvw), lambda b, n: (row(b, n), 1)),
        ],
        out_specs=pl.BlockSpec((W, qw), lambda b, n: (row(b, n), 0)),
        out_shape=jax.ShapeDtypeStruct((batch * seq, qw), BF16),
        compiler_params=_params(("arbitrary", "arbitrary")),
        name="swa_attention",
    )(sinks.astype(F32), q, kv, kv, kv, kv)


def _decode_attn_kernel(sink_ref, q_ref, kn_ref, vn_ref, ck_ref, cv_ref, o_ref, kw_ref, vw_ref):
    W, D, KV, H = WINDOW, A_HEAD_DIM, A_KV_HEADS, A_HEADS
    scale = A_HEAD_DIM ** -0.5
    head_kv = lax.broadcasted_iota(jnp.int32, (H, W * KV), 0) // A_GROUP
    own_c = (lax.broadcasted_iota(jnp.int32, (H, W * KV), 1) % KV) == head_kv
    own_n = lax.broadcasted_iota(jnp.int32, (H, KV), 1) == head_kv[:, :KV]
    sink = sink_ref[:, 0:1]
    nt = (((1,), (1,)), ((), ()))
    for b in range(q_ref.shape[0]):
        q = q_ref[b]
        kc = ck_ref[b].reshape(W * KV, D).astype(BF16)
        vc = cv_ref[b].reshape(W * KV, D).astype(BF16)
        kn = kn_ref[b].astype(BF16)
        vn = vn_ref[b].astype(BF16)
        s_c = lax.dot_general(q, kc, nt, preferred_element_type=F32) * scale
        s_n = lax.dot_general(q, kn, nt, preferred_element_type=F32) * scale
        s_c = jnp.where(own_c, s_c, NEG_INF)
        s_n = jnp.where(own_n, s_n, NEG_INF)
        mx = jnp.maximum(jnp.maximum(jnp.max(s_c, axis=1, keepdims=True),
                                     jnp.max(s_n, axis=1, keepdims=True)), sink)
        e_c = jnp.exp(s_c - mx)
        e_n = jnp.exp(s_n - mx)
        den = (jnp.sum(e_c, axis=1, keepdims=True) + jnp.sum(e_n, axis=1, keepdims=True)
               + jnp.exp(sink - mx))
        o = (jnp.dot((e_c / den).astype(BF16), vc, preferred_element_type=F32)
             + jnp.dot((e_n / den).astype(BF16), vn, preferred_element_type=F32))
        o_ref[b] = o.astype(o_ref.dtype)
        kw_ref[b, 0:W - 1] = ck_ref[b, 1:W]
        kw_ref[b, W - 1] = kn_ref[b]
        vw_ref[b, 0:W - 1] = cv_ref[b, 1:W]
        vw_ref[b, W - 1] = vn_ref[b]


def _decode_attention(q, k_new, v_new, cache_k, cache_v, sinks):
    batch = q.shape[0]
    bb = math.gcd(batch, DECODE_ATTN_BLOCK_B)
    H, KV, D, W = A_HEADS, A_KV_HEADS, A_HEAD_DIM, WINDOW
    new_spec = pl.BlockSpec((bb, KV, D), lambda b: (b, 0, 0))
    win_spec = pl.BlockSpec((bb, W, KV, D), lambda b: (b, 0, 0, 0))
    o, k_win, v_win = pl.pallas_call(
        _decode_attn_kernel,
        grid=(batch // bb,),
        in_specs=[pl.BlockSpec((H, LANES), lambda b: (0, 0)),
                  pl.BlockSpec((bb, H, D), lambda b: (b, 0, 0)),
                  new_spec, new_spec, win_spec, win_spec],
        out_specs=[pl.BlockSpec((bb, H, D), lambda b: (b, 0, 0)), win_spec, win_spec],
        out_shape=[jax.ShapeDtypeStruct((batch, H, D), BF16),
                   jax.ShapeDtypeStruct((batch, W, KV, D), F32),
                   jax.ShapeDtypeStruct((batch, W, KV, D), F32)],
        compiler_params=_params(("arbitrary",)),
        name="decode_attention",
    )(jnp.broadcast_to(sinks.astype(F32)[:, None], (H, LANES)), q.reshape(batch, H, D),
      k_new.reshape(batch, KV, D), v_new.reshape(batch, KV, D), cache_k, cache_v)
    return o.reshape(batch, H * D), k_win, v_win


def _conv_act(g2, g1, g0, val, cw_ref, cb_ref):
    conv = cb_ref[...] + cw_ref[0:1, :] * g2
    conv = conv + cw_ref[1:2, :] * g1
    conv = conv + cw_ref[2:3, :] * g0
    return conv * jax.nn.sigmoid(conv) * val


class _ConvFfnEpilogue(_Epilogue):
    def __init__(self, n_col_steps, tiles_per_seq):
        self.n_col_steps = n_col_steps
        self.tiles_per_seq = tiles_per_seq

    def scratch_shapes(self, bm, bn):
        return [pltpu.VMEM((SUBLANES, bn), F32)]

    def init_scratch(self, scratch):
        scratch[0][...] = jnp.zeros_like(scratch[0])

    def rows(self, acc, j, extra, extra2, outs2):
        cw_ref, cb_ref = extra[:2]
        g2_ref, g1_ref, wd_ref = extra2
        act_ref, gate_ref, wd_bf_ref = outs2
        wd_bf_ref[...] = wd_ref[...].astype(wd_bf_ref.dtype)
        bn = acc.shape[1] // 2
        gate, val = acc[:, :bn], acc[:, bn:]
        act_ref[...] = _conv_act(g2_ref[...], g1_ref[...], gate, val, cw_ref, cb_ref).astype(act_ref.dtype)
        gate_ref[...] = gate

    def piece(self, p, n_pieces, acc, i, j, extra, outs, scratch):
        cw_ref, cb_ref, halo0_ref = extra
        act_ref, tail_ref = outs
        carry_ref, = scratch
        rows = acc.shape[0]
        bn = acc.shape[1] // 2
        gate, val = acc[:, :bn], acc[:, bn:]
        halo = carry_ref[...]
        if p == 0:
            seq_start = (i % self.tiles_per_seq) == 0
            halo = jnp.where(seq_start, halo0_ref[i // self.tiles_per_seq], halo)
        row = lax.broadcasted_iota(jnp.int32, (SUBLANES, bn), 0)

        def shifted(d):
            body = pltpu.roll(gate, d, axis=0)
            top = jnp.where(row < d, pltpu.roll(halo, d, axis=0), body[:SUBLANES])
            return jnp.concatenate([top, body[SUBLANES:]], axis=0)

        g1, g2 = shifted(1), shifted(2)
        act = _conv_act(g2, g1, gate, val, cw_ref, cb_ref).astype(act_ref.dtype)
        act_ref[p * rows:(p + 1) * rows, :] = act
        tail = gate[rows - SUBLANES:, :]
        carry_ref[...] = tail
        if p == n_pieces - 1:
            tail_ref[i] = tail


def _conv_ffn_up(xn, xn2, w_up, w_down, conv_w, conv_b, conv0, conv0_2, batch, seq):
    m, m2 = xn.shape[0], xn2.shape[0]
    bn = FFN_BLOCK_N
    assert D_FF % bn == 0
    n_j = D_FF // bn
    cw = conv_w.astype(F32)
    cb = conv_b.reshape(1, D_FF).astype(F32)
    col_spec = lambda rows: pl.BlockSpec((rows, bn), lambda j, i: (0, j))
    bm = min(MM_BLOCK_M, seq)
    assert seq % bm == 0 and bm % SUBLANES == 0
    tiles_per_seq = seq // bm
    n_i = m // bm
    halo0 = jnp.concatenate(
        [jnp.zeros((batch, SUBLANES - (CONV_W - 1), D_FF), F32), conv0.astype(F32)], axis=1)
    g2, g1 = conv0_2[:, 0, :], conv0_2[:, 1, :]
    rows_spec = pl.BlockSpec((m2, bn), lambda j, i: (0, j))
    k_down, n_down = w_down.arr.shape[1:]
    wd_rows = k_down // n_j
    assert wd_rows * n_j == k_down and wd_rows % (2 * SUBLANES) == 0 and not w_down.transposed
    (act, tails), (act2, gate2, wd_bf16) = _matmul(
        xn, xn2, w_up, _ConvFfnEpilogue(n_j, tiles_per_seq),
        w_col_blocks=[lambda j: j, lambda j: j + n_j], bn=bn, bm=bm,
        extra=(cw, cb, halo0),
        extra_specs=(col_spec(CONV_W), col_spec(1),
                     pl.BlockSpec((batch, SUBLANES, bn), lambda j, i: (0, 0, j))),
        extra2=(g2, g1, w_down.arr),
        extra2_specs=(rows_spec, rows_spec,
                      pl.BlockSpec((None, wd_rows, n_down), lambda j, i: (w_down.layer, j, 0))),
        out_shapes=[jax.ShapeDtypeStruct((m, D_FF), BF16),
                    jax.ShapeDtypeStruct((n_i, SUBLANES, D_FF), F32)],
        out_specs=[pl.BlockSpec((bm, bn), lambda j, i: (i, j)),
                   pl.BlockSpec((n_i, SUBLANES, bn), lambda j, i: (0, 0, j))],
        out2_shapes=[jax.ShapeDtypeStruct((m2, D_FF), BF16), jax.ShapeDtypeStruct((m2, D_FF), F32),
                     jax.ShapeDtypeStruct((k_down, n_down), BF16)],
        out2_specs=[rows_spec, rows_spec, pl.BlockSpec((wd_rows, n_down), lambda j, i: (j, 0))],
        name="conv_ffn_up", prefetch=True, delayed=True)
    last_tiles = tails.reshape(batch, tiles_per_seq, SUBLANES, D_FF)[:, -1]
    return ((act, last_tiles[:, SUBLANES - (CONV_W - 1):, :]),
            (act2, jnp.stack([g1, gate2], axis=1)), wd_bf16)


def _conv_ffn(h, h2, gain, w_up, conv_w, conv_b, w_down, conv0, conv0_2, batch, seq):
    gain = gain.reshape(1, D_MODEL)
    xn, = _rms_norm(h, gain, BF16)
    xn2, = _rms_norm(h2, gain, BF16)
    (act, conv_new), (act2, conv_new2), wd_bf16 = _conv_ffn_up(
        xn, xn2, w_up, w_down, conv_w, conv_b, conv0, conv0_2, batch, seq)
    out, out2 = _linear(act, act2, _Weight(wd_bf16[None], 0), col0=0, n=D_MODEL, out_dtype=F32,
                        bn=DOWN_BLOCK_N, bm=DOWN_BLOCK_M, residual=h, residual2=h2, name="ffn_down")
    return (out, conv_new), (out2, conv_new2)


def _rope_tables(pos):
    inv_freq = ROPE_THETA ** (-jnp.arange(0, A_HEAD_DIM, 2, dtype=F32) / A_HEAD_DIM)
    ang = pos.astype(F32)[:, None] * inv_freq[None, :]
    cos, sin = jnp.cos(ang), jnp.sin(ang)
    return jnp.concatenate([cos, cos], axis=1), jnp.concatenate([-sin, sin], axis=1)


def _model(x_p, x_s, state_p, state_s, k_win0, v_win0, w):
    batch, seq, _ = x_p.shape
    batch2 = x_s.shape[0]
    assert x_s.shape[1] == 1
    hp = x_p.reshape(batch * seq, D_MODEL)
    hs = x_s.reshape(batch2, D_MODEL)
    c0_p, n0_p, m0_p, conv0_p = state_p
    c0_s, n0_s, m0_s, conv0_s = state_s
    qk_w = M_HEADS * M_DK
    v_w = M_HEADS * M_DV
    kv_w = A_KV_HEADS * A_HEAD_DIM
    q_w = A_HEADS * A_HEAD_DIM

    g_mix0 = w["g_mix"][0:1]
    hn_p, = _rms_norm(hp, g_mix0, BF16)
    hn_s, = _rms_norm(hs, g_mix0, BF16)
    wt_in = _Weight(w["wt_mlstm_in"], 0, transposed=True)
    qkv_p, qkv_s = _linear(hn_p, hn_s, wt_in, col0=0, n=2 * qk_w + v_w, out_dtype=BF16,
                           scaled_cols=qk_w, scale=M_DK ** -0.5, name="mlstm_qkv")
    o_p, o_s = _linear(hn_p, hn_s, wt_in, col0=2 * qk_w + v_w, n=v_w, out_dtype=F32, name="mlstm_ogate")
    wt_gates = w["wt_mlstm_in"][0, 2 * qk_w + 2 * v_w:, :]
    gates_p = _mlstm_gates(hn_p, wt_gates, w["b_mlstm_gates"][0])
    gates_s = _mlstm_gates(hn_s, wt_gates, w["b_mlstm_gates"][0])
    hg_p, c_p, n_p, m_p = _mlstm_chunks(qkv_p, o_p, gates_p, c0_p[0], n0_p[0], m0_p[0], batch, seq)
    hg_s, c_s, n_s, m_s = _mlstm_step(qkv_s, o_s, gates_s, c0_s[0], n0_s[0], m0_s[0])
    hp, hs = _linear(hg_p, hg_s, _Weight(w["w_mlstm_out"], 0), col0=0, n=D_MODEL, out_dtype=F32,
                     residual=hp, residual2=hs, name="mlstm_out")
    (hp, conv_a_p), (hs, conv_a_s) = _conv_ffn(
        hp, hs, w["g_ffn"][0], _Weight(w["w_ffn_up"], 0), w["ffn_conv_w"][0], w["ffn_conv_b"][0],
        _Weight(w["w_ffn_down"], 0), conv0_p[0], conv0_s[0], batch, seq)

    g_kv_mix = jnp.stack([w["g_kv"], w["g_mix"][1]])
    kvn_p, hn_p = _rms_norm(hp, g_kv_mix, BF16)
    kvn_s, hn_s = _rms_norm(hs, g_kv_mix, BF16)
    tab_p = _rope_tables(jnp.arange(seq))
    tab_s = tuple(jnp.broadcast_to(t, (batch2, A_HEAD_DIM)) for t in _rope_tables(PAST_LEN + jnp.arange(1)))
    (kv_bf_p, kv_f32_p), (_, kv_f32_s) = _rope_linear(
        kvn_p, kvn_s, _Weight(w["w_kv"][None], 0), tab_p, tab_s, n=2 * kv_w, rope_cols=kv_w,
        out_dtypes=(BF16, F32), rows_per_seq=seq, name="kv_proj")
    (q_p,), (q_s,) = _rope_linear(
        hn_p, hn_s, _Weight(w["w_attn_q"], 0), tab_p, tab_s, n=q_w, rope_cols=q_w,
        out_dtypes=(BF16,), rows_per_seq=seq, name="q_proj")
    sinks = w["attn_sinks"][0]
    att_p = _swa_attention(q_p, kv_bf_p, sinks, batch, seq)
    kv4 = kv_f32_p.reshape(batch, seq, 2, A_KV_HEADS, A_HEAD_DIM)
    k_win_p, v_win_p = kv4[:, -WINDOW:, 0], kv4[:, -WINDOW:, 1]
    att_s, k_win_s, v_win_s = _decode_attention(
        q_s, kv_f32_s[:, :kv_w], kv_f32_s[:, kv_w:], k_win0, v_win0, sinks)
    hp, hs = _linear(att_p, att_s, _Weight(w["w_attn_o"], 0), col0=0, n=D_MODEL, out_dtype=F32,
                     residual=hp, residual2=hs, name="attn_out")
    (hp, conv_b_p), (hs, conv_b_s) = _conv_ffn(
        hp, hs, w["g_ffn"][1], _Weight(w["w_ffn_up"], 1), w["ffn_conv_w"][1], w["ffn_conv_b"][1],
        _Weight(w["w_ffn_down"], 1), conv0_p[1], conv0_s[1], batch, seq)
    g_final = w["g_final"].reshape(1, D_MODEL)
    y_p, = _rms_norm(hp, g_final, F32)
    y_s, = _rms_norm(hs, g_final, F32)
    out_p = (y_p.reshape(batch, seq, D_MODEL), c_p[None], n_p[None], m_p[None], k_win_p, v_win_p,
             jnp.stack([conv_a_p, conv_b_p]))
    out_s = (y_s.reshape(batch2, 1, D_MODEL), c_s[None], n_s[None], m_s[None], k_win_s, v_win_s,
             jnp.stack([conv_a_s, conv_b_s]))
    return out_p, out_s


def kernel(x_prompt, x_sample, state_mlstm_C, state_mlstm_n, state_mlstm_m, cache_k_win, cache_v_win,
           state_ffn_conv, g_mix, g_ffn, w_mlstm_in, b_mlstm_gates, w_mlstm_out, g_kv, w_kv, w_attn_q,
           attn_sinks, w_attn_o, w_ffn_up, ffn_conv_w, ffn_conv_b, w_ffn_down, g_final):
    w = dict(g_mix=g_mix, g_ffn=g_ffn, wt_mlstm_in=jnp.swapaxes(w_mlstm_in, 1, 2),
             b_mlstm_gates=b_mlstm_gates,
             w_mlstm_out=w_mlstm_out, g_kv=g_kv, w_kv=w_kv, w_attn_q=w_attn_q, attn_sinks=attn_sinks,
             w_attn_o=w_attn_o, w_ffn_up=w_ffn_up, ffn_conv_w=ffn_conv_w, ffn_conv_b=ffn_conv_b,
             w_ffn_down=w_ffn_down, g_final=g_final)
    bp = x_prompt.shape[0]
    n_a = state_mlstm_C.shape[0]
    depth = state_ffn_conv.shape[0]
    state_p = (jnp.zeros((n_a, bp, M_HEADS, M_DK, M_DV), F32), jnp.zeros((n_a, bp, M_HEADS, M_DK), F32),
               jnp.zeros((n_a, bp, M_HEADS), F32), jnp.zeros((depth, bp, CONV_W - 1, D_FF), F32))
    state_s = (state_mlstm_C, state_mlstm_n, state_mlstm_m, state_ffn_conv)
    prompt, sample = _model(x_prompt, x_sample, state_p, state_s, cache_k_win, cache_v_win, w)
    return (prompt[0], sample[0]) + prompt[1:] + sample[1:]
```

```python
import functools
import math
from typing import NamedTuple

import jax
import jax.numpy as jnp
from jax import lax
from jax.experimental import pallas as pl
from jax.experimental.pallas import tpu as pltpu

F32 = jnp.float32
BF16 = jnp.bfloat16

D_MODEL = 4096
PAST_LEN = 16384
M_HEADS = 8
M_DK = D_MODEL // (2 * M_HEADS)
M_DV = D_MODEL // M_HEADS
M_CHUNK = 128
GATE_SOFT_CAP = 15.0
A_HEADS = 32
A_KV_HEADS = 8
A_GROUP = A_HEADS // A_KV_HEADS
A_HEAD_DIM = D_MODEL // A_HEADS
WINDOW = 128
ROPE_THETA = 10000.0
NEG_INF = -1e30
D_FF = 11008
CONV_W = 3
RMS_EPS = 1e-6

V7X_VMEM_BYTES = 64 * 1024 * 1024
VMEM_LIMIT_BYTES = V7X_VMEM_BYTES - 8 * 1024 * 1024
LANES = 128
SUBLANES = 8

MM_BLOCK_M = 1024
MM_BLOCK_M_F32_OUT = 512
MM_BLOCK_N = 1024
FFN_BLOCK_N = 256
DELAYED_ROW_PIECES = 4
DOWN_BLOCK_M = 512
DOWN_BLOCK_N = 512
RMS_BLOCK_M = 512
DECODE_ATTN_BLOCK_B = 4
MLSTM_HEAD_GROUP = 8
CAST_ROWS = 256


def _params(sem):
    return pltpu.CompilerParams(dimension_semantics=sem, vmem_limit_bytes=VMEM_LIMIT_BYTES)


def _rms_kernel(x_ref, g_ref, *o_refs):
    xf = x_ref[...]
    r = lax.rsqrt(jnp.mean(xf * xf, axis=-1, keepdims=True) + RMS_EPS)
    xr = xf * r
    for k, o_ref in enumerate(o_refs):
        o_ref[...] = (xr * g_ref[k:k + 1, :]).astype(o_ref.dtype)


def _rms_norm(x, gains, out_dtype):
    m, d = x.shape
    n_g = gains.shape[0]
    bm = min(RMS_BLOCK_M, m)
    assert m % bm == 0
    outs = pl.pallas_call(
        _rms_kernel,
        grid=(m // bm,),
        in_specs=[pl.BlockSpec((bm, d), lambda i: (i, 0)),
                  pl.BlockSpec((n_g, d), lambda i: (0, 0))],
        out_specs=[pl.BlockSpec((bm, d), lambda i: (i, 0)) for _ in range(n_g)],
        out_shape=[jax.ShapeDtypeStruct((m, d), out_dtype) for _ in range(n_g)],
        compiler_params=_params(("arbitrary",)),
        name="rms_norm",
    )(x, gains)
    return outs


def _cast_weight(w_ref, wbf_ref, col0, ncols, transposed):
    k = wbf_ref.shape[0]
    rows_per = min(CAST_ROWS, k)
    assert k % rows_per == 0

    def body(r, carry):
        rows = pl.ds(pl.multiple_of(r * rows_per, rows_per), rows_per)
        if transposed:
            wbf_ref[rows, col0:col0 + ncols] = w_ref[:, rows].astype(BF16).T
        else:
            wbf_ref[rows, col0:col0 + ncols] = w_ref[rows, :].astype(BF16)
        return carry

    lax.fori_loop(0, k // rows_per, body, 0)


class _Prefetch(NamedTuple):
    col_blocks: tuple
    layer: int
    bn: int


def _mm_kernel(*refs, n_w, n_w_refs, counts, epilogue, cast, transposed, prefetch, n_i, n_j, delayed):
    n_extra, n_extra2, n_out, n_out2 = counts
    x_ref, x2_ref = refs[0], refs[1]
    refs = refs[2:]
    w_refs, refs = refs[:n_w_refs], refs[n_w_refs:]
    extra, refs = refs[:n_extra], refs[n_extra:]
    extra2, refs = refs[:n_extra2], refs[n_extra2:]
    outs, refs = refs[:n_out], refs[n_out:]
    outs2, scratch = refs[:n_out2], refs[n_out2:]
    bm = x_ref.shape[0]
    if delayed:
        n_tiles = n_j * n_i
        s = pl.program_id(0)
        cur = jnp.minimum(s, n_tiles - 1)
        prev = jnp.maximum(s - 1, 0)
        j, i = cur // n_i, cur % n_i
        j_prev, i_prev = prev // n_i, prev % n_i
    else:
        j = pl.program_id(0)
        i = pl.program_id(1)
    if cast:
        wbf_ref = scratch[0]
        scratch = scratch[1:]
    if prefetch is not None:
        w_hbm = w_refs[0]
        stage_ref, sem_ref = scratch[0], scratch[1]
        scratch = scratch[2:]
        w_refs = [stage_ref.at[t] for t in range(n_w)]

        def slab_copy(t, jj):
            start = pl.multiple_of(prefetch.col_blocks[t](jj) * prefetch.bn, prefetch.bn)
            cols = pl.ds(start, prefetch.bn)
            src = w_hbm.at[prefetch.layer, cols, :] if transposed else w_hbm.at[prefetch.layer, :, cols]
            return pltpu.make_async_copy(src, stage_ref.at[t], sem_ref.at[t])

    if delayed:
        acc_ref, acc2_ref = scratch[0], scratch[1]
        scratch = scratch[2:]
    w_ref = wbf_ref if cast else w_refs[0]

    def load_weights():
        if prefetch is not None:
            @pl.when(j == 0)
            def _():
                for t in range(n_w):
                    slab_copy(t, j).start()

            for t in range(n_w):
                slab_copy(t, j).wait()
        if cast:
            bn = wbf_ref.shape[1] // n_w
            for t, wt_ref in enumerate(w_refs):
                _cast_weight(wt_ref, wbf_ref, t * bn, bn, transposed)
        if prefetch is not None:
            @pl.when(j + 1 < n_j)
            def _():
                for t in range(n_w):
                    slab_copy(t, j + 1).start()

    def tile_dot(with_rows2):
        rows = x_ref[...]
        if with_rows2:
            rows = jnp.concatenate([rows, x2_ref[...]], axis=0)
        acc = jnp.dot(rows, w_ref[...], preferred_element_type=F32)
        return acc[:bm], acc[bm:]

    if not delayed:
        @pl.when(i == 0)
        def _():
            epilogue.init_scratch(scratch)
            load_weights()
            acc, acc2 = tile_dot(True)
            epilogue.tile(acc, i, j, extra, outs, scratch)
            epilogue.rows(acc2, j, extra, extra2, outs2)

        @pl.when(i != 0)
        def _():
            acc, _ = tile_dot(False)
            epilogue.tile(acc, i, j, extra, outs, scratch)

        return

    cm = bm // DELAYED_ROW_PIECES

    def step(multiply, rows2_now, rows2_prev):
        if rows2_prev:
            epilogue.rows(acc2_ref[...], j_prev, extra, extra2, outs2)
        for p in range(DELAYED_ROW_PIECES):
            rows = slice(p * cm, (p + 1) * cm)
            epilogue.piece(p, DELAYED_ROW_PIECES, acc_ref[rows, :], i_prev, j_prev, extra, outs, scratch)
            if multiply:
                lhs = x_ref[rows, :]
                last = p == DELAYED_ROW_PIECES - 1
                if rows2_now and last:
                    lhs = jnp.concatenate([lhs, x2_ref[...]], axis=0)
                acc = jnp.dot(lhs, w_ref[...], preferred_element_type=F32)
                acc_ref[rows, :] = acc[:cm]
                if rows2_now and last:
                    acc2_ref[...] = acc[cm:]

    @pl.when(s == 0)
    def _():
        acc_ref[...] = jnp.zeros_like(acc_ref)
        epilogue.init_scratch(scratch)

    @pl.when((s < n_tiles) & (i == 0))
    def _():
        load_weights()
        step(True, True, False)

    @pl.when((s < n_tiles) & (i == 1))
    def _():
        step(True, False, True)

    @pl.when((s < n_tiles) & (i > 1))
    def _():
        step(True, False, False)

    @pl.when(s == n_tiles)
    def _():
        step(False, False, False)


class _Epilogue:
    def scratch_shapes(self, bm, bn):
        return []

    def init_scratch(self, scratch):
        pass

    def tile(self, acc, i, j, extra, outs, scratch):
        self.piece(0, 1, acc, i, j, extra, outs, scratch)

    def piece(self, p, n_pieces, acc, i, j, extra, outs, scratch):
        raise NotImplementedError


class _Weight(NamedTuple):
    arr: jax.Array
    layer: int = 0
    transposed: bool = False

    @property
    def k(self):
        return self.arr.shape[2 if self.transposed else 1]


def _matmul(x, x2, w, epilogue, *, w_col_blocks, bn, bm=MM_BLOCK_M, extra=(), extra_specs=(),
            extra2=(), extra2_specs=(), out_shapes, out_specs, out2_shapes, out2_specs, name,
            prefetch=False, delayed=False):
    m, k = x.shape
    m2 = x2.shape[0]
    bm = min(bm, m)
    assert m % bm == 0 and w.k == k and x2.shape[1] == k
    cast = w.arr.dtype != BF16
    assert cast or not w.transposed
    n_w = len(w_col_blocks)
    n_j = epilogue.n_col_steps
    n_i = m // bm
    delayed = delayed and n_i >= 2
    x_specs = [pl.BlockSpec((bm, k), lambda j, i: (i, 0)),
               pl.BlockSpec((m2, k), lambda j, i: (0, 0))]
    slab_shape = (bn, k) if w.transposed else (k, bn)
    if prefetch:
        assert cast
        w_specs = [pl.BlockSpec(memory_space=pl.ANY)]
        w_args = [w.arr]
    else:
        w_specs = []
        for f in w_col_blocks:
            if w.transposed:
                index_map = functools.partial(lambda j, i, f: (w.layer, f(j), 0), f=f)
            else:
                index_map = functools.partial(lambda j, i, f: (w.layer, 0, f(j)), f=f)
            w_specs.append(pl.BlockSpec((None,) + slab_shape, index_map))
        w_args = [w.arr] * n_w
    late_specs = list(extra_specs) + list(extra2_specs)
    all_out_specs = list(out_specs) + list(out2_specs)
    if delayed:
        n_tiles = n_j * n_i

        def at_step(spec, lag):
            if spec.block_shape is None:
                return spec

            def index_map(s):
                t = jnp.clip(s - lag, 0, n_tiles - 1)
                return spec.index_map(t // n_i, t % n_i)
            return pl.BlockSpec(spec.block_shape, index_map)

        x_specs = [at_step(sp, 0) for sp in x_specs]
        w_specs = [at_step(sp, 0) for sp in w_specs]
        late_specs = [at_step(sp, 1) for sp in late_specs]
        all_out_specs = [at_step(sp, 1) for sp in all_out_specs]
        grid, semantics = (n_tiles + 1,), ("arbitrary",)
    else:
        grid, semantics = (n_j, n_i), ("arbitrary", "arbitrary")
    scratch = []
    if cast:
        scratch.append(pltpu.VMEM((k, n_w * bn), BF16))
    if prefetch:
        scratch.append(pltpu.VMEM((n_w,) + slab_shape, F32))
        scratch.append(pltpu.SemaphoreType.DMA((n_w,)))
    if delayed:
        scratch.append(pltpu.VMEM((bm, n_w * bn), F32))
        scratch.append(pltpu.VMEM((m2, n_w * bn), F32))
    scratch.extend(epilogue.scratch_shapes(bm, bn))
    counts = (len(extra), len(extra2), len(out_shapes), len(out2_shapes))
    kern = functools.partial(
        _mm_kernel, n_w=n_w, n_w_refs=len(w_args), counts=counts, epilogue=epilogue, cast=cast,
        transposed=w.transposed,
        prefetch=_Prefetch(tuple(w_col_blocks), w.layer, bn) if prefetch else None,
        n_i=n_i, n_j=n_j, delayed=delayed)
    res = pl.pallas_call(
        kern,
        grid=grid,
        in_specs=x_specs + w_specs + late_specs,
        out_specs=all_out_specs,
        out_shape=list(out_shapes) + list(out2_shapes),
        scratch_shapes=scratch,
        compiler_params=_params(semantics),
        name=name,
    )(x, x2, *w_args, *extra, *extra2)
    return res[:len(out_shapes)], res[len(out_shapes):]


class _StoreEpilogue(_Epilogue):
    def __init__(self, n_col_steps, scaled_col_steps=0, scale=1.0, residual=False):
        self.n_col_steps = n_col_steps
        self.scaled_col_steps = scaled_col_steps
        self.scale = scale
        self.residual = residual

    def _store(self, acc, j, res_refs, out_ref):
        if self.scaled_col_steps:
            acc = acc * jnp.where(j < self.scaled_col_steps, self.scale, 1.0).astype(F32)
        if self.residual:
            acc = res_refs[0][...] + acc
        out_ref[...] = acc.astype(out_ref.dtype)

    def tile(self, acc, i, j, extra, outs, scratch):
        self._store(acc, j, extra, outs[0])

    def rows(self, acc, j, extra, extra2, outs2):
        self._store(acc, j, extra2, outs2[0])


def _linear(x, x2, w, *, col0, n, out_dtype, bn=MM_BLOCK_N, scaled_cols=0, scale=1.0,
            residual=None, residual2=None, bm=None, name):
    m, m2 = x.shape[0], x2.shape[0]
    if bm is None:
        bm = MM_BLOCK_M if out_dtype == BF16 else MM_BLOCK_M_F32_OUT
    bm = min(bm, m)
    assert n % bn == 0 and col0 % bn == 0 and scaled_cols % bn == 0
    assert (residual is None) == (residual2 is None)
    j0 = col0 // bn
    ep = _StoreEpilogue(n // bn, scaled_cols // bn, scale, residual is not None)
    tile_spec = pl.BlockSpec((bm, bn), lambda j, i: (i, j))
    rows_spec = pl.BlockSpec((m2, bn), lambda j, i: (0, j))
    has_res = residual is not None
    (out,), (out2,) = _matmul(
        x, x2, w, ep, w_col_blocks=[lambda j: j + j0], bn=bn, bm=bm,
        extra=(residual,) if has_res else (), extra_specs=(tile_spec,) if has_res else (),
        extra2=(residual2,) if has_res else (), extra2_specs=(rows_spec,) if has_res else (),
        out_shapes=[jax.ShapeDtypeStruct((m, n), out_dtype)], out_specs=[tile_spec],
        out2_shapes=[jax.ShapeDtypeStruct((m2, n), out_dtype)], out2_specs=[rows_spec],
        name=name, prefetch=w.arr.dtype != BF16)
    return out, out2


class _RopeEpilogue(_Epilogue):
    def __init__(self, n_col_steps, rope_col_steps, n_out):
        self.n_col_steps = n_col_steps
        self.rope_col_steps = rope_col_steps
        self.n_out = n_out

    def _store(self, acc, j, cos, sin, outs):
        def rotated():
            heads = []
            for h in range(acc.shape[1] // A_HEAD_DIM):
                xh = acc[:, h * A_HEAD_DIM:(h + 1) * A_HEAD_DIM]
                heads.append(xh * cos + pltpu.roll(xh, A_HEAD_DIM // 2, axis=1) * sin)
            return jnp.concatenate(heads, axis=1)

        if self.rope_col_steps >= self.n_col_steps:
            val = rotated()
        else:
            val = jnp.where(j < self.rope_col_steps, rotated(), acc)
        for o_ref in outs:
            o_ref[...] = val.astype(o_ref.dtype)

    def tile(self, acc, i, j, extra, outs, scratch):
        cos_ref, sin_ref = extra
        bm = acc.shape[0]
        rows = pl.ds(pl.multiple_of((i % (cos_ref.shape[0] // bm)) * bm, bm), bm)
        self._store(acc, j, cos_ref[rows, :], sin_ref[rows, :], outs)

    def rows(self, acc, j, extra, extra2, outs2):
        self._store(acc, j, extra2[0][...], extra2[1][...], outs2)


def _rope_linear(x, x2, w, tables, tables2, *, n, rope_cols, out_dtypes, rows_per_seq, name,
                 bn=MM_BLOCK_N):
    m, m2 = x.shape[0], x2.shape[0]
    bm = min(MM_BLOCK_M_F32_OUT, m, rows_per_seq)
    assert rows_per_seq % bm == 0 and n % bn == 0 and rope_cols % bn == 0
    tiles_per_seq = rows_per_seq // bm
    ep = _RopeEpilogue(n // bn, rope_cols // bn, len(out_dtypes))
    tab_spec = pl.BlockSpec((rows_per_seq, A_HEAD_DIM), lambda j, i: (0, 0))
    tab2_spec = pl.BlockSpec((m2, A_HEAD_DIM), lambda j, i: (0, 0))
    return _matmul(
        x, x2, w, ep, w_col_blocks=[lambda j: j], bn=bn, bm=bm,
        extra=tables, extra_specs=(tab_spec, tab_spec),
        extra2=tables2, extra2_specs=(tab2_spec, tab2_spec),
        out_shapes=[jax.ShapeDtypeStruct((m, n), dt) for dt in out_dtypes],
        out_specs=[pl.BlockSpec((bm, bn), lambda j, i: (i, j)) for _ in out_dtypes],
        out2_shapes=[jax.ShapeDtypeStruct((m2, n), dt) for dt in out_dtypes],
        out2_specs=[pl.BlockSpec((m2, bn), lambda j, i: (0, j)) for _ in out_dtypes],
        name=name, prefetch=True)


def _gates_kernel(x_ref, wt_ref, b_ref, o_ref):
    g = lax.dot_general(x_ref[...], wt_ref[...].astype(BF16), (((1,), (1,)), ((), ())),
                        preferred_element_type=F32) + b_ref[...]
    ig = GATE_SOFT_CAP * jnp.tanh(g / GATE_SOFT_CAP)
    lf = jnp.minimum(g, 0.0) - jnp.log1p(jnp.exp(-jnp.abs(g)))
    lane = lax.broadcasted_iota(jnp.int32, g.shape, 1)
    o_ref[...] = jnp.where(lane < M_HEADS, ig, lf)


def _mlstm_gates(xn, wt_gates, b_gates):
    m, d = xn.shape
    bm = min(MM_BLOCK_M, m)
    ng = 2 * M_HEADS
    return pl.pallas_call(
        _gates_kernel,
        grid=(m // bm,),
        in_specs=[pl.BlockSpec((bm, d), lambda i: (i, 0)),
                  pl.BlockSpec((ng, d), lambda i: (0, 0)),
                  pl.BlockSpec((1, ng), lambda i: (0, 0))],
        out_specs=pl.BlockSpec((bm, ng), lambda i: (i, 0)),
        out_shape=jax.ShapeDtypeStruct((m, ng), F32),
        compiler_params=_params(("arbitrary",)),
        name="mlstm_gates",
    )(xn, wt_gates, b_gates.reshape(1, ng).astype(F32))


def _mlstm_chunk_kernel(q_ref, k_ref, v_ref, o_ref, gcol_ref, grow_ref, c0_ref, n0_ref, m0_ref,
                        h_ref, c_ref, n_ref, m_ref):
    L = q_ref.shape[0]

    @pl.when(pl.program_id(1) == 0)
    def _():
        c_ref[...] = c0_ref[...]
        n_ref[...] = n0_ref[...]
        m_ref[...] = m0_ref[...]

    t_idx = lax.broadcasted_iota(jnp.int32, (L, L), 0)
    s_idx = lax.broadcasted_iota(jnp.int32, (L, L), 1)
    causal = s_idx <= t_idx
    qs = [q_ref[:, h * M_DK:(h + 1) * M_DK] for h in range(M_HEADS)]
    ks = [k_ref[:, h * M_DK:(h + 1) * M_DK] for h in range(M_HEADS)]
    vs = [v_ref[:, h * M_DV:(h + 1) * M_DV] for h in range(M_HEADS)]

    for g0 in range(0, M_HEADS, MLSTM_HEAD_GROUP):
        heads = range(g0, g0 + MLSTM_HEAD_GROUP)

        gw = {}
        for h in heads:
            ig_col = gcol_ref[:, h:h + 1]
            lf_col = gcol_ref[:, M_HEADS + h:M_HEADS + h + 1]
            ig_row = grow_ref[h:h + 1, :]
            lf_row = grow_ref[M_HEADS + h:M_HEADS + h + 1, :]
            m_prev = m_ref[0, h:h + 1, 0:1]
            b_col = jnp.sum(jnp.where(causal, lf_row, 0.0), axis=1, keepdims=True)
            b_row = jnp.sum(jnp.where(t_idx <= s_idx, lf_col, 0.0), axis=0, keepdims=True)
            dlog = jnp.where(causal, b_col - b_row + ig_row, -jnp.inf)
            g = b_col + m_prev
            m_t = jnp.maximum(g, jnp.max(dlog, axis=1, keepdims=True))
            w_intra = jnp.exp(dlog - m_t)
            w_inter = jnp.exp(g - m_t)
            b_last = b_col[L - 1:L, :]
            m_last = m_t[L - 1:L, :]
            w_end = jnp.exp(b_last - b_col + ig_col - m_last)
            gw[h] = (w_intra, w_inter, m_t, w_end)

        for h in heads:
            w_intra, w_inter, m_t, _ = gw[h]
            q, k, v = qs[h], ks[h], vs[h]
            c_prev = c_ref[0, h]
            n_prev = n_ref[0, h:h + 1, :]
            qk = lax.dot_general(q, k, (((1,), (1,)), ((), ())), preferred_element_type=F32)
            s = qk * w_intra
            num = (jnp.dot(s.astype(BF16), v, preferred_element_type=F32)
                   + w_inter * jnp.dot(q, c_prev.astype(BF16), preferred_element_type=F32))
            qn = jnp.sum(q.astype(F32) * n_prev, axis=1, keepdims=True)
            den = jnp.sum(s, axis=1, keepdims=True) + w_inter * qn
            inv = 1.0 / jnp.maximum(jnp.abs(den), jnp.exp(-m_t))
            gate = jax.nn.sigmoid(o_ref[:, h * M_DV:(h + 1) * M_DV])
            h_ref[:, h * M_DV:(h + 1) * M_DV] = (gate * (num * inv)).astype(h_ref.dtype)

        for h in heads:
            _, w_inter, m_t, w_end = gw[h]
            decay = w_inter[L - 1:L, :]
            kw = ks[h].astype(F32) * w_end
            c_ref[0, h] = decay * c_ref[0, h] + lax.dot_general(
                kw.astype(BF16), vs[h], (((0,), (0,)), ((), ())), preferred_element_type=F32)
            n_ref[0, h:h + 1, :] = decay * n_ref[0, h:h + 1, :] + jnp.sum(kw, axis=0, keepdims=True)
            m_ref[0, h:h + 1, :] = jnp.broadcast_to(m_t[L - 1:L, :], (1, LANES))


def _mlstm_chunks(qkv, o_pre, gates, c0, n0, m0, batch, seq):
    L = M_CHUNK
    assert seq % L == 0
    nc = seq // L
    qk_w = M_HEADS * M_DK
    v_w = M_HEADS * M_DV
    m_rows = batch * seq
    grow = gates.T
    m0b = jnp.broadcast_to(m0[:, :, None], (batch, M_HEADS, LANES))
    row = lambda b, c: b * nc + c
    hg, c_new, n_new, m_new = pl.pallas_call(
        _mlstm_chunk_kernel,
        grid=(batch, nc),
        in_specs=[
            pl.BlockSpec((L, qk_w), lambda b, c: (row(b, c), 0)),
            pl.BlockSpec((L, qk_w), lambda b, c: (row(b, c), 1)),
            pl.BlockSpec((L, v_w), lambda b, c: (row(b, c), 1)),
            pl.BlockSpec((L, v_w), lambda b, c: (row(b, c), 0)),
            pl.BlockSpec((L, 2 * M_HEADS), lambda b, c: (row(b, c), 0)),
            pl.BlockSpec((2 * M_HEADS, L), lambda b, c: (0, row(b, c))),
            pl.BlockSpec((1, M_HEADS, M_DK, M_DV), lambda b, c: (b, 0, 0, 0)),
            pl.BlockSpec((1, M_HEADS, M_DK), lambda b, c: (b, 0, 0)),
            pl.BlockSpec((1, M_HEADS, LANES), lambda b, c: (b, 0, 0)),
        ],
        out_specs=[
            pl.BlockSpec((L, v_w), lambda b, c: (row(b, c), 0)),
            pl.BlockSpec((1, M_HEADS, M_DK, M_DV), lambda b, c: (b, 0, 0, 0)),
            pl.BlockSpec((1, M_HEADS, M_DK), lambda b, c: (b, 0, 0)),
            pl.BlockSpec((1, M_HEADS, LANES), lambda b, c: (b, 0, 0)),
        ],
        out_shape=[
            jax.ShapeDtypeStruct((m_rows, v_w), BF16),
            jax.ShapeDtypeStruct((batch, M_HEADS, M_DK, M_DV), F32),
            jax.ShapeDtypeStruct((batch, M_HEADS, M_DK), F32),
            jax.ShapeDtypeStruct((batch, M_HEADS, LANES), F32),
        ],
        compiler_params=_params(("arbitrary", "arbitrary")),
        name="mlstm_chunks",
    )(qkv, qkv, qkv, o_pre, gates, grow, c0, n0, m0b)
    return hg, c_new, n_new, m_new[:, :, 0]


def _mlstm_step_kernel(q_ref, k_ref, v_ref, o_ref, g_ref, c0_ref, n0_ref, m0_ref,
                       h_ref, c_ref, n_ref, m_ref):
    dk_idx = lax.broadcasted_iota(jnp.int32, (M_DK, M_DK), 0)
    dk_lane = lax.broadcasted_iota(jnp.int32, (M_DK, M_DK), 1)
    eye = dk_idx == dk_lane
    m_rows = []
    for h in range(M_HEADS):
        q = q_ref[0, :, h * M_DK:(h + 1) * M_DK]
        k = k_ref[0, :, h * M_DK:(h + 1) * M_DK]
        v = v_ref[0, :, h * M_DV:(h + 1) * M_DV]
        ig = g_ref[0, :, h:h + 1]
        lf = g_ref[0, :, M_HEADS + h:M_HEADS + h + 1]
        c_prev = c0_ref[0, h]
        n_prev = n0_ref[0, h:h + 1, :]
        m_prev = m0_ref[0, h:h + 1, 0:1]
        qf, kf, vf = q.astype(F32), k.astype(F32), v.astype(F32)

        g = lf + m_prev
        m_t = jnp.maximum(g, ig)
        w_intra = jnp.exp(ig - m_t)
        w_inter = jnp.exp(g - m_t)
        s = jnp.sum(qf * kf, axis=1, keepdims=True) * w_intra
        q_rows = jnp.broadcast_to(q, (2 * SUBLANES, M_DK))
        qc = jnp.dot(q_rows, c_prev.astype(BF16), preferred_element_type=F32)[0:1, :]
        num = s * vf + w_inter * qc
        den = s + w_inter * jnp.sum(qf * n_prev, axis=1, keepdims=True)
        hh = num / jnp.maximum(jnp.abs(den), jnp.exp(-m_t))
        gate = jax.nn.sigmoid(o_ref[0, :, h * M_DV:(h + 1) * M_DV])
        h_ref[0, :, h * M_DV:(h + 1) * M_DV] = (gate * hh).astype(h_ref.dtype)

        kw = kf * w_intra
        kw_col = jnp.sum(jnp.where(eye, kw, 0.0), axis=1, keepdims=True)
        c_ref[0, h] = w_inter * c_prev + kw_col * vf
        n_ref[0, h:h + 1, :] = w_inter * n_prev + kw
        m_rows.append(jnp.broadcast_to(m_t, (1, LANES)))
    m_ref[0] = jnp.concatenate(m_rows, axis=0)


def _mlstm_step(qkv, o_pre, gates, c0, n0, m0):
    batch = qkv.shape[0]
    qk_w = M_HEADS * M_DK
    v_w = M_HEADS * M_DV
    qkv3 = qkv.reshape(batch, 1, 2 * qk_w + v_w)
    m0b = jnp.broadcast_to(m0[:, :, None], (batch, M_HEADS, LANES))
    hg, c_new, n_new, m_new = pl.pallas_call(
        _mlstm_step_kernel,
        grid=(batch,),
        in_specs=[
            pl.BlockSpec((1, 1, qk_w), lambda b: (b, 0, 0)),
            pl.BlockSpec((1, 1, qk_w), lambda b: (b, 0, 1)),
            pl.BlockSpec((1, 1, v_w), lambda b: (b, 0, 1)),
            pl.BlockSpec((1, 1, v_w), lambda b: (b, 0, 0)),
            pl.BlockSpec((1, 1, 2 * M_HEADS), lambda b: (b, 0, 0)),
            pl.BlockSpec((1, M_HEADS, M_DK, M_DV), lambda b: (b, 0, 0, 0)),
            pl.BlockSpec((1, M_HEADS, M_DK), lambda b: (b, 0, 0)),
            pl.BlockSpec((1, M_HEADS, LANES), lambda b: (b, 0, 0)),
        ],
        out_specs=[
            pl.BlockSpec((1, 1, v_w), lambda b: (b, 0, 0)),
            pl.BlockSpec((1, M_HEADS, M_DK, M_DV), lambda b: (b, 0, 0, 0)),
            pl.BlockSpec((1, M_HEADS, M_DK), lambda b: (b, 0, 0)),
            pl.BlockSpec((1, M_HEADS, LANES), lambda b: (b, 0, 0)),
        ],
        out_shape=[
            jax.ShapeDtypeStruct((batch, 1, v_w), BF16),
            jax.ShapeDtypeStruct((batch, M_HEADS, M_DK, M_DV), F32),
            jax.ShapeDtypeStruct((batch, M_HEADS, M_DK), F32),
            jax.ShapeDtypeStruct((batch, M_HEADS, LANES), F32),
        ],
        compiler_params=_params(("arbitrary",)),
        name="mlstm_step",
    )(qkv3, qkv3, qkv3, o_pre.reshape(batch, 1, v_w), gates.reshape(batch, 1, 2 * M_HEADS),
      c0, n0, m0b)
    return hg.reshape(batch, v_w), c_new, n_new, m_new[:, :, 0]


def _softmax_with_sink(scores, sink):
    mx = jnp.maximum(jnp.max(scores, axis=-1, keepdims=True), sink)
    e = jnp.exp(scores - mx)
    return e / (jnp.sum(e, axis=-1, keepdims=True) + jnp.exp(sink - mx))


def _swa_kernel(sink_ref, q_ref, kp_ref, ko_ref, vp_ref, vo_ref, o_ref):
    W = WINDOW
    D = A_HEAD_DIM
    blk = pl.program_id(1)
    i_idx = lax.broadcasted_iota(jnp.int32, (W, 2 * W), 0)
    j_idx = lax.broadcasted_iota(jnp.int32, (W, 2 * W), 1)
    first_key = jnp.where(blk > 0, 0, W)
    mask = (j_idx >= jnp.maximum(i_idx, first_key)) & (j_idx <= i_idx + W)
    for kv in range(A_KV_HEADS):
        cols = slice(kv * D, (kv + 1) * D)
        k_cat = jnp.concatenate([kp_ref[:, cols], ko_ref[:, cols]], axis=0)
        v_cat = jnp.concatenate([vp_ref[:, cols], vo_ref[:, cols]], axis=0)
        q = jnp.concatenate([q_ref[:, (kv * A_GROUP + g) * D:(kv * A_GROUP + g + 1) * D]
                             for g in range(A_GROUP)], axis=0)
        scores = lax.dot_general(q, k_cat, (((1,), (1,)), ((), ())), preferred_element_type=F32)
        scores = scores * (A_HEAD_DIM ** -0.5)
        probs = []
        for g in range(A_GROUP):
            sc = jnp.where(mask, scores[g * W:(g + 1) * W, :], NEG_INF)
            probs.append(_softmax_with_sink(sc, sink_ref[kv * A_GROUP + g]).astype(BF16))
        p = jnp.concatenate(probs, axis=0)
        o = jnp.dot(p, v_cat, preferred_element_type=F32)
        for g in range(A_GROUP):
            hq = kv * A_GROUP + g
            o_ref[:, hq * D:(hq + 1) * D] = o[g * W:(g + 1) * W, :].astype(o_ref.dtype)


def _swa_attention(q, kv, sinks, batch, seq):
    W = WINDOW
    assert seq % W == 0
    nb = seq // W
    kvw = A_KV_HEADS * A_HEAD_DIM
    qw = A_HEADS * A_HEAD_DIM
    row = lambda b, n: b * nb + n
    prev = lambda b, n: jnp.maximum(b * nb + n - 1, 0)
    return pl.pallas_call(
        _swa_kernel,
        grid=(batch, nb),
        in_specs=[
            pl.BlockSpec(memory_space=pltpu.SMEM),
            pl.BlockSpec((W, qw), lambda b, n: (row(b, n), 0)),
            pl.BlockSpec((W, kvw), lambda b, n: (prev(b, n), 0)),
            pl.BlockSpec((W, kvw), lambda b, n: (row(b, n), 0)),
            pl.BlockSpec((W, kvw), lambda b, n: (prev(b, n), 1)),
            pl.BlockSpec((W, kvw), lambda b, n: (row(b, n), 1)),
        ],
        out_specs=pl.BlockSpec((W, qw), lambda b, n: (row(b, n), 0)),
        out_shape=jax.ShapeDtypeStruct((batch * seq, qw), BF16),
        compiler_params=_params(("arbitrary", "arbitrary")),
        name="swa_attention",
    )(sinks.astype(F32), q, kv, kv, kv, kv)


def _decode_attn_kernel(sink_ref, q_ref, kn_ref, vn_ref, ck_ref, cv_ref, o_ref, kw_ref, vw_ref):
    W, D, KV, H = WINDOW, A_HEAD_DIM, A_KV_HEADS, A_HEADS
    scale = A_HEAD_DIM ** -0.5
    head_kv = lax.broadcasted_iota(jnp.int32, (H, W * KV), 0) // A_GROUP
    own_c = (lax.broadcasted_iota(jnp.int32, (H, W * KV), 1) % KV) == head_kv
    own_n = lax.broadcasted_iota(jnp.int32, (H, KV), 1) == head_kv[:, :KV]
    sink = sink_ref[:, 0:1]
    nt = (((1,), (1,)), ((), ()))
    for b in range(q_ref.shape[0]):
        q = q_ref[b]
        kc = ck_ref[b].reshape(W * KV, D).astype(BF16)
        vc = cv_ref[b].reshape(W * KV, D).astype(BF16)
        kn = kn_ref[b].astype(BF16)
        vn = vn_ref[b].astype(BF16)
        s_c = lax.dot_general(q, kc, nt, preferred_element_type=F32) * scale
        s_n = lax.dot_general(q, kn, nt, preferred_element_type=F32) * scale
        s_c = jnp.where(own_c, s_c, NEG_INF)
        s_n = jnp.where(own_n, s_n, NEG_INF)
        mx = jnp.maximum(jnp.maximum(jnp.max(s_c, axis=1, keepdims=True),
                                     jnp.max(s_n, axis=1, keepdims=True)), sink)
        e_c = jnp.exp(s_c - mx)
        e_n = jnp.exp(s_n - mx)
        den = (jnp.sum(e_c, axis=1, keepdims=True) + jnp.sum(e_n, axis=1, keepdims=True)
               + jnp.exp(sink - mx))
        o = (jnp.dot((e_c / den).astype(BF16), vc, preferred_element_type=F32)
             + jnp.dot((e_n / den).astype(BF16), vn, preferred_element_type=F32))
        o_ref[b] = o.astype(o_ref.dtype)
        kw_ref[b, 0:W - 1] = ck_ref[b, 1:W]
        kw_ref[b, W - 1] = kn_ref[b]
        vw_ref[b, 0:W - 1] = cv_ref[b, 1:W]
        vw_ref[b, W - 1] = vn_ref[b]


def _decode_attention(q, k_new, v_new, cache_k, cache_v, sinks):
    batch = q.shape[0]
    bb = math.gcd(batch, DECODE_ATTN_BLOCK_B)
    H, KV, D, W = A_HEADS, A_KV_HEADS, A_HEAD_DIM, WINDOW
    new_spec = pl.BlockSpec((bb, KV, D), lambda b: (b, 0, 0))
    win_spec = pl.BlockSpec((bb, W, KV, D), lambda b: (b, 0, 0, 0))
    o, k_win, v_win = pl.pallas_call(
        _decode_attn_kernel,
        grid=(batch // bb,),
        in_specs=[pl.BlockSpec((H, LANES), lambda b: (0, 0)),
                  pl.BlockSpec((bb, H, D), lambda b: (b, 0, 0)),
                  new_spec, new_spec, win_spec, win_spec],
        out_specs=[pl.BlockSpec((bb, H, D), lambda b: (b, 0, 0)), win_spec, win_spec],
        out_shape=[jax.ShapeDtypeStruct((batch, H, D), BF16),
                   jax.ShapeDtypeStruct((batch, W, KV, D), F32),
                   jax.ShapeDtypeStruct((batch, W, KV, D), F32)],
        compiler_params=_params(("arbitrary",)),
        name="decode_attention",
    )(jnp.broadcast_to(sinks.astype(F32)[:, None], (H, LANES)), q.reshape(batch, H, D),
      k_new.reshape(batch, KV, D), v_new.reshape(batch, KV, D), cache_k, cache_v)
    return o.reshape(batch, H * D), k_win, v_win


def _conv_act(g2, g1, g0, val, cw_ref, cb_ref):
    conv = cb_ref[...] + cw_ref[0:1, :] * g2
    conv = conv + cw_ref[1:2, :] * g1
    conv = conv + cw_ref[2:3, :] * g0
    return conv * jax.nn.sigmoid(conv) * val


class _ConvFfnEpilogue(_Epilogue):
    def __init__(self, n_col_steps, tiles_per_seq):
        self.n_col_steps = n_col_steps
        self.tiles_per_seq = tiles_per_seq

    def scratch_shapes(self, bm, bn):
        return [pltpu.VMEM((SUBLANES, bn), F32)]

    def init_scratch(self, scratch):
        scratch[0][...] = jnp.zeros_like(scratch[0])

    def rows(self, acc, j, extra, extra2, outs2):
        cw_ref, cb_ref = extra[:2]
        g2_ref, g1_ref, wd_ref = extra2
        act_ref, gate_ref, wd_bf_ref = outs2
        wd_bf_ref[...] = wd_ref[...].astype(wd_bf_ref.dtype)
        bn = acc.shape[1] // 2
        gate, val = acc[:, :bn], acc[:, bn:]
        act_ref[...] = _conv_act(g2_ref[...], g1_ref[...], gate, val, cw_ref, cb_ref).astype(act_ref.dtype)
        gate_ref[...] = gate

    def piece(self, p, n_pieces, acc, i, j, extra, outs, scratch):
        cw_ref, cb_ref, halo0_ref = extra
        act_ref, tail_ref = outs
        carry_ref, = scratch
        rows = acc.shape[0]
        bn = acc.shape[1] // 2
        gate, val = acc[:, :bn], acc[:, bn:]
        halo = carry_ref[...]
        if p == 0:
            seq_start = (i % self.tiles_per_seq) == 0
            halo = jnp.where(seq_start, halo0_ref[i // self.tiles_per_seq], halo)
        row = lax.broadcasted_iota(jnp.int32, (SUBLANES, bn), 0)

        def shifted(d):
            body = pltpu.roll(gate, d, axis=0)
            top = jnp.where(row < d, pltpu.roll(halo, d, axis=0), body[:SUBLANES])
            return jnp.concatenate([top, body[SUBLANES:]], axis=0)

        g1, g2 = shifted(1), shifted(2)
        act = _conv_act(g2, g1, gate, val, cw_ref, cb_ref).astype(act_ref.dtype)
        act_ref[p * rows:(p + 1) * rows, :] = act
        tail = gate[rows - SUBLANES:, :]
        carry_ref[...] = tail
        if p == n_pieces - 1:
            tail_ref[i] = tail


def _conv_ffn_up(xn, xn2, w_up, w_down, conv_w, conv_b, conv0, conv0_2, batch, seq):
    m, m2 = xn.shape[0], xn2.shape[0]
    bn = FFN_BLOCK_N
    assert D_FF % bn == 0
    n_j = D_FF // bn
    cw = conv_w.astype(F32)
    cb = conv_b.reshape(1, D_FF).astype(F32)
    col_spec = lambda rows: pl.BlockSpec((rows, bn), lambda j, i: (0, j))
    bm = min(MM_BLOCK_M, seq)
    assert seq % bm == 0 and bm % SUBLANES == 0
    tiles_per_seq = seq // bm
    n_i = m // bm
    halo0 = jnp.concatenate(
        [jnp.zeros((batch, SUBLANES - (CONV_W - 1), D_FF), F32), conv0.astype(F32)], axis=1)
    g2, g1 = conv0_2[:, 0, :], conv0_2[:, 1, :]
    rows_spec = pl.BlockSpec((m2, bn), lambda j, i: (0, j))
    k_down, n_down = w_down.arr.shape[1:]
    wd_rows = k_down // n_j
    assert wd_rows * n_j == k_down and wd_rows % (2 * SUBLANES) == 0 and not w_down.transposed
    (act, tails), (act2, gate2, wd_bf16) = _matmul(
        xn, xn2, w_up, _ConvFfnEpilogue(n_j, tiles_per_seq),
        w_col_blocks=[lambda j: j, lambda j: j + n_j], bn=bn, bm=bm,
        extra=(cw, cb, halo0),
        extra_specs=(col_spec(CONV_W), col_spec(1),
                     pl.BlockSpec((batch, SUBLANES, bn), lambda j, i: (0, 0, j))),
        extra2=(g2, g1, w_down.arr),
        extra2_specs=(rows_spec, rows_spec,
                      pl.BlockSpec((None, wd_rows, n_down), lambda j, i: (w_down.layer, j, 0))),
        out_shapes=[jax.ShapeDtypeStruct((m, D_FF), BF16),
                    jax.ShapeDtypeStruct((n_i, SUBLANES, D_FF), F32)],
        out_specs=[pl.BlockSpec((bm, bn), lambda j, i: (i, j)),
                   pl.BlockSpec((n_i, SUBLANES, bn), lambda j, i: (0, 0, j))],
        out2_shapes=[jax.ShapeDtypeStruct((m2, D_FF), BF16), jax.ShapeDtypeStruct((m2, D_FF), F32),
                     jax.ShapeDtypeStruct((k_down, n_down), BF16)],
        out2_specs=[rows_spec, rows_spec, pl.BlockSpec((wd_rows, n_down), lambda j, i: (j, 0))],
        name="conv_ffn_up", prefetch=True, delayed=True)
    last_tiles = tails.reshape(batch, tiles_per_seq, SUBLANES, D_FF)[:, -1]
    return ((act, last_tiles[:, SUBLANES - (CONV_W - 1):, :]),
            (act2, jnp.stack([g1, gate2], axis=1)), wd_bf16)


def _conv_ffn(h, h2, gain, w_up, conv_w, conv_b, w_down, conv0, conv0_2, batch, seq):
    gain = gain.reshape(1, D_MODEL)
    xn, = _rms_norm(h, gain, BF16)
    xn2, = _rms_norm(h2, gain, BF16)
    (act, conv_new), (act2, conv_new2), wd_bf16 = _conv_ffn_up(
        xn, xn2, w_up, w_down, conv_w, conv_b, conv0, conv0_2, batch, seq)
    out, out2 = _linear(act, act2, _Weight(wd_bf16[None], 0), col0=0, n=D_MODEL, out_dtype=F32,
                        bn=DOWN_BLOCK_N, bm=DOWN_BLOCK_M, residual=h, residual2=h2, name="ffn_down")
    return (out, conv_new), (out2, conv_new2)


def _rope_tables(pos):
    inv_freq = ROPE_THETA ** (-jnp.arange(0, A_HEAD_DIM, 2, dtype=F32) / A_HEAD_DIM)
    ang = pos.astype(F32)[:, None] * inv_freq[None, :]
    cos, sin = jnp.cos(ang), jnp.sin(ang)
    return jnp.concatenate([cos, cos], axis=1), jnp.concatenate([-sin, sin], axis=1)


def _model(x_p, x_s, state_p, state_s, k_win0, v_win0, w):
    batch, seq, _ = x_p.shape
    batch2 = x_s.shape[0]
    assert x_s.shape[1] == 1
    hp = x_p.reshape(batch * seq, D_MODEL)
    hs = x_s.reshape(batch2, D_MODEL)
    c0_p, n0_p, m0_p, conv0_p = state_p
    c0_s, n0_s, m0_s, conv0_s = state_s
    qk_w = M_HEADS * M_DK
    v_w = M_HEADS * M_DV
    kv_w = A_KV_HEADS * A_HEAD_DIM
    q_w = A_HEADS * A_HEAD_DIM

    g_mix0 = w["g_mix"][0:1]
    hn_p, = _rms_norm(hp, g_mix0, BF16)
    hn_s, = _rms_norm(hs, g_mix0, BF16)
    wt_in = _Weight(w["wt_mlstm_in"], 0, transposed=True)
    qkv_p, qkv_s = _linear(hn_p, hn_s, wt_in, col0=0, n=2 * qk_w + v_w, out_dtype=BF16,
                           scaled_cols=qk_w, scale=M_DK ** -0.5, name="mlstm_qkv")
    o_p, o_s = _linear(hn_p, hn_s, wt_in, col0=2 * qk_w + v_w, n=v_w, out_dtype=F32, name="mlstm_ogate")
    wt_gates = w["wt_mlstm_in"][0, 2 * qk_w + 2 * v_w:, :]
    gates_p = _mlstm_gates(hn_p, wt_gates, w["b_mlstm_gates"][0])
    gates_s = _mlstm_gates(hn_s, wt_gates, w["b_mlstm_gates"][0])
    hg_p, c_p, n_p, m_p = _mlstm_chunks(qkv_p, o_p, gates_p, c0_p[0], n0_p[0], m0_p[0], batch, seq)
    hg_s, c_s, n_s, m_s = _mlstm_step(qkv_s, o_s, gates_s, c0_s[0], n0_s[0], m0_s[0])
    hp, hs = _linear(hg_p, hg_s, _Weight(w["w_mlstm_out"], 0), col0=0, n=D_MODEL, out_dtype=F32,
                     residual=hp, residual2=hs, name="mlstm_out")
    (hp, conv_a_p), (hs, conv_a_s) = _conv_ffn(
        hp, hs, w["g_ffn"][0], _Weight(w["w_ffn_up"], 0), w["ffn_conv_w"][0], w["ffn_conv_b"][0],
        _Weight(w["w_ffn_down"], 0), conv0_p[0], conv0_s[0], batch, seq)

    g_kv_mix = jnp.stack([w["g_kv"], w["g_mix"][1]])
    kvn_p, hn_p = _rms_norm(hp, g_kv_mix, BF16)
    kvn_s, hn_s = _rms_norm(hs, g_kv_mix, BF16)
    tab_p = _rope_tables(jnp.arange(seq))
    tab_s = tuple(jnp.broadcast_to(t, (batch2, A_HEAD_DIM)) for t in _rope_tables(PAST_LEN + jnp.arange(1)))
    (kv_bf_p, kv_f32_p), (_, kv_f32_s) = _rope_linear(
        kvn_p, kvn_s, _Weight(w["w_kv"][None], 0), tab_p, tab_s, n=2 * kv_w, rope_cols=kv_w,
        out_dtypes=(BF16, F32), rows_per_seq=seq, name="kv_proj")
    (q_p,), (q_s,) = _rope_linear(
        hn_p, hn_s, _Weight(w["w_attn_q"], 0), tab_p, tab_s, n=q_w, rope_cols=q_w,
        out_dtypes=(BF16,), rows_per_seq=seq, name="q_proj")
    sinks = w["attn_sinks"][0]
    att_p = _swa_attention(q_p, kv_bf_p, sinks, batch, seq)
    kv4 = kv_f32_p.reshape(batch, seq, 2, A_KV_HEADS, A_HEAD_DIM)
    k_win_p, v_win_p = kv4[:, -WINDOW:, 0], kv4[:, -WINDOW:, 1]
    att_s, k_win_s, v_win_s = _decode_attention(
        q_s, kv_f32_s[:, :kv_w], kv_f32_s[:, kv_w:], k_win0, v_win0, sinks)
    hp, hs = _linear(att_p, att_s, _Weight(w["w_attn_o"], 0), col0=0, n=D_MODEL, out_dtype=F32,
                     residual=hp, residual2=hs, name="attn_out")
    (hp, conv_b_p), (hs, conv_b_s) = _conv_ffn(
        hp, hs, w["g_ffn"][1], _Weight(w["w_ffn_up"], 1), w["ffn_conv_w"][1], w["ffn_conv_b"][1],
        _Weight(w["w_ffn_down"], 1), conv0_p[1], conv0_s[1], batch, seq)
    g_final = w["g_final"].reshape(1, D_MODEL)
    y_p, = _rms_norm(hp, g_final, F32)
    y_s, = _rms_norm(hs, g_final, F32)
    out_p = (y_p.reshape(batch, seq, D_MODEL), c_p[None], n_p[None], m_p[None], k_win_p, v_win_p,
             jnp.stack([conv_a_p, conv_b_p]))
    out_s = (y_s.reshape(batch2, 1, D_MODEL), c_s[None], n_s[None], m_s[None], k_win_s, v_win_s,
             jnp.stack([conv_a_s, conv_b_s]))
    return out_p, out_s


def kernel(x_prompt, x_sample, state_mlstm_C, state_mlstm_n, state_mlstm_m, cache_k_win, cache_v_win,
           state_ffn_conv, g_mix, g_ffn, w_mlstm_in, b_mlstm_gates, w_mlstm_out, g_kv, w_kv, w_attn_q,
           attn_sinks, w_attn_o, w_ffn_up, ffn_conv_w, ffn_conv_b, w_ffn_down, g_final):
    w = dict(g_mix=g_mix, g_ffn=g_ffn, wt_mlstm_in=jnp.swapaxes(w_mlstm_in, 1, 2),
             b_mlstm_gates=b_mlstm_gates,
             w_mlstm_out=w_mlstm_out, g_kv=g_kv, w_kv=w_kv, w_attn_q=w_attn_q, attn_sinks=attn_sinks,
             w_attn_o=w_attn_o, w_ffn_up=w_ffn_up, ffn_conv_w=ffn_conv_w, ffn_conv_b=ffn_conv_b,
             w_ffn_down=w_ffn_down, g_final=g_final)
    bp = x_prompt.shape[0]
    n_a = state_mlstm_C.shape[0]
    depth = state_ffn_conv.shape[0]
    state_p = (jnp.zeros((n_a, bp, M_HEADS, M_DK, M_DV), F32), jnp.zeros((n_a, bp, M_HEADS, M_DK), F32),
               jnp.zeros((n_a, bp, M_HEADS), F32), jnp.zeros((depth, bp, CONV_W - 1, D_FF), F32))
    state_s = (state_mlstm_C, state_mlstm_n, state_mlstm_m, state_ffn_conv)
    prompt, sample = _model(x_prompt, x_sample, state_p, state_s, cache_k_win, cache_v_win, w)
    return (prompt[0], sample[0]) + prompt[1:] + sample[1:]
```

```python
import functools
import math
from typing import NamedTuple

import jax
import jax.numpy as jnp
from jax import lax
from jax.experimental import pallas as pl
from jax.experimental.pallas import tpu as pltpu

F32 = jnp.float32
BF16 = jnp.bfloat16

D_MODEL = 4096
PAST_LEN = 16384
M_HEADS = 8
M_DK = D_MODEL // (2 * M_HEADS)
M_DV = D_MODEL // M_HEADS
M_CHUNK = 128
GATE_SOFT_CAP = 15.0
A_HEADS = 32
A_KV_HEADS = 8
A_GROUP = A_HEADS // A_KV_HEADS
A_HEAD_DIM = D_MODEL // A_HEADS
WINDOW = 128
ROPE_THETA = 10000.0
NEG_INF = -1e30
D_FF = 11008
CONV_W = 3
RMS_EPS = 1e-6

V7X_VMEM_BYTES = 64 * 1024 * 1024
VMEM_LIMIT_BYTES = V7X_VMEM_BYTES - 8 * 1024 * 1024
LANES = 128
SUBLANES = 8

MM_BLOCK_M = 1024
MM_BLOCK_M_F32_OUT = 512
MM_BLOCK_N = 1024
FFN_BLOCK_N = 256
DELAYED_ROW_PIECES = 4
DOWN_BLOCK_M = 512
DOWN_BLOCK_N = 512
RMS_BLOCK_M = 512
DECODE_ATTN_BLOCK_B = 4
MLSTM_HEAD_GROUP = 8
CAST_ROWS = 256


def _params(sem):
    return pltpu.CompilerParams(dimension_semantics=sem, vmem_limit_bytes=VMEM_LIMIT_BYTES)


def _rms_kernel(x_ref, g_ref, *o_refs):
    xf = x_ref[...]
    r = lax.rsqrt(jnp.mean(xf * xf, axis=-1, keepdims=True) + RMS_EPS)
    xr = xf * r
    for k, o_ref in enumerate(o_refs):
        o_ref[...] = (xr * g_ref[k:k + 1, :]).astype(o_ref.dtype)


def _rms_norm(x, gains, out_dtype):
    m, d = x.shape
    n_g = gains.shape[0]
    bm = min(RMS_BLOCK_M, m)
    assert m % bm == 0
    outs = pl.pallas_call(
        _rms_kernel,
        grid=(m // bm,),
        in_specs=[pl.BlockSpec((bm, d), lambda i: (i, 0)),
                  pl.BlockSpec((n_g, d), lambda i: (0, 0))],
        out_specs=[pl.BlockSpec((bm, d), lambda i: (i, 0)) for _ in range(n_g)],
        out_shape=[jax.ShapeDtypeStruct((m, d), out_dtype) for _ in range(n_g)],
        compiler_params=_params(("arbitrary",)),
        name="rms_norm",
    )(x, gains)
    return outs


def _cast_weight(w_ref, wbf_ref, col0, ncols, transposed):
    k = wbf_ref.shape[0]
    rows_per = min(CAST_ROWS, k)
    assert k % rows_per == 0

    def body(r, carry):
        rows = pl.ds(pl.multiple_of(r * rows_per, rows_per), rows_per)
        if transposed:
            wbf_ref[rows, col0:col0 + ncols] = w_ref[:, rows].astype(BF16).T
        else:
            wbf_ref[rows, col0:col0 + ncols] = w_ref[rows, :].astype(BF16)
        return carry

    lax.fori_loop(0, k // rows_per, body, 0)


class _Prefetch(NamedTuple):
    col_blocks: tuple
    layer: int
    bn: int


def _mm_kernel(*refs, n_w, n_w_refs, counts, epilogue, cast, transposed, prefetch, n_i, n_j, delayed):
    n_extra, n_extra2, n_out, n_out2 = counts
    x_ref, x2_ref = refs[0], refs[1]
    refs = refs[2:]
    w_refs, refs = refs[:n_w_refs], refs[n_w_refs:]
    extra, refs = refs[:n_extra], refs[n_extra:]
    extra2, refs = refs[:n_extra2], refs[n_extra2:]
    outs, refs = refs[:n_out], refs[n_out:]
    outs2, scratch = refs[:n_out2], refs[n_out2:]
    bm = x_ref.shape[0]
    if delayed:
        n_tiles = n_j * n_i
        s = pl.program_id(0)
        cur = jnp.minimum(s, n_tiles - 1)
        prev = jnp.maximum(s - 1, 0)
        j, i = cur // n_i, cur % n_i
        j_prev, i_prev = prev // n_i, prev % n_i
    else:
        j = pl.program_id(0)
        i = pl.program_id(1)
    if cast:
        wbf_ref = scratch[0]
        scratch = scratch[1:]
    if prefetch is not None:
        w_hbm = w_refs[0]
        stage_ref, sem_ref = scratch[0], scratch[1]
        scratch = scratch[2:]
        w_refs = [stage_ref.at[t] for t in range(n_w)]

        def slab_copy(t, jj):
            start = pl.multiple_of(prefetch.col_blocks[t](jj) * prefetch.bn, prefetch.bn)
            cols = pl.ds(start, prefetch.bn)
            src = w_hbm.at[prefetch.layer, cols, :] if transposed else w_hbm.at[prefetch.layer, :, cols]
            return pltpu.make_async_copy(src, stage_ref.at[t], sem_ref.at[t])

    if delayed:
        acc_ref, acc2_ref = scratch[0], scratch[1]
        scratch = scratch[2:]
    w_ref = wbf_ref if cast else w_refs[0]

    def load_weights():
        if prefetch is not None:
            @pl.when(j == 0)
            def _():
                for t in range(n_w):
                    slab_copy(t, j).start()

            for t in range(n_w):
                slab_copy(t, j).wait()
        if cast:
            bn = wbf_ref.shape[1] // n_w
            for t, wt_ref in enumerate(w_refs):
                _cast_weight(wt_ref, wbf_ref, t * bn, bn, transposed)
        if prefetch is not None:
            @pl.when(j + 1 < n_j)
            def _():
                for t in range(n_w):
                    slab_copy(t, j + 1).start()

    def tile_dot(with_rows2):
        rows = x_ref[...]
        if with_rows2:
            rows = jnp.concatenate([rows, x2_ref[...]], axis=0)
        acc = jnp.dot(rows, w_ref[...], preferred_element_type=F32)
        return acc[:bm], acc[bm:]

    if not delayed:
        @pl.when(i == 0)
        def _():
            epilogue.init_scratch(scratch)
            load_weights()
            acc, acc2 = tile_dot(True)
            epilogue.tile(acc, i, j, extra, outs, scratch)
            epilogue.rows(acc2, j, extra, extra2, outs2)

        @pl.when(i != 0)
        def _():
            acc, _ = tile_dot(False)
            epilogue.tile(acc, i, j, extra, outs, scratch)

        return

    cm = bm // DELAYED_ROW_PIECES

    def step(multiply, rows2_now, rows2_prev):
        if rows2_prev:
            epilogue.rows(acc2_ref[...], j_prev, extra, extra2, outs2)
        for p in range(DELAYED_ROW_PIECES):
            rows = slice(p * cm, (p + 1) * cm)
            epilogue.piece(p, DELAYED_ROW_PIECES, acc_ref[rows, :], i_prev, j_prev, extra, outs, scratch)
            if multiply:
                lhs = x_ref[rows, :]
                last = p == DELAYED_ROW_PIECES - 1
                if rows2_now and last:
                    lhs = jnp.concatenate([lhs, x2_ref[...]], axis=0)
                acc = jnp.dot(lhs, w_ref[...], preferred_element_type=F32)
                acc_ref[rows, :] = acc[:cm]
                if rows2_now and last:
                    acc2_ref[...] = acc[cm:]

    @pl.when(s == 0)
    def _():
        acc_ref[...] = jnp.zeros_like(acc_ref)
        epilogue.init_scratch(scratch)

    @pl.when((s < n_tiles) & (i == 0))
    def _():
        load_weights()
        step(True, True, False)

    @pl.when((s < n_tiles) & (i == 1))
    def _():
        step(True, False, True)

    @pl.when((s < n_tiles) & (i > 1))
    def _():
        step(True, False, False)

    @pl.when(s == n_tiles)
    def _():
        step(False, False, False)


class _Epilogue:
    def scratch_shapes(self, bm, bn):
        return []

    def init_scratch(self, scratch):
        pass

    def tile(self, acc, i, j, extra, outs, scratch):
        self.piece(0, 1, acc, i, j, extra, outs, scratch)

    def piece(self, p, n_pieces, acc, i, j, extra, outs, scratch):
        raise NotImplementedError


class _Weight(NamedTuple):
    arr: jax.Array
    layer: int = 0
    transposed: bool = False

    @property
    def k(self):
        return self.arr.shape[2 if self.transposed else 1]


def _matmul(x, x2, w, epilogue, *, w_col_blocks, bn, bm=MM_BLOCK_M, extra=(), extra_specs=(),
            extra2=(), extra2_specs=(), out_shapes, out_specs, out2_shapes, out2_specs, name,
            prefetch=False, delayed=False):
    m, k = x.shape
    m2 = x2.shape[0]
    bm = min(bm, m)
    assert m % bm == 0 and w.k == k and x2.shape[1] == k
    cast = w.arr.dtype != BF16
    assert cast or not w.transposed
    n_w = len(w_col_blocks)
    n_j = epilogue.n_col_steps
    n_i = m // bm
    delayed = delayed and n_i >= 2
    x_specs = [pl.BlockSpec((bm, k), lambda j, i: (i, 0)),
               pl.BlockSpec((m2, k), lambda j, i: (0, 0))]
    slab_shape = (bn, k) if w.transposed else (k, bn)
    if prefetch:
        assert cast
        w_specs = [pl.BlockSpec(memory_space=pl.ANY)]
        w_args = [w.arr]
    else:
        w_specs = []
        for f in w_col_blocks:
            if w.transposed:
                index_map = functools.partial(lambda j, i, f: (w.layer, f(j), 0), f=f)
            else:
                index_map = functools.partial(lambda j, i, f: (w.layer, 0, f(j)), f=f)
            w_specs.append(pl.BlockSpec((None,) + slab_shape, index_map))
        w_args = [w.arr] * n_w
    late_specs = list(extra_specs) + list(extra2_specs)
    all_out_specs = list(out_specs) + list(out2_specs)
    if delayed:
        n_tiles = n_j * n_i

        def at_step(spec, lag):
            if spec.block_shape is None:
                return spec

            def index_map(s):
                t = jnp.clip(s - lag, 0, n_tiles - 1)
                return spec.index_map(t // n_i, t % n_i)
            return pl.BlockSpec(spec.block_shape, index_map)

        x_specs = [at_step(sp, 0) for sp in x_specs]
        w_specs = [at_step(sp, 0) for sp in w_specs]
        late_specs = [at_step(sp, 1) for sp in late_specs]
        all_out_specs = [at_step(sp, 1) for sp in all_out_specs]
        grid, semantics = (n_tiles + 1,), ("arbitrary",)
    else:
        grid, semantics = (n_j, n_i), ("arbitrary", "arbitrary")
    scratch = []
    if cast:
        scratch.append(pltpu.VMEM((k, n_w * bn), BF16))
    if prefetch:
        scratch.append(pltpu.VMEM((n_w,) + slab_shape, F32))
        scratch.append(pltpu.SemaphoreType.DMA((n_w,)))
    if delayed:
        scratch.append(pltpu.VMEM((bm, n_w * bn), F32))
        scratch.append(pltpu.VMEM((m2, n_w * bn), F32))
    scratch.extend(epilogue.scratch_shapes(bm, bn))
    counts = (len(extra), len(extra2), len(out_shapes), len(out2_shapes))
    kern = functools.partial(
        _mm_kernel, n_w=n_w, n_w_refs=len(w_args), counts=counts, epilogue=epilogue, cast=cast,
        transposed=w.transposed,
        prefetch=_Prefetch(tuple(w_col_blocks), w.layer, bn) if prefetch else None,
        n_i=n_i, n_j=n_j, delayed=delayed)
    res = pl.pallas_call(
        kern,
        grid=grid,
        in_specs=x_specs + w_specs + late_specs,
        out_specs=all_out_specs,
        out_shape=list(out_shapes) + list(out2_shapes),
        scratch_shapes=scratch,
        compiler_params=_params(semantics),
        name=name,
    )(x, x2, *w_args, *extra, *extra2)
    return res[:len(out_shapes)], res[len(out_shapes):]


class _StoreEpilogue(_Epilogue):
    def __init__(self, n_col_steps, scaled_col_steps=0, scale=1.0, residual=False):
        self.n_col_steps = n_col_steps
        self.scaled_col_steps = scaled_col_steps
        self.scale = scale
        self.residual = residual

    def _store(self, acc, j, res_refs, out_ref):
        if self.scaled_col_steps:
            acc = acc * jnp.where(j < self.scaled_col_steps, self.scale, 1.0).astype(F32)
        if self.residual:
            acc = res_refs[0][...] + acc
        out_ref[...] = acc.astype(out_ref.dtype)

    def tile(self, acc, i, j, extra, outs, scratch):
        self._store(acc, j, extra, outs[0])

    def rows(self, acc, j, extra, extra2, outs2):
        self._store(acc, j, extra2, outs2[0])


def _linear(x, x2, w, *, col0, n, out_dtype, bn=MM_BLOCK_N, scaled_cols=0, scale=1.0,
            residual=None, residual2=None, bm=None, name):
    m, m2 = x.shape[0], x2.shape[0]
    if bm is None:
        bm = MM_BLOCK_M if out_dtype == BF16 else MM_BLOCK_M_F32_OUT
    bm = min(bm, m)
    assert n % bn == 0 and col0 % bn == 0 and scaled_cols % bn == 0
    assert (residual is None) == (residual2 is None)
    j0 = col0 // bn
    ep = _StoreEpilogue(n // bn, scaled_cols // bn, scale, residual is not None)
    tile_spec = pl.BlockSpec((bm, bn), lambda j, i: (i, j))
    rows_spec = pl.BlockSpec((m2, bn), lambda j, i: (0, j))
    has_res = residual is not None
    (out,), (out2,) = _matmul(
        x, x2, w, ep, w_col_blocks=[lambda j: j + j0], bn=bn, bm=bm,
        extra=(residual,) if has_res else (), extra_specs=(tile_spec,) if has_res else (),
        extra2=(residual2,) if has_res else (), extra2_specs=(rows_spec,) if has_res else (),
        out_shapes=[jax.ShapeDtypeStruct((m, n), out_dtype)], out_specs=[tile_spec],
        out2_shapes=[jax.ShapeDtypeStruct((m2, n), out_dtype)], out2_specs=[rows_spec],
        name=name, prefetch=w.arr.dtype != BF16)
    return out, out2


class _RopeEpilogue(_Epilogue):
    def __init__(self, n_col_steps, rope_col_steps, n_out):
        self.n_col_steps = n_col_steps
        self.rope_col_steps = rope_col_steps
        self.n_out = n_out

    def _store(self, acc, j, cos, sin, outs):
        def rotated():
            heads = []
            for h in range(acc.shape[1] // A_HEAD_DIM):
                xh = acc[:, h * A_HEAD_DIM:(h + 1) * A_HEAD_DIM]
                heads.append(xh * cos + pltpu.roll(xh, A_HEAD_DIM // 2, axis=1) * sin)
            return jnp.concatenate(heads, axis=1)

        if self.rope_col_steps >= self.n_col_steps:
            val = rotated()
        else:
            val = jnp.where(j < self.rope_col_steps, rotated(), acc)
        for o_ref in outs:
            o_ref[...] = val.astype(o_ref.dtype)

    def tile(self, acc, i, j, extra, outs, scratch):
        cos_ref, sin_ref = extra
        bm = acc.shape[0]
        rows = pl.ds(pl.multiple_of((i % (cos_ref.shape[0] // bm)) * bm, bm), bm)
        self._store(acc, j, cos_ref[rows, :], sin_ref[rows, :], outs)

    def rows(self, acc, j, extra, extra2, outs2):
        self._store(acc, j, extra2[0][...], extra2[1][...], outs2)


def _rope_linear(x, x2, w, tables, tables2, *, n, rope_cols, out_dtypes, rows_per_seq, name,
                 bn=MM_BLOCK_N):
    m, m2 = x.shape[0], x2.shape[0]
    bm = min(MM_BLOCK_M_F32_OUT, m, rows_per_seq)
    assert rows_per_seq % bm == 0 and n % bn == 0 and rope_cols % bn == 0
    tiles_per_seq = rows_per_seq // bm
    ep = _RopeEpilogue(n // bn, rope_cols // bn, len(out_dtypes))
    tab_spec = pl.BlockSpec((rows_per_seq, A_HEAD_DIM), lambda j, i: (0, 0))
    tab2_spec = pl.BlockSpec((m2, A_HEAD_DIM), lambda j, i: (0, 0))
    return _matmul(
        x, x2, w, ep, w_col_blocks=[lambda j: j], bn=bn, bm=bm,
        extra=tables, extra_specs=(tab_spec, tab_spec),
        extra2=tables2, extra2_specs=(tab2_spec, tab2_spec),
        out_shapes=[jax.ShapeDtypeStruct((m, n), dt) for dt in out_dtypes],
        out_specs=[pl.BlockSpec((bm, bn), lambda j, i: (i, j)) for _ in out_dtypes],
        out2_shapes=[jax.ShapeDtypeStruct((m2, n), dt) for dt in out_dtypes],
        out2_specs=[pl.BlockSpec((m2, bn), lambda j, i: (0, j)) for _ in out_dtypes],
        name=name, prefetch=True)


def _gates_kernel(x_ref, wt_ref, b_ref, o_ref):
    g = lax.dot_general(x_ref[...], wt_ref[...].astype(BF16), (((1,), (1,)), ((), ())),
                        preferred_element_type=F32) + b_ref[...]
    ig = GATE_SOFT_CAP * jnp.tanh(g / GATE_SOFT_CAP)
    lf = jnp.minimum(g, 0.0) - jnp.log1p(jnp.exp(-jnp.abs(g)))
    lane = lax.broadcasted_iota(jnp.int32, g.shape, 1)
    o_ref[...] = jnp.where(lane < M_HEADS, ig, lf)


def _mlstm_gates(xn, wt_gates, b_gates):
    m, d = xn.shape
    bm = min(MM_BLOCK_M, m)
    ng = 2 * M_HEADS
    return pl.pallas_call(
        _gates_kernel,
        grid=(m // bm,),
        in_specs=[pl.BlockSpec((bm, d), lambda i: (i, 0)),
                  pl.BlockSpec((ng, d), lambda i: (0, 0)),
                  pl.BlockSpec((1, ng), lambda i: (0, 0))],
        out_specs=pl.BlockSpec((bm, ng), lambda i: (i, 0)),
        out_shape=jax.ShapeDtypeStruct((m, ng), F32),
        compiler_params=_params(("arbitrary",)),
        name="mlstm_gates",
    )(xn, wt_gates, b_gates.reshape(1, ng).astype(F32))


def _mlstm_chunk_kernel(q_ref, k_ref, v_ref, o_ref, gcol_ref, grow_ref, c0_ref, n0_ref, m0_ref,
                        h_ref, c_ref, n_ref, m_ref):
    L = q_ref.shape[0]

    @pl.when(pl.program_id(1) == 0)
    def _():
        c_ref[...] = c0_ref[...]
        n_ref[...] = n0_ref[...]
        m_ref[...] = m0_ref[...]

    t_idx = lax.broadcasted_iota(jnp.int32, (L, L), 0)
    s_idx = lax.broadcasted_iota(jnp.int32, (L, L), 1)
    causal = s_idx <= t_idx
    qs = [q_ref[:, h * M_DK:(h + 1) * M_DK] for h in range(M_HEADS)]
    ks = [k_ref[:, h * M_DK:(h + 1) * M_DK] for h in range(M_HEADS)]
    vs = [v_ref[:, h * M_DV:(h + 1) * M_DV] for h in range(M_HEADS)]

    for g0 in range(0, M_HEADS, MLSTM_HEAD_GROUP):
        heads = range(g0, g0 + MLSTM_HEAD_GROUP)

        cums, dlogs, gw = {}, {}, {}
        for h in heads:
            lf_col = gcol_ref[:, M_HEADS + h:M_HEADS + h + 1]
            lf_row = grow_ref[M_HEADS + h:M_HEADS + h + 1, :]
            b_col = jnp.sum(jnp.where(causal, lf_row, 0.0), axis=1, keepdims=True)
            b_row = jnp.sum(jnp.where(t_idx <= s_idx, lf_col, 0.0), axis=0, keepdims=True)
            cums[h] = (b_col, b_row)
        for h in heads:
            b_col, b_row = cums[h]
            ig_row = grow_ref[h:h + 1, :]
            m_prev = m_ref[0, h:h + 1, 0:1]
            dlog = jnp.where(causal, b_col - b_row + ig_row, -jnp.inf)
            g = b_col + m_prev
            m_t = jnp.maximum(g, jnp.max(dlog, axis=1, keepdims=True))
            dlogs[h] = (dlog, g, m_t)
        for h in heads:
            b_col, _ = cums[h]
            dlog, g, m_t = dlogs[h]
            ig_col = gcol_ref[:, h:h + 1]
            w_intra = jnp.exp(dlog - m_t)
            w_inter = jnp.exp(g - m_t)
            b_last = b_col[L - 1:L, :]
            m_last = m_t[L - 1:L, :]
            w_end = jnp.exp(b_last - b_col + ig_col - m_last)
            gw[h] = (w_intra, w_inter, m_t, w_end)

        scores, invs = {}, {}
        for h in heads:
            w_intra = gw[h][0]
            qk = lax.dot_general(qs[h], ks[h], (((1,), (1,)), ((), ())), preferred_element_type=F32)
            scores[h] = qk * w_intra
        for h in heads:
            _, w_inter, m_t, _ = gw[h]
            n_prev = n_ref[0, h:h + 1, :]
            qn = jnp.sum(qs[h].astype(F32) * n_prev, axis=1, keepdims=True)
            den = jnp.sum(scores[h], axis=1, keepdims=True) + w_inter * qn
            invs[h] = 1.0 / jnp.maximum(jnp.abs(den), jnp.exp(-m_t))
        for h in heads:
            w_inter = gw[h][1]
            c_prev = c_ref[0, h]
            num = (jnp.dot(scores[h].astype(BF16), vs[h], preferred_element_type=F32)
                   + w_inter * jnp.dot(qs[h], c_prev.astype(BF16), preferred_element_type=F32))
            gate = jax.nn.sigmoid(o_ref[:, h * M_DV:(h + 1) * M_DV])
            h_ref[:, h * M_DV:(h + 1) * M_DV] = (gate * (num * invs[h])).astype(h_ref.dtype)

        kws = {}
        for h in heads:
            _, w_inter, m_t, w_end = gw[h]
            decay = w_inter[L - 1:L, :]
            kw = ks[h].astype(F32) * w_end
            n_ref[0, h:h + 1, :] = decay * n_ref[0, h:h + 1, :] + jnp.sum(kw, axis=0, keepdims=True)
            m_ref[0, h:h + 1, :] = jnp.broadcast_to(m_t[L - 1:L, :], (1, LANES))
            kws[h] = kw.astype(BF16)
        for h in heads:
            decay = gw[h][1][L - 1:L, :]
            c_ref[0, h] = decay * c_ref[0, h] + lax.dot_general(
                kws[h], vs[h], (((0,), (0,)), ((), ())), preferred_element_type=F32)


def _mlstm_chunks(qkv, o_pre, gates, c0, n0, m0, batch, seq):
    L = M_CHUNK
    assert seq % L == 0
    nc = seq // L
    qk_w = M_HEADS * M_DK
    v_w = M_HEADS * M_DV
    m_rows = batch * seq
    grow = gates.T
    m0b = jnp.broadcast_to(m0[:, :, None], (batch, M_HEADS, LANES))
    row = lambda b, c: b * nc + c
    hg, c_new, n_new, m_new = pl.pallas_call(
        _mlstm_chunk_kernel,
        grid=(batch, nc),
        in_specs=[
            pl.BlockSpec((L, qk_w), lambda b, c: (row(b, c), 0)),
            pl.BlockSpec((L, qk_w), lambda b, c: (row(b, c), 1)),
            pl.BlockSpec((L, v_w), lambda b, c: (row(b, c), 1)),
            pl.BlockSpec((L, v_w), lambda b, c: (row(b, c), 0)),
            pl.BlockSpec((L, 2 * M_HEADS), lambda b, c: (row(b, c), 0)),
            pl.BlockSpec((2 * M_HEADS, L), lambda b, c: (0, row(b, c))),
            pl.BlockSpec((1, M_HEADS, M_DK, M_DV), lambda b, c: (b, 0, 0, 0)),
            pl.BlockSpec((1, M_HEADS, M_DK), lambda b, c: (b, 0, 0)),
            pl.BlockSpec((1, M_HEADS, LANES), lambda b, c: (b, 0, 0)),
        ],
        out_specs=[
            pl.BlockSpec((L, v_w), lambda b, c: (row(b, c), 0)),
            pl.BlockSpec((1, M_HEADS, M_DK, M_DV), lambda b, c: (b, 0, 0, 0)),
            pl.BlockSpec((1, M_HEADS, M_DK), lambda b, c: (b, 0, 0)),
            pl.BlockSpec((1, M_HEADS, LANES), lambda b, c: (b, 0, 0)),
        ],
        out_shape=[
            jax.ShapeDtypeStruct((m_rows, v_w), BF16),
            jax.ShapeDtypeStruct((batch, M_HEADS, M_DK, M_DV), F32),
            jax.ShapeDtypeStruct((batch, M_HEADS, M_DK), F32),
            jax.ShapeDtypeStruct((batch, M_HEADS, LANES), F32),
        ],
        compiler_params=_params(("arbitrary", "arbitrary")),
        name="mlstm_chunks",
    )(qkv, qkv, qkv, o_pre, gates, grow, c0, n0, m0b)
    return hg, c_new, n_new, m_new[:, :, 0]


def _mlstm_step_kernel(q_ref, k_ref, v_ref, o_ref, g_ref, c0_ref, n0_ref, m0_ref,
                       h_ref, c_ref, n_ref, m_ref):
    dk_idx = lax.broadcasted_iota(jnp.int32, (M_DK, M_DK), 0)
    dk_lane = lax.broadcasted_iota(jnp.int32, (M_DK, M_DK), 1)
    eye = dk_idx == dk_lane
    m_rows = []
    for h in range(M_HEADS):
        q = q_ref[0, :, h * M_DK:(h + 1) * M_DK]
        k = k_ref[0, :, h * M_DK:(h + 1) * M_DK]
        v = v_ref[0, :, h * M_DV:(h + 1) * M_DV]
        ig = g_ref[0, :, h:h + 1]
        lf = g_ref[0, :, M_HEADS + h:M_HEADS + h + 1]
        c_prev = c0_ref[0, h]
        n_prev = n0_ref[0, h:h + 1, :]
        m_prev = m0_ref[0, h:h + 1, 0:1]
        qf, kf, vf = q.astype(F32), k.astype(F32), v.astype(F32)

        g = lf + m_prev
        m_t = jnp.maximum(g, ig)
        w_intra = jnp.exp(ig - m_t)
        w_inter = jnp.exp(g - m_t)
        s = jnp.sum(qf * kf, axis=1, keepdims=True) * w_intra
        q_rows = jnp.broadcast_to(q, (2 * SUBLANES, M_DK))
        qc = jnp.dot(q_rows, c_prev.astype(BF16), preferred_element_type=F32)[0:1, :]
        num = s * vf + w_inter * qc
        den = s + w_inter * jnp.sum(qf * n_prev, axis=1, keepdims=True)
        hh = num / jnp.maximum(jnp.abs(den), jnp.exp(-m_t))
        gate = jax.nn.sigmoid(o_ref[0, :, h * M_DV:(h + 1) * M_DV])
        h_ref[0, :, h * M_DV:(h + 1) * M_DV] = (gate * hh).astype(h_ref.dtype)

        kw = kf * w_intra
        kw_col = jnp.sum(jnp.where(eye, kw, 0.0), axis=1, keepdims=True)
        c_ref[0, h] = w_inter * c_prev + kw_col * vf
        n_ref[0, h:h + 1, :] = w_inter * n_prev + kw
        m_rows.append(jnp.broadcast_to(m_t, (1, LANES)))
    m_ref[0] = jnp.concatenate(m_rows, axis=0)


def _mlstm_step(qkv, o_pre, gates, c0, n0, m0):
    batch = qkv.shape[0]
    qk_w = M_HEADS * M_DK
    v_w = M_HEADS * M_DV
    qkv3 = qkv.reshape(batch, 1, 2 * qk_w + v_w)
    m0b = jnp.broadcast_to(m0[:, :, None], (batch, M_HEADS, LANES))
    hg, c_new, n_new, m_new = pl.pallas_call(
        _mlstm_step_kernel,
        grid=(batch,),
        in_specs=[
            pl.BlockSpec((1, 1, qk_w), lambda b: (b, 0, 0)),
            pl.BlockSpec((1, 1, qk_w), lambda b: (b, 0, 1)),
            pl.BlockSpec((1, 1, v_w), lambda b: (b, 0, 1)),
            pl.BlockSpec((1, 1, v_w), lambda b: (b, 0, 0)),
            pl.BlockSpec((1, 1, 2 * M_HEADS), lambda b: (b, 0, 0)),
            pl.BlockSpec((1, M_HEADS, M_DK, M_DV), lambda b: (b, 0, 0, 0)),
            pl.BlockSpec((1, M_HEADS, M_DK), lambda b: (b, 0, 0)),
            pl.BlockSpec((1, M_HEADS, LANES), lambda b: (b, 0, 0)),
        ],
        out_specs=[
            pl.BlockSpec((1, 1, v_w), lambda b: (b, 0, 0)),
            pl.BlockSpec((1, M_HEADS, M_DK, M_DV), lambda b: (b, 0, 0, 0)),
            pl.BlockSpec((1, M_HEADS, M_DK), lambda b: (b, 0, 0)),
            pl.BlockSpec((1, M_HEADS, LANES), lambda b: (b, 0, 0)),
        ],
        out_shape=[
            jax.ShapeDtypeStruct((batch, 1, v_w), BF16),
            jax.ShapeDtypeStruct((batch, M_HEADS, M_DK, M_DV), F32),
            jax.ShapeDtypeStruct((batch, M_HEADS, M_DK), F32),
            jax.ShapeDtypeStruct((batch, M_HEADS, LANES), F32),
        ],
        compiler_params=_params(("arbitrary",)),
        name="mlstm_step",
    )(qkv3, qkv3, qkv3, o_pre.reshape(batch, 1, v_w), gates.reshape(batch, 1, 2 * M_HEADS),
      c0, n0, m0b)
    return hg.reshape(batch, v_w), c_new, n_new, m_new[:, :, 0]


def _softmax_with_sink(scores, sink):
    mx = jnp.maximum(jnp.max(scores, axis=-1, keepdims=True), sink)
    e = jnp.exp(scores - mx)
    return e / (jnp.sum(e, axis=-1, keepdims=True) + jnp.exp(sink - mx))


def _swa_kernel(sink_ref, q_ref, kp_ref, ko_ref, vp_ref, vo_ref, o_ref):
    W = WINDOW
    D = A_HEAD_DIM
    blk = pl.program_id(1)
    i_idx = lax.broadcasted_iota(jnp.int32, (W, 2 * W), 0)
    j_idx = lax.broadcasted_iota(jnp.int32, (W, 2 * W), 1)
    first_key = jnp.where(blk > 0, 0, W)
    mask = (j_idx >= jnp.maximum(i_idx, first_key)) & (j_idx <= i_idx + W)
    def head_scores(kv):
        cols = slice(kv * D, (kv + 1) * D)
        k_cat = jnp.concatenate([kp_ref[:, cols], ko_ref[:, cols]], axis=0)
        q = jnp.concatenate([q_ref[:, (kv * A_GROUP + g) * D:(kv * A_GROUP + g + 1) * D]
                             for g in range(A_GROUP)], axis=0)
        sc = lax.dot_general(q, k_cat, (((1,), (1,)), ((), ())), preferred_element_type=F32)
        return sc * (A_HEAD_DIM ** -0.5)

    next_scores = head_scores(0)
    for kv in range(A_KV_HEADS):
        cols = slice(kv * D, (kv + 1) * D)
        v_cat = jnp.concatenate([vp_ref[:, cols], vo_ref[:, cols]], axis=0)
        scores = next_scores
        if kv + 1 < A_KV_HEADS:
            next_scores = head_scores(kv + 1)
        probs = []
        for g in range(A_GROUP):
            sc = jnp.where(mask, scores[g * W:(g + 1) * W, :], NEG_INF)
            probs.append(_softmax_with_sink(sc, sink_ref[kv * A_GROUP + g]).astype(BF16))
        p = jnp.concatenate(probs, axis=0)
        o = jnp.dot(p, v_cat, preferred_element_type=F32)
        for g in range(A_GROUP):
            hq = kv * A_GROUP + g
            o_ref[:, hq * D:(hq + 1) * D] = o[g * W:(g + 1) * W, :].astype(o_ref.dtype)


def _swa_attention(q, kv, sinks, batch, seq):
    W = WINDOW
    assert seq % W == 0
    nb = seq // W
    kvw = A_KV_HEADS * A_HEAD_DIM
    qw = A_HEADS * A_HEAD_DIM
    row = lambda b, n: b * nb + n
    prev = lambda b, n: jnp.maximum(b * nb + n - 1, 0)
    return pl.pallas_call(
        _swa_kernel,
        grid=(batch, nb),
        in_specs=[
            pl.BlockSpec(memory_space=pltpu.SMEM),
            pl.BlockSpec((W, qw), lambda b, n: (row(b, n), 0)),
            pl.BlockSpec((W, kvw), lambda b, n: (prev(b, n), 0)),
            pl.BlockSpec((W, kvw), lambda b, n: (row(b, n), 0)),
            pl.BlockSpec((W, kvw), lambda b, n: (prev(b, n), 1)),
            pl.BlockSpec((W, kvw), lambda b, n: (row(b, n), 1)),
        ],
        out_specs=pl.BlockSpec((W, qw), lambda b, n: (row(b, n), 0)),
        out_shape=jax.ShapeDtypeStruct((batch * seq, qw), BF16),
        compiler_params=_params(("arbitrary", "arbitrary")),
        name="swa_attention",
    )(sinks.astype(F32), q, kv, kv, kv, kv)


def _decode_attn_kernel(sink_ref, q_ref, kn_ref, vn_ref, ck_ref, cv_ref, o_ref, kw_ref, vw_ref):
    W, D, KV, H = WINDOW, A_HEAD_DIM, A_KV_HEADS, A_HEADS
    scale = A_HEAD_DIM ** -0.5
    head_kv = lax.broadcasted_iota(jnp.int32, (H, W * KV), 0) // A_GROUP
    own_c = (lax.broadcasted_iota(jnp.int32, (H, W * KV), 1) % KV) == head_kv
    own_n = lax.broadcasted_iota(jnp.int32, (H, KV), 1) == head_kv[:, :KV]
    sink = sink_ref[:, 0:1]
    nt = (((1,), (1,)), ((), ()))
    for b in range(q_ref.shape[0]):
        q = q_ref[b]
        kc = ck_ref[b].reshape(W * KV, D).astype(BF16)
        vc = cv_ref[b].reshape(W * KV, D).astype(BF16)
        kn = kn_ref[b].astype(BF16)
        vn = vn_ref[b].astype(BF16)
        s_c = lax.dot_general(q, kc, nt, preferred_element_type=F32) * scale
        s_n = lax.dot_general(q, kn, nt, preferred_element_type=F32) * scale
        s_c = jnp.where(own_c, s_c, NEG_INF)
        s_n = jnp.where(own_n, s_n, NEG_INF)
        mx = jnp.maximum(jnp.maximum(jnp.max(s_c, axis=1, keepdims=True),
                                     jnp.max(s_n, axis=1, keepdims=True)), sink)
        e_c = jnp.exp(s_c - mx)
        e_n = jnp.exp(s_n - mx)
        den = (jnp.sum(e_c, axis=1, keepdims=True) + jnp.sum(e_n, axis=1, keepdims=True)
               + jnp.exp(sink - mx))
        o = (jnp.dot((e_c / den).astype(BF16), vc, preferred_element_type=F32)
             + jnp.dot((e_n / den).astype(BF16), vn, preferred_element_type=F32))
        o_ref[b] = o.astype(o_ref.dtype)
        kw_ref[b, 0:W - 1] = ck_ref[b, 1:W]
        kw_ref[b, W - 1] = kn_ref[b]
        vw_ref[b, 0:W - 1] = cv_ref[b, 1:W]
        vw_ref[b, W - 1] = vn_ref[b]


def _decode_attention(q, k_new, v_new, cache_k, cache_v, sinks):
    batch = q.shape[0]
    bb = math.gcd(batch, DECODE_ATTN_BLOCK_B)
    H, KV, D, W = A_HEADS, A_KV_HEADS, A_HEAD_DIM, WINDOW
    new_spec = pl.BlockSpec((bb, KV, D), lambda b: (b, 0, 0))
    win_spec = pl.BlockSpec((bb, W, KV, D), lambda b: (b, 0, 0, 0))
    o, k_win, v_win = pl.pallas_call(
        _decode_attn_kernel,
        grid=(batch // bb,),
        in_specs=[pl.BlockSpec((H, LANES), lambda b: (0, 0)),
                  pl.BlockSpec((bb, H, D), lambda b: (b, 0, 0)),
                  new_spec, new_spec, win_spec, win_spec],
        out_specs=[pl.BlockSpec((bb, H, D), lambda b: (b, 0, 0)), win_spec, win_spec],
        out_shape=[jax.ShapeDtypeStruct((batch, H, D), BF16),
                   jax.ShapeDtypeStruct((batch, W, KV, D), F32),
                   jax.ShapeDtypeStruct((batch, W, KV, D), F32)],
        compiler_params=_params(("arbitrary",)),
        name="decode_attention",
    )(jnp.broadcast_to(sinks.astype(F32)[:, None], (H, LANES)), q.reshape(batch, H, D),
      k_new.reshape(batch, KV, D), v_new.reshape(batch, KV, D), cache_k, cache_v)
    return o.reshape(batch, H * D), k_win, v_win


def _conv_act(g2, g1, g0, val, cw_ref, cb_ref):
    conv = cb_ref[...] + cw_ref[0:1, :] * g2
    conv = conv + cw_ref[1:2, :] * g1
    conv = conv + cw_ref[2:3, :] * g0
    return conv * jax.nn.sigmoid(conv) * val


class _ConvFfnEpilogue(_Epilogue):
    def __init__(self, n_col_steps, tiles_per_seq):
        self.n_col_steps = n_col_steps
        self.tiles_per_seq = tiles_per_seq

    def scratch_shapes(self, bm, bn):
        return [pltpu.VMEM((SUBLANES, bn), F32)]

    def init_scratch(self, scratch):
        scratch[0][...] = jnp.zeros_like(scratch[0])

    def rows(self, acc, j, extra, extra2, outs2):
        cw_ref, cb_ref = extra[:2]
        g2_ref, g1_ref, wd_ref = extra2
        act_ref, gate_ref, wd_bf_ref = outs2
        wd_bf_ref[...] = wd_ref[...].astype(wd_bf_ref.dtype)
        bn = acc.shape[1] // 2
        gate, val = acc[:, :bn], acc[:, bn:]
        act_ref[...] = _conv_act(g2_ref[...], g1_ref[...], gate, val, cw_ref, cb_ref).astype(act_ref.dtype)
        gate_ref[...] = gate

    def piece(self, p, n_pieces, acc, i, j, extra, outs, scratch):
        cw_ref, cb_ref, halo0_ref = extra
        act_ref, tail_ref = outs
        carry_ref, = scratch
        rows = acc.shape[0]
        bn = acc.shape[1] // 2
        gate, val = acc[:, :bn], acc[:, bn:]
        halo = carry_ref[...]
        if p == 0:
            seq_start = (i % self.tiles_per_seq) == 0
            halo = jnp.where(seq_start, halo0_ref[i // self.tiles_per_seq], halo)
        row = lax.broadcasted_iota(jnp.int32, (SUBLANES, bn), 0)

        def shifted(d):
            body = pltpu.roll(gate, d, axis=0)
            top = jnp.where(row < d, pltpu.roll(halo, d, axis=0), body[:SUBLANES])
            return jnp.concatenate([top, body[SUBLANES:]], axis=0)

        g1, g2 = shifted(1), shifted(2)
        act = _conv_act(g2, g1, gate, val, cw_ref, cb_ref).astype(act_ref.dtype)
        act_ref[p * rows:(p + 1) * rows, :] = act
        tail = gate[rows - SUBLANES:, :]
        carry_ref[...] = tail
        if p == n_pieces - 1:
            tail_ref[i] = tail


def _conv_ffn_up(xn, xn2, w_up, w_down, conv_w, conv_b, conv0, conv0_2, batch, seq):
    m, m2 = xn.shape[0], xn2.shape[0]
    bn = FFN_BLOCK_N
    assert D_FF % bn == 0
    n_j = D_FF // bn
    cw = conv_w.astype(F32)
    cb = conv_b.reshape(1, D_FF).astype(F32)
    col_spec = lambda rows: pl.BlockSpec((rows, bn), lambda j, i: (0, j))
    bm = min(MM_BLOCK_M, seq)
    assert seq % bm == 0 and bm % SUBLANES == 0
    tiles_per_seq = seq // bm
    n_i = m // bm
    halo0 = jnp.concatenate(
        [jnp.zeros((batch, SUBLANES - (CONV_W - 1), D_FF), F32), conv0.astype(F32)], axis=1)
    g2, g1 = conv0_2[:, 0, :], conv0_2[:, 1, :]
    rows_spec = pl.BlockSpec((m2, bn), lambda j, i: (0, j))
    k_down, n_down = w_down.arr.shape[1:]
    wd_rows = k_down // n_j
    assert wd_rows * n_j == k_down and wd_rows % (2 * SUBLANES) == 0 and not w_down.transposed
    (act, tails), (act2, gate2, wd_bf16) = _matmul(
        xn, xn2, w_up, _ConvFfnEpilogue(n_j, tiles_per_seq),
        w_col_blocks=[lambda j: j, lambda j: j + n_j], bn=bn, bm=bm,
        extra=(cw, cb, halo0),
        extra_specs=(col_spec(CONV_W), col_spec(1),
                     pl.BlockSpec((batch, SUBLANES, bn), lambda j, i: (0, 0, j))),
        extra2=(g2, g1, w_down.arr),
        extra2_specs=(rows_spec, rows_spec,
                      pl.BlockSpec((None, wd_rows, n_down), lambda j, i: (w_down.layer, j, 0))),
        out_shapes=[jax.ShapeDtypeStruct((m, D_FF), BF16),
                    jax.ShapeDtypeStruct((n_i, SUBLANES, D_FF), F32)],
        out_specs=[pl.BlockSpec((bm, bn), lambda j, i: (i, j)),
                   pl.BlockSpec((n_i, SUBLANES, bn), lambda j, i: (0, 0, j))],
        out2_shapes=[jax.ShapeDtypeStruct((m2, D_FF), BF16), jax.ShapeDtypeStruct((m2, D_FF), F32),
                     jax.ShapeDtypeStruct((k_down, n_down), BF16)],
        out2_specs=[rows_spec, rows_spec, pl.BlockSpec((wd_rows, n_down), lambda j, i: (j, 0))],
        name="conv_ffn_up", prefetch=True, delayed=True)
    last_tiles = tails.reshape(batch, tiles_per_seq, SUBLANES, D_FF)[:, -1]
    return ((act, last_tiles[:, SUBLANES - (CONV_W - 1):, :]),
            (act2, jnp.stack([g1, gate2], axis=1)), wd_bf16)


def _conv_ffn(h, h2, gain, w_up, conv_w, conv_b, w_down, conv0, conv0_2, batch, seq):
    gain = gain.reshape(1, D_MODEL)
    xn, = _rms_norm(h, gain, BF16)
    xn2, = _rms_norm(h2, gain, BF16)
    (act, conv_new), (act2, conv_new2), wd_bf16 = _conv_ffn_up(
        xn, xn2, w_up, w_down, conv_w, conv_b, conv0, conv0_2, batch, seq)
    out, out2 = _linear(act, act2, _Weight(wd_bf16[None], 0), col0=0, n=D_MODEL, out_dtype=F32,
                        bn=DOWN_BLOCK_N, bm=DOWN_BLOCK_M, residual=h, residual2=h2, name="ffn_down")
    return (out, conv_new), (out2, conv_new2)


def _rope_tables(pos):
    inv_freq = ROPE_THETA ** (-jnp.arange(0, A_HEAD_DIM, 2, dtype=F32) / A_HEAD_DIM)
    ang = pos.astype(F32)[:, None] * inv_freq[None, :]
    cos, sin = jnp.cos(ang), jnp.sin(ang)
    return jnp.concatenate([cos, cos], axis=1), jnp.concatenate([-sin, sin], axis=1)


def _model(x_p, x_s, state_p, state_s, k_win0, v_win0, w):
    batch, seq, _ = x_p.shape
    batch2 = x_s.shape[0]
    assert x_s.shape[1] == 1
    hp = x_p.reshape(batch * seq, D_MODEL)
    hs = x_s.reshape(batch2, D_MODEL)
    c0_p, n0_p, m0_p, conv0_p = state_p
    c0_s, n0_s, m0_s, conv0_s = state_s
    qk_w = M_HEADS * M_DK
    v_w = M_HEADS * M_DV
    kv_w = A_KV_HEADS * A_HEAD_DIM
    q_w = A_HEADS * A_HEAD_DIM

    g_mix0 = w["g_mix"][0:1]
    hn_p, = _rms_norm(hp, g_mix0, BF16)
    hn_s, = _rms_norm(hs, g_mix0, BF16)
    wt_in = _Weight(w["wt_mlstm_in"], 0, transposed=True)
    qkv_p, qkv_s = _linear(hn_p, hn_s, wt_in, col0=0, n=2 * qk_w + v_w, out_dtype=BF16,
                           scaled_cols=qk_w, scale=M_DK ** -0.5, name="mlstm_qkv")
    o_p, o_s = _linear(hn_p, hn_s, wt_in, col0=2 * qk_w + v_w, n=v_w, out_dtype=F32, name="mlstm_ogate")
    wt_gates = w["wt_mlstm_in"][0, 2 * qk_w + 2 * v_w:, :]
    gates_p = _mlstm_gates(hn_p, wt_gates, w["b_mlstm_gates"][0])
    gates_s = _mlstm_gates(hn_s, wt_gates, w["b_mlstm_gates"][0])
    hg_p, c_p, n_p, m_p = _mlstm_chunks(qkv_p, o_p, gates_p, c0_p[0], n0_p[0], m0_p[0], batch, seq)
    hg_s, c_s, n_s, m_s = _mlstm_step(qkv_s, o_s, gates_s, c0_s[0], n0_s[0], m0_s[0])
    hp, hs = _linear(hg_p, hg_s, _Weight(w["w_mlstm_out"], 0), col0=0, n=D_MODEL, out_dtype=F32,
                     residual=hp, residual2=hs, name="mlstm_out")
    (hp, conv_a_p), (hs, conv_a_s) = _conv_ffn(
        hp, hs, w["g_ffn"][0], _Weight(w["w_ffn_up"], 0), w["ffn_conv_w"][0], w["ffn_conv_b"][0],
        _Weight(w["w_ffn_down"], 0), conv0_p[0], conv0_s[0], batch, seq)

    g_kv_mix = jnp.stack([w["g_kv"], w["g_mix"][1]])
    kvn_p, hn_p = _rms_norm(hp, g_kv_mix, BF16)
    kvn_s, hn_s = _rms_norm(hs, g_kv_mix, BF16)
    tab_p = _rope_tables(jnp.arange(seq))
    tab_s = tuple(jnp.broadcast_to(t, (batch2, A_HEAD_DIM)) for t in _rope_tables(PAST_LEN + jnp.arange(1)))
    (kv_bf_p, kv_f32_p), (_, kv_f32_s) = _rope_linear(
        kvn_p, kvn_s, _Weight(w["w_kv"][None], 0), tab_p, tab_s, n=2 * kv_w, rope_cols=kv_w,
        out_dtypes=(BF16, F32), rows_per_seq=seq, name="kv_proj")
    (q_p,), (q_s,) = _rope_linear(
        hn_p, hn_s, _Weight(w["w_attn_q"], 0), tab_p, tab_s, n=q_w, rope_cols=q_w,
        out_dtypes=(BF16,), rows_per_seq=seq, name="q_proj")
    sinks = w["attn_sinks"][0]
    att_p = _swa_attention(q_p, kv_bf_p, sinks, batch, seq)
    kv4 = kv_f32_p.reshape(batch, seq, 2, A_KV_HEADS, A_HEAD_DIM)
    k_win_p, v_win_p = kv4[:, -WINDOW:, 0], kv4[:, -WINDOW:, 1]
    att_s, k_win_s, v_win_s = _decode_attention(
        q_s, kv_f32_s[:, :kv_w], kv_f32_s[:, kv_w:], k_win0, v_win0, sinks)
    hp, hs = _linear(att_p, att_s, _Weight(w["w_attn_o"], 0), col0=0, n=D_MODEL, out_dtype=F32,
                     residual=hp, residual2=hs, name="attn_out")
    (hp, conv_b_p), (hs, conv_b_s) = _conv_ffn(
        hp, hs, w["g_ffn"][1], _Weight(w["w_ffn_up"], 1), w["ffn_conv_w"][1], w["ffn_conv_b"][1],
        _Weight(w["w_ffn_down"], 1), conv0_p[1], conv0_s[1], batch, seq)
    g_final = w["g_final"].reshape(1, D_MODEL)
    y_p, = _rms_norm(hp, g_final, F32)
    y_s, = _rms_norm(hs, g_final, F32)
    out_p = (y_p.reshape(batch, seq, D_MODEL), c_p[None], n_p[None], m_p[None], k_win_p, v_win_p,
             jnp.stack([conv_a_p, conv_b_p]))
    out_s = (y_s.reshape(batch2, 1, D_MODEL), c_s[None], n_s[None], m_s[None], k_win_s, v_win_s,
             jnp.stack([conv_a_s, conv_b_s]))
    return out_p, out_s


def kernel(x_prompt, x_sample, state_mlstm_C, state_mlstm_n, state_mlstm_m, cache_k_win, cache_v_win,
           state_ffn_conv, g_mix, g_ffn, w_mlstm_in, b_mlstm_gates, w_mlstm_out, g_kv, w_kv, w_attn_q,
           attn_sinks, w_attn_o, w_ffn_up, ffn_conv_w, ffn_conv_b, w_ffn_down, g_final):
    w = dict(g_mix=g_mix, g_ffn=g_ffn, wt_mlstm_in=jnp.swapaxes(w_mlstm_in, 1, 2),
             b_mlstm_gates=b_mlstm_gates,
             w_mlstm_out=w_mlstm_out, g_kv=g_kv, w_kv=w_kv, w_attn_q=w_attn_q, attn_sinks=attn_sinks,
             w_attn_o=w_attn_o, w_ffn_up=w_ffn_up, ffn_conv_w=ffn_conv_w, ffn_conv_b=ffn_conv_b,
             w_ffn_down=w_ffn_down, g_final=g_final)
    bp = x_prompt.shape[0]
    n_a = state_mlstm_C.shape[0]
    depth = state_ffn_conv.shape[0]
    state_p = (jnp.zeros((n_a, bp, M_HEADS, M_DK, M_DV), F32), jnp.zeros((n_a, bp, M_HEADS, M_DK), F32),
               jnp.zeros((n_a, bp, M_HEADS), F32), jnp.zeros((depth, bp, CONV_W - 1, D_FF), F32))
    state_s = (state_mlstm_C, state_mlstm_n, state_mlstm_m, state_ffn_conv)
    prompt, sample = _model(x_prompt, x_sample, state_p, state_s, cache_k_win, cache_v_win, w)
    return (prompt[0], sample[0]) + prompt[1:] + sample[1:]
```

```python
import functools
import math
from typing import NamedTuple

import jax
import jax.numpy as jnp
from jax import lax
from jax.experimental import pallas as pl
from jax.experimental.pallas import tpu as pltpu

F32 = jnp.float32
BF16 = jnp.bfloat16

D_MODEL = 4096
PAST_LEN = 16384
M_HEADS = 8
M_DK = D_MODEL // (2 * M_HEADS)
M_DV = D_MODEL // M_HEADS
M_CHUNK = 128
GATE_SOFT_CAP = 15.0
A_HEADS = 32
A_KV_HEADS = 8
A_GROUP = A_HEADS // A_KV_HEADS
A_HEAD_DIM = D_MODEL // A_HEADS
WINDOW = 128
ROPE_THETA = 10000.0
NEG_INF = -1e30
D_FF = 11008
CONV_W = 3
RMS_EPS = 1e-6

V7X_VMEM_BYTES = 64 * 1024 * 1024
VMEM_LIMIT_BYTES = V7X_VMEM_BYTES - 8 * 1024 * 1024
LANES = 128
SUBLANES = 8

MM_BLOCK_M = 1024
MM_BLOCK_M_F32_OUT = 512
MM_BLOCK_N = 1024
FFN_BLOCK_N = 256
FFN_UP_ROW_PIECES = 4
DOWN_BLOCK_M = 512
DOWN_BLOCK_N = 512
RMS_BLOCK_M = 512
DECODE_ATTN_BLOCK_B = 4
MLSTM_HEAD_GROUP = 4
CAST_ROWS = 256


def _params(sem):
    return pltpu.CompilerParams(dimension_semantics=sem, vmem_limit_bytes=VMEM_LIMIT_BYTES)


def _rms_kernel(x_ref, g_ref, *o_refs):
    xf = x_ref[...]
    r = lax.rsqrt(jnp.mean(xf * xf, axis=-1, keepdims=True) + RMS_EPS)
    xr = xf * r
    for k, o_ref in enumerate(o_refs):
        o_ref[...] = (xr * g_ref[k:k + 1, :]).astype(o_ref.dtype)


def _rms_norm(x, gains, out_dtype):
    m, d = x.shape
    n_g = gains.shape[0]
    bm = min(RMS_BLOCK_M, m)
    assert m % bm == 0
    outs = pl.pallas_call(
        _rms_kernel,
        grid=(m // bm,),
        in_specs=[pl.BlockSpec((bm, d), lambda i: (i, 0)),
                  pl.BlockSpec((n_g, d), lambda i: (0, 0))],
        out_specs=[pl.BlockSpec((bm, d), lambda i: (i, 0)) for _ in range(n_g)],
        out_shape=[jax.ShapeDtypeStruct((m, d), out_dtype) for _ in range(n_g)],
        compiler_params=_params(("arbitrary",)),
        name="rms_norm",
    )(x, gains)
    return outs


def _cast_weight(w_ref, wbf_ref, col0, ncols, transposed):
    k = wbf_ref.shape[0]
    rows_per = min(CAST_ROWS, k)
    assert k % rows_per == 0

    def body(r, carry):
        rows = pl.ds(pl.multiple_of(r * rows_per, rows_per), rows_per)
        if transposed:
            wbf_ref[rows, col0:col0 + ncols] = w_ref[:, rows].astype(BF16).T
        else:
            wbf_ref[rows, col0:col0 + ncols] = w_ref[rows, :].astype(BF16)
        return carry

    lax.fori_loop(0, k // rows_per, body, 0)


class _Prefetch(NamedTuple):
    col_blocks: tuple
    layer: int
    bn: int


def _mm_kernel(*refs, n_w, n_w_refs, counts, epilogue, cast, transposed, prefetch, n_i, n_j, delayed):
    n_extra, n_extra2, n_out, n_out2 = counts
    x_ref, x2_ref = refs[0], refs[1]
    refs = refs[2:]
    w_refs, refs = refs[:n_w_refs], refs[n_w_refs:]
    extra, refs = refs[:n_extra], refs[n_extra:]
    extra2, refs = refs[:n_extra2], refs[n_extra2:]
    outs, refs = refs[:n_out], refs[n_out:]
    outs2, scratch = refs[:n_out2], refs[n_out2:]
    bm = x_ref.shape[0]
    if delayed:
        n_tiles = n_j * n_i
        s = pl.program_id(0)
        cur = jnp.minimum(s, n_tiles - 1)
        prev = jnp.maximum(s - 1, 0)
        j, i = cur // n_i, cur % n_i
        j_prev, i_prev = prev // n_i, prev % n_i
    else:
        j = pl.program_id(0)
        i = pl.program_id(1)
    if cast:
        wbf_ref = scratch[0]
        scratch = scratch[1:]
    if prefetch is not None:
        w_hbm = w_refs[0]
        stage_ref, sem_ref = scratch[0], scratch[1]
        scratch = scratch[2:]
        w_refs = [stage_ref.at[t] for t in range(n_w)]

        def slab_copy(t, jj):
            start = pl.multiple_of(prefetch.col_blocks[t](jj) * prefetch.bn, prefetch.bn)
            cols = pl.ds(start, prefetch.bn)
            src = w_hbm.at[prefetch.layer, cols, :] if transposed else w_hbm.at[prefetch.layer, :, cols]
            return pltpu.make_async_copy(src, stage_ref.at[t], sem_ref.at[t])

    if delayed:
        acc_ref, acc2_ref = scratch[0], scratch[1]
        scratch = scratch[2:]
    w_ref = wbf_ref if cast else w_refs[0]

    def load_weights():
        if prefetch is not None:
            @pl.when(j == 0)
            def _():
                for t in range(n_w):
                    slab_copy(t, j).start()

            for t in range(n_w):
                slab_copy(t, j).wait()
        if cast:
            bn = wbf_ref.shape[1] // n_w
            for t, wt_ref in enumerate(w_refs):
                _cast_weight(wt_ref, wbf_ref, t * bn, bn, transposed)
        if prefetch is not None:
            @pl.when(j + 1 < n_j)
            def _():
                for t in range(n_w):
                    slab_copy(t, j + 1).start()

    def tile_dot(with_rows2):
        rows = x_ref[...]
        if with_rows2:
            rows = jnp.concatenate([rows, x2_ref[...]], axis=0)
        acc = jnp.dot(rows, w_ref[...], preferred_element_type=F32)
        return acc[:bm], acc[bm:]

    if not delayed:
        @pl.when(i == 0)
        def _():
            epilogue.init_scratch(scratch)
            load_weights()
            acc, acc2 = tile_dot(True)
            epilogue.tile(acc, i, j, extra, outs, scratch)
            epilogue.rows(acc2, j, extra, extra2, outs2)

        @pl.when(i != 0)
        def _():
            acc, _ = tile_dot(False)
            epilogue.tile(acc, i, j, extra, outs, scratch)

        return

    n_pieces = delayed
    cm = bm // n_pieces

    def step(multiply, rows2_now, rows2_prev):
        if rows2_prev:
            epilogue.rows(acc2_ref[...], j_prev, extra, extra2, outs2)
        for p in range(n_pieces):
            rows = slice(p * cm, (p + 1) * cm)
            epilogue.piece(p, n_pieces, acc_ref[rows, :], i_prev, j_prev, extra, outs, scratch)
            if multiply:
                lhs = x_ref[rows, :]
                last = p == n_pieces - 1
                if rows2_now and last:
                    lhs = jnp.concatenate([lhs, x2_ref[...]], axis=0)
                acc = jnp.dot(lhs, w_ref[...], preferred_element_type=F32)
                acc_ref[rows, :] = acc[:cm]
                if rows2_now and last:
                    acc2_ref[...] = acc[cm:]

    @pl.when(s == 0)
    def _():
        acc_ref[...] = jnp.zeros_like(acc_ref)
        epilogue.init_scratch(scratch)

    @pl.when((s < n_tiles) & (i == 0))
    def _():
        load_weights()
        step(True, True, False)

    @pl.when((s < n_tiles) & (i == 1))
    def _():
        step(True, False, True)

    @pl.when((s < n_tiles) & (i > 1))
    def _():
        step(True, False, False)

    @pl.when(s == n_tiles)
    def _():
        step(False, False, False)


class _Epilogue:
    def scratch_shapes(self, bm, bn):
        return []

    def init_scratch(self, scratch):
        pass

    def tile(self, acc, i, j, extra, outs, scratch):
        self.piece(0, 1, acc, i, j, extra, outs, scratch)

    def piece(self, p, n_pieces, acc, i, j, extra, outs, scratch):
        raise NotImplementedError


class _Weight(NamedTuple):
    arr: jax.Array
    layer: int = 0
    transposed: bool = False

    @property
    def k(self):
        return self.arr.shape[2 if self.transposed else 1]


def _matmul(x, x2, w, epilogue, *, w_col_blocks, bn, bm=MM_BLOCK_M, extra=(), extra_specs=(),
            extra2=(), extra2_specs=(), out_shapes, out_specs, out2_shapes, out2_specs, name,
            prefetch=False, delayed=0):
    m, k = x.shape
    m2 = x2.shape[0]
    bm = min(bm, m)
    assert m % bm == 0 and w.k == k and x2.shape[1] == k
    cast = w.arr.dtype != BF16
    assert cast or not w.transposed
    n_w = len(w_col_blocks)
    n_j = epilogue.n_col_steps
    n_i = m // bm
    if delayed and (n_i < 2 or bm % (delayed * 2 * SUBLANES) != 0):
        delayed = 0
    x_specs = [pl.BlockSpec((bm, k), lambda j, i: (i, 0)),
               pl.BlockSpec((m2, k), lambda j, i: (0, 0))]
    slab_shape = (bn, k) if w.transposed else (k, bn)
    if prefetch:
        assert cast
        w_specs = [pl.BlockSpec(memory_space=pl.ANY)]
        w_args = [w.arr]
    else:
        w_specs = []
        for f in w_col_blocks:
            if w.transposed:
                index_map = functools.partial(lambda j, i, f: (w.layer, f(j), 0), f=f)
            else:
                index_map = functools.partial(lambda j, i, f: (w.layer, 0, f(j)), f=f)
            w_specs.append(pl.BlockSpec((None,) + slab_shape, index_map))
        w_args = [w.arr] * n_w
    late_specs = list(extra_specs) + list(extra2_specs)
    all_out_specs = list(out_specs) + list(out2_specs)
    if delayed:
        n_tiles = n_j * n_i

        def at_step(spec, lag):
            if spec.block_shape is None:
                return spec

            def index_map(s):
                t = jnp.clip(s - lag, 0, n_tiles - 1)
                return spec.index_map(t // n_i, t % n_i)
            return pl.BlockSpec(spec.block_shape, index_map)

        x_specs = [at_step(sp, 0) for sp in x_specs]
        w_specs = [at_step(sp, 0) for sp in w_specs]
        late_specs = [at_step(sp, 1) for sp in late_specs]
        all_out_specs = [at_step(sp, 1) for sp in all_out_specs]
        grid, semantics = (n_tiles + 1,), ("arbitrary",)
    else:
        grid, semantics = (n_j, n_i), ("arbitrary", "arbitrary")
    scratch = []
    if cast:
        scratch.append(pltpu.VMEM((k, n_w * bn), BF16))
    if prefetch:
        scratch.append(pltpu.VMEM((n_w,) + slab_shape, F32))
        scratch.append(pltpu.SemaphoreType.DMA((n_w,)))
    if delayed:
        scratch.append(pltpu.VMEM((bm, n_w * bn), F32))
        scratch.append(pltpu.VMEM((m2, n_w * bn), F32))
    scratch.extend(epilogue.scratch_shapes(bm, bn))
    counts = (len(extra), len(extra2), len(out_shapes), len(out2_shapes))
    kern = functools.partial(
        _mm_kernel, n_w=n_w, n_w_refs=len(w_args), counts=counts, epilogue=epilogue, cast=cast,
        transposed=w.transposed,
        prefetch=_Prefetch(tuple(w_col_blocks), w.layer, bn) if prefetch else None,
        n_i=n_i, n_j=n_j, delayed=delayed)
    res = pl.pallas_call(
        kern,
        grid=grid,
        in_specs=x_specs + w_specs + late_specs,
        out_specs=all_out_specs,
        out_shape=list(out_shapes) + list(out2_shapes),
        scratch_shapes=scratch,
        compiler_params=_params(semantics),
        name=name,
    )(x, x2, *w_args, *extra, *extra2)
    return res[:len(out_shapes)], res[len(out_shapes):]


class _StoreEpilogue(_Epilogue):
    def __init__(self, n_col_steps, scaled_col_steps=0, scale=1.0, residual=False):
        self.n_col_steps = n_col_steps
        self.scaled_col_steps = scaled_col_steps
        self.scale = scale
        self.residual = residual

    def _store(self, acc, j, res_refs, out_ref, rows):
        if self.scaled_col_steps:
            acc = acc * jnp.where(j < self.scaled_col_steps, self.scale, 1.0).astype(F32)
        if self.residual:
            acc = res_refs[0][rows, :] + acc
        out_ref[rows, :] = acc.astype(out_ref.dtype)

    def piece(self, p, n_pieces, acc, i, j, extra, outs, scratch):
        rows = slice(p * acc.shape[0], (p + 1) * acc.shape[0])
        self._store(acc, j, extra, outs[0], rows)

    def rows(self, acc, j, extra, extra2, outs2):
        self._store(acc, j, extra2, outs2[0], slice(None))


def _linear(x, x2, w, *, col0, n, out_dtype, bn=MM_BLOCK_N, scaled_cols=0, scale=1.0,
            residual=None, residual2=None, bm=None, delayed=0, name):
    m, m2 = x.shape[0], x2.shape[0]
    if bm is None:
        bm = MM_BLOCK_M if out_dtype == BF16 else MM_BLOCK_M_F32_OUT
    bm = min(bm, m)
    assert n % bn == 0 and col0 % bn == 0 and scaled_cols % bn == 0
    assert (residual is None) == (residual2 is None)
    j0 = col0 // bn
    ep = _StoreEpilogue(n // bn, scaled_cols // bn, scale, residual is not None)
    tile_spec = pl.BlockSpec((bm, bn), lambda j, i: (i, j))
    rows_spec = pl.BlockSpec((m2, bn), lambda j, i: (0, j))
    has_res = residual is not None
    (out,), (out2,) = _matmul(
        x, x2, w, ep, w_col_blocks=[lambda j: j + j0], bn=bn, bm=bm,
        extra=(residual,) if has_res else (), extra_specs=(tile_spec,) if has_res else (),
        extra2=(residual2,) if has_res else (), extra2_specs=(rows_spec,) if has_res else (),
        out_shapes=[jax.ShapeDtypeStruct((m, n), out_dtype)], out_specs=[tile_spec],
        out2_shapes=[jax.ShapeDtypeStruct((m2, n), out_dtype)], out2_specs=[rows_spec],
        name=name, prefetch=w.arr.dtype != BF16, delayed=delayed)
    return out, out2


class _RopeEpilogue(_Epilogue):
    def __init__(self, n_col_steps, rope_col_steps, n_out):
        self.n_col_steps = n_col_steps
        self.rope_col_steps = rope_col_steps
        self.n_out = n_out

    def _store(self, acc, j, cos, sin, outs):
        def rotated():
            heads = []
            for h in range(acc.shape[1] // A_HEAD_DIM):
                xh = acc[:, h * A_HEAD_DIM:(h + 1) * A_HEAD_DIM]
                heads.append(xh * cos + pltpu.roll(xh, A_HEAD_DIM // 2, axis=1) * sin)
            return jnp.concatenate(heads, axis=1)

        if self.rope_col_steps >= self.n_col_steps:
            val = rotated()
        else:
            val = jnp.where(j < self.rope_col_steps, rotated(), acc)
        for o_ref in outs:
            o_ref[...] = val.astype(o_ref.dtype)

    def tile(self, acc, i, j, extra, outs, scratch):
        cos_ref, sin_ref = extra
        bm = acc.shape[0]
        rows = pl.ds(pl.multiple_of((i % (cos_ref.shape[0] // bm)) * bm, bm), bm)
        self._store(acc, j, cos_ref[rows, :], sin_ref[rows, :], outs)

    def rows(self, acc, j, extra, extra2, outs2):
        self._store(acc, j, extra2[0][...], extra2[1][...], outs2)


def _rope_linear(x, x2, w, tables, tables2, *, n, rope_cols, out_dtypes, rows_per_seq, name,
                 bn=MM_BLOCK_N):
    m, m2 = x.shape[0], x2.shape[0]
    bm = min(MM_BLOCK_M_F32_OUT, m, rows_per_seq)
    assert rows_per_seq % bm == 0 and n % bn == 0 and rope_cols % bn == 0
    tiles_per_seq = rows_per_seq // bm
    ep = _RopeEpilogue(n // bn, rope_cols // bn, len(out_dtypes))
    tab_spec = pl.BlockSpec((rows_per_seq, A_HEAD_DIM), lambda j, i: (0, 0))
    tab2_spec = pl.BlockSpec((m2, A_HEAD_DIM), lambda j, i: (0, 0))
    return _matmul(
        x, x2, w, ep, w_col_blocks=[lambda j: j], bn=bn, bm=bm,
        extra=tables, extra_specs=(tab_spec, tab_spec),
        extra2=tables2, extra2_specs=(tab2_spec, tab2_spec),
        out_shapes=[jax.ShapeDtypeStruct((m, n), dt) for dt in out_dtypes],
        out_specs=[pl.BlockSpec((bm, bn), lambda j, i: (i, j)) for _ in out_dtypes],
        out2_shapes=[jax.ShapeDtypeStruct((m2, n), dt) for dt in out_dtypes],
        out2_specs=[pl.BlockSpec((m2, bn), lambda j, i: (0, j)) for _ in out_dtypes],
        name=name, prefetch=True)


def _norm_gates_kernel(x_ref, g_ref, wt_ref, b_ref, xn_ref, gates_ref):
    xf = x_ref[...]
    r = lax.rsqrt(jnp.mean(xf * xf, axis=-1, keepdims=True) + RMS_EPS)
    xn = (xf * r * g_ref[...]).astype(BF16)
    xn_ref[...] = xn
    g = lax.dot_general(xn, wt_ref[...].astype(BF16), (((1,), (1,)), ((), ())),
                        preferred_element_type=F32) + b_ref[...]
    ig = GATE_SOFT_CAP * jnp.tanh(g / GATE_SOFT_CAP)
    lf = jnp.minimum(g, 0.0) - jnp.log1p(jnp.exp(-jnp.abs(g)))
    lane = lax.broadcasted_iota(jnp.int32, g.shape, 1)
    gates_ref[...] = jnp.where(lane < M_HEADS, ig, lf)


def _rms_norm_gates(x, gain, wt_gates, b_gates):
    m, d = x.shape
    bm = min(RMS_BLOCK_M, m)
    ng = 2 * M_HEADS
    assert m % bm == 0
    return pl.pallas_call(
        _norm_gates_kernel,
        grid=(m // bm,),
        in_specs=[pl.BlockSpec((bm, d), lambda i: (i, 0)),
                  pl.BlockSpec((1, d), lambda i: (0, 0)),
                  pl.BlockSpec((ng, d), lambda i: (0, 0)),
                  pl.BlockSpec((1, ng), lambda i: (0, 0))],
        out_specs=[pl.BlockSpec((bm, d), lambda i: (i, 0)),
                   pl.BlockSpec((bm, ng), lambda i: (i, 0))],
        out_shape=[jax.ShapeDtypeStruct((m, d), BF16), jax.ShapeDtypeStruct((m, ng), F32)],
        compiler_params=_params(("arbitrary",)),
        name="rms_norm_gates",
    )(x, gain, wt_gates, b_gates.reshape(1, ng).astype(F32))


def _mlstm_chunk_kernel(q_ref, k_ref, v_ref, o_ref, gcol_ref, grow_ref, c0_ref, n0_ref, m0_ref,
                        h_ref, c_ref, n_ref, m_ref):
    L = q_ref.shape[0]

    @pl.when(pl.program_id(1) == 0)
    def _():
        c_ref[...] = c0_ref[...]
        n_ref[...] = n0_ref[...]
        m_ref[...] = m0_ref[...]

    t_idx = lax.broadcasted_iota(jnp.int32, (L, L), 0)
    s_idx = lax.broadcasted_iota(jnp.int32, (L, L), 1)
    causal = s_idx <= t_idx
    qs = [q_ref[:, h * M_DK:(h + 1) * M_DK] for h in range(M_HEADS)]
    ks = [k_ref[:, h * M_DK:(h + 1) * M_DK] for h in range(M_HEADS)]
    vs = [v_ref[:, h * M_DV:(h + 1) * M_DV] for h in range(M_HEADS)]

    def group_phases(heads):
        cums, dlogs, gw, scores, invs, kws = {}, {}, {}, {}, {}, {}

        def cumulative_gates():
            for h in heads:
                lf_col = gcol_ref[:, M_HEADS + h:M_HEADS + h + 1]
                lf_row = grow_ref[M_HEADS + h:M_HEADS + h + 1, :]
                b_col = jnp.sum(jnp.where(causal, lf_row, 0.0), axis=1, keepdims=True)
                b_row = jnp.sum(jnp.where(t_idx <= s_idx, lf_col, 0.0), axis=0, keepdims=True)
                cums[h] = (b_col, b_row)

        def log_weights():
            for h in heads:
                b_col, b_row = cums[h]
                ig_row = grow_ref[h:h + 1, :]
                m_prev = m_ref[0, h:h + 1, 0:1]
                dlog = jnp.where(causal, b_col - b_row + ig_row, -jnp.inf)
                g = b_col + m_prev
                m_t = jnp.maximum(g, jnp.max(dlog, axis=1, keepdims=True))
                dlogs[h] = (dlog, g, m_t)

        def weights():
            for h in heads:
                b_col, _ = cums[h]
                dlog, g, m_t = dlogs[h]
                ig_col = gcol_ref[:, h:h + 1]
                w_intra = jnp.exp(dlog - m_t)
                w_inter = jnp.exp(g - m_t)
                b_last = b_col[L - 1:L, :]
                m_last = m_t[L - 1:L, :]
                w_end = jnp.exp(b_last - b_col + ig_col - m_last)
                gw[h] = (w_intra, w_inter, m_t, w_end)

        def intra_scores():
            for h in heads:
                qk = lax.dot_general(qs[h], ks[h], (((1,), (1,)), ((), ())),
                                     preferred_element_type=F32)
                scores[h] = qk * gw[h][0]

        def denominators():
            for h in heads:
                _, w_inter, m_t, _ = gw[h]
                n_prev = n_ref[0, h:h + 1, :]
                qn = jnp.sum(qs[h].astype(F32) * n_prev, axis=1, keepdims=True)
                den = jnp.sum(scores[h], axis=1, keepdims=True) + w_inter * qn
                invs[h] = 1.0 / jnp.maximum(jnp.abs(den), jnp.exp(-m_t))

        def outputs():
            for h in heads:
                w_inter = gw[h][1]
                c_prev = c_ref[0, h]
                num = (jnp.dot(scores[h].astype(BF16), vs[h], preferred_element_type=F32)
                       + w_inter * jnp.dot(qs[h], c_prev.astype(BF16), preferred_element_type=F32))
                gate = jax.nn.sigmoid(o_ref[:, h * M_DV:(h + 1) * M_DV])
                h_ref[:, h * M_DV:(h + 1) * M_DV] = (gate * (num * invs[h])).astype(h_ref.dtype)

        def normaliser_state():
            for h in heads:
                _, w_inter, m_t, w_end = gw[h]
                decay = w_inter[L - 1:L, :]
                kw = ks[h].astype(F32) * w_end
                n_ref[0, h:h + 1, :] = decay * n_ref[0, h:h + 1, :] + jnp.sum(kw, axis=0, keepdims=True)
                m_ref[0, h:h + 1, :] = jnp.broadcast_to(m_t[L - 1:L, :], (1, LANES))
                kws[h] = kw.astype(BF16)

        def memory_state():
            for h in heads:
                decay = gw[h][1][L - 1:L, :]
                c_ref[0, h] = decay * c_ref[0, h] + lax.dot_general(
                    kws[h], vs[h], (((0,), (0,)), ((), ())), preferred_element_type=F32)

        return ([cumulative_gates, log_weights, weights], [intra_scores, denominators, outputs],
                [normaliser_state, memory_state])

    for g0 in range(0, M_HEADS, MLSTM_HEAD_GROUP):
        for stages in group_phases(range(g0, g0 + MLSTM_HEAD_GROUP)):
            for stage in stages:
                stage()


def _mlstm_chunks(qkv, o_pre, gates, c0, n0, m0, batch, seq):
    L = M_CHUNK
    assert seq % L == 0
    nc = seq // L
    qk_w = M_HEADS * M_DK
    v_w = M_HEADS * M_DV
    m_rows = batch * seq
    grow = gates.T
    m0b = jnp.broadcast_to(m0[:, :, None], (batch, M_HEADS, LANES))
    row = lambda b, c: b * nc + c
    hg, c_new, n_new, m_new = pl.pallas_call(
        _mlstm_chunk_kernel,
        grid=(batch, nc),
        in_specs=[
            pl.BlockSpec((L, qk_w), lambda b, c: (row(b, c), 0)),
            pl.BlockSpec((L, qk_w), lambda b, c: (row(b, c), 1)),
            pl.BlockSpec((L, v_w), lambda b, c: (row(b, c), 1)),
            pl.BlockSpec((L, v_w), lambda b, c: (row(b, c), 0)),
            pl.BlockSpec((L, 2 * M_HEADS), lambda b, c: (row(b, c), 0)),
            pl.BlockSpec((2 * M_HEADS, L), lambda b, c: (0, row(b, c))),
            pl.BlockSpec((1, M_HEADS, M_DK, M_DV), lambda b, c: (b, 0, 0, 0)),
            pl.BlockSpec((1, M_HEADS, M_DK), lambda b, c: (b, 0, 0)),
            pl.BlockSpec((1, M_HEADS, LANES), lambda b, c: (b, 0, 0)),
        ],
        out_specs=[
            pl.BlockSpec((L, v_w), lambda b, c: (row(b, c), 0)),
            pl.BlockSpec((1, M_HEADS, M_DK, M_DV), lambda b, c: (b, 0, 0, 0)),
            pl.BlockSpec((1, M_HEADS, M_DK), lambda b, c: (b, 0, 0)),
            pl.BlockSpec((1, M_HEADS, LANES), lambda b, c: (b, 0, 0)),
        ],
        out_shape=[
            jax.ShapeDtypeStruct((m_rows, v_w), BF16),
            jax.ShapeDtypeStruct((batch, M_HEADS, M_DK, M_DV), F32),
            jax.ShapeDtypeStruct((batch, M_HEADS, M_DK), F32),
            jax.ShapeDtypeStruct((batch, M_HEADS, LANES), F32),
        ],
        compiler_params=_params(("arbitrary", "arbitrary")),
        name="mlstm_chunks",
    )(qkv, qkv, qkv, o_pre, gates, grow, c0, n0, m0b)
    return hg, c_new, n_new, m_new[:, :, 0]


def _mlstm_step_kernel(q_ref, k_ref, v_ref, o_ref, g_ref, c0_ref, n0_ref, m0_ref,
                       h_ref, c_ref, n_ref, m_ref):
    dk_idx = lax.broadcasted_iota(jnp.int32, (M_DK, M_DK), 0)
    dk_lane = lax.broadcasted_iota(jnp.int32, (M_DK, M_DK), 1)
    eye = dk_idx == dk_lane
    m_rows = []
    for h in range(M_HEADS):
        q = q_ref[0, :, h * M_DK:(h + 1) * M_DK]
        k = k_ref[0, :, h * M_DK:(h + 1) * M_DK]
        v = v_ref[0, :, h * M_DV:(h + 1) * M_DV]
        ig = g_ref[0, :, h:h + 1]
        lf = g_ref[0, :, M_HEADS + h:M_HEADS + h + 1]
        c_prev = c0_ref[0, h]
        n_prev = n0_ref[0, h:h + 1, :]
        m_prev = m0_ref[0, h:h + 1, 0:1]
        qf, kf, vf = q.astype(F32), k.astype(F32), v.astype(F32)

        g = lf + m_prev
        m_t = jnp.maximum(g, ig)
        w_intra = jnp.exp(ig - m_t)
        w_inter = jnp.exp(g - m_t)
        s = jnp.sum(qf * kf, axis=1, keepdims=True) * w_intra
        q_rows = jnp.broadcast_to(q, (2 * SUBLANES, M_DK))
        qc = jnp.dot(q_rows, c_prev.astype(BF16), preferred_element_type=F32)[0:1, :]
        num = s * vf + w_inter * qc
        den = s + w_inter * jnp.sum(qf * n_prev, axis=1, keepdims=True)
        hh = num / jnp.maximum(jnp.abs(den), jnp.exp(-m_t))
        gate = jax.nn.sigmoid(o_ref[0, :, h * M_DV:(h + 1) * M_DV])
        h_ref[0, :, h * M_DV:(h + 1) * M_DV] = (gate * hh).astype(h_ref.dtype)

        kw = kf * w_intra
        kw_col = jnp.sum(jnp.where(eye, kw, 0.0), axis=1, keepdims=True)
        c_ref[0, h] = w_inter * c_prev + kw_col * vf
        n_ref[0, h:h + 1, :] = w_inter * n_prev + kw
        m_rows.append(jnp.broadcast_to(m_t, (1, LANES)))
    m_ref[0] = jnp.concatenate(m_rows, axis=0)


def _mlstm_step(qkv, o_pre, gates, c0, n0, m0):
    batch = qkv.shape[0]
    qk_w = M_HEADS * M_DK
    v_w = M_HEADS * M_DV
    qkv3 = qkv.reshape(batch, 1, 2 * qk_w + v_w)
    m0b = jnp.broadcast_to(m0[:, :, None], (batch, M_HEADS, LANES))
    hg, c_new, n_new, m_new = pl.pallas_call(
        _mlstm_step_kernel,
        grid=(batch,),
        in_specs=[
            pl.BlockSpec((1, 1, qk_w), lambda b: (b, 0, 0)),
            pl.BlockSpec((1, 1, qk_w), lambda b: (b, 0, 1)),
            pl.BlockSpec((1, 1, v_w), lambda b: (b, 0, 1)),
            pl.BlockSpec((1, 1, v_w), lambda b: (b, 0, 0)),
            pl.BlockSpec((1, 1, 2 * M_HEADS), lambda b: (b, 0, 0)),
            pl.BlockSpec((1, M_HEADS, M_DK, M_DV), lambda b: (b, 0, 0, 0)),
            pl.BlockSpec((1, M_HEADS, M_DK), lambda b: (b, 0, 0)),
            pl.BlockSpec((1, M_HEADS, LANES), lambda b: (b, 0, 0)),
        ],
        out_specs=[
            pl.BlockSpec((1, 1, v_w), lambda b: (b, 0, 0)),
            pl.BlockSpec((1, M_HEADS, M_DK, M_DV), lambda b: (b, 0, 0, 0)),
            pl.BlockSpec((1, M_HEADS, M_DK), lambda b: (b, 0, 0)),
            pl.BlockSpec((1, M_HEADS, LANES), lambda b: (b, 0, 0)),
        ],
        out_shape=[
            jax.ShapeDtypeStruct((batch, 1, v_w), BF16),
            jax.ShapeDtypeStruct((batch, M_HEADS, M_DK, M_DV), F32),
            jax.ShapeDtypeStruct((batch, M_HEADS, M_DK), F32),
            jax.ShapeDtypeStruct((batch, M_HEADS, LANES), F32),
        ],
        compiler_params=_params(("arbitrary",)),
        name="mlstm_step",
    )(qkv3, qkv3, qkv3, o_pre.reshape(batch, 1, v_w), gates.reshape(batch, 1, 2 * M_HEADS),
      c0, n0, m0b)
    return hg.reshape(batch, v_w), c_new, n_new, m_new[:, :, 0]


def _swa_kernel(sink_ref, q_ref, kp_ref, ko_ref, vp_ref, vo_ref, o_ref):
    W = WINDOW
    D = A_HEAD_DIM
    blk = pl.program_id(1)
    i_idx = lax.broadcasted_iota(jnp.int32, (W, 2 * W), 0)
    j_idx = lax.broadcasted_iota(jnp.int32, (W, 2 * W), 1)
    first_key = jnp.where(blk > 0, 0, W)
    mask = (j_idx >= jnp.maximum(i_idx, first_key)) & (j_idx <= i_idx + W)
    def head_scores(kv):
        cols = slice(kv * D, (kv + 1) * D)
        k_cat = jnp.concatenate([kp_ref[:, cols], ko_ref[:, cols]], axis=0)
        q = jnp.concatenate([q_ref[:, (kv * A_GROUP + g) * D:(kv * A_GROUP + g + 1) * D]
                             for g in range(A_GROUP)], axis=0)
        sc = lax.dot_general(q, k_cat, (((1,), (1,)), ((), ())), preferred_element_type=F32)
        return sc * (A_HEAD_DIM ** -0.5)

    next_scores = head_scores(0)
    for kv in range(A_KV_HEADS):
        cols = slice(kv * D, (kv + 1) * D)
        v_cat = jnp.concatenate([vp_ref[:, cols], vo_ref[:, cols]], axis=0)
        scores = next_scores
        if kv + 1 < A_KV_HEADS:
            next_scores = head_scores(kv + 1)
        sinks = [sink_ref[kv * A_GROUP + g] for g in range(A_GROUP)]
        masked = [jnp.where(mask, scores[g * W:(g + 1) * W, :], NEG_INF) for g in range(A_GROUP)]
        mx = [jnp.maximum(jnp.max(masked[g], axis=-1, keepdims=True), sinks[g]) for g in range(A_GROUP)]
        e = [jnp.exp(masked[g] - mx[g]) for g in range(A_GROUP)]
        den = [jnp.sum(e[g], axis=-1, keepdims=True) + jnp.exp(sinks[g] - mx[g]) for g in range(A_GROUP)]
        p = jnp.concatenate([(e[g] / den[g]).astype(BF16) for g in range(A_GROUP)], axis=0)
        o = jnp.dot(p, v_cat, preferred_element_type=F32)
        for g in range(A_GROUP):
            hq = kv * A_GROUP + g
            o_ref[:, hq * D:(hq + 1) * D] = o[g * W:(g + 1) * W, :].astype(o_ref.dtype)


def _swa_attention(q, kv, sinks, batch, seq):
    W = WINDOW
    assert seq % W == 0
    nb = seq // W
    kvw = A_KV_HEADS * A_HEAD_DIM
    qw = A_HEADS * A_HEAD_DIM
    row = lambda b, n: b * nb + n
    prev = lambda b, n: jnp.maximum(b * nb + n - 1, 0)
    return pl.pallas_call(
        _swa_kernel,
        grid=(batch, nb),
        in_specs=[
            pl.BlockSpec(memory_space=pltpu.SMEM),
            pl.BlockSpec((W, qw), lambda b, n: (row(b, n), 0)),
            pl.BlockSpec((W, kvw), lambda b, n: (prev(b, n), 0)),
            pl.BlockSpec((W, kvw), lambda b, n: (row(b, n), 0)),
            pl.BlockSpec((W, kvw), lambda b, n: (prev(b, n), 1)),
            pl.BlockSpec((W, kvw), lambda b, n: (row(b, n), 1)),
        ],
        out_specs=pl.BlockSpec((W, qw), lambda b, n: (row(b, n), 0)),
        out_shape=jax.ShapeDtypeStruct((batch * seq, qw), BF16),
        compiler_params=_params(("arbitrary", "arbitrary")),
        name="swa_attention",
    )(sinks.astype(F32), q, kv, kv, kv, kv)


def _decode_attn_kernel(sink_ref, q_ref, kn_ref, vn_ref, ck_ref, cv_ref, o_ref, kw_ref, vw_ref):
    W, D, KV, H = WINDOW, A_HEAD_DIM, A_KV_HEADS, A_HEADS
    scale = A_HEAD_DIM ** -0.5
    head_kv = lax.broadcasted_iota(jnp.int32, (H, W * KV), 0) // A_GROUP
    own_c = (lax.broadcasted_iota(jnp.int32, (H, W * KV), 1) % KV) == head_kv
    own_n = lax.broadcasted_iota(jnp.int32, (H, KV), 1) == head_kv[:, :KV]
    sink = sink_ref[:, 0:1]
    nt = (((1,), (1,)), ((), ()))
    for b in range(q_ref.shape[0]):
        q = q_ref[b]
        kc = ck_ref[b].reshape(W * KV, D).astype(BF16)
        vc = cv_ref[b].reshape(W * KV, D).astype(BF16)
        kn = kn_ref[b].astype(BF16)
        vn = vn_ref[b].astype(BF16)
        s_c = lax.dot_general(q, kc, nt, preferred_element_type=F32) * scale
        s_n = lax.dot_general(q, kn, nt, preferred_element_type=F32) * scale
        s_c = jnp.where(own_c, s_c, NEG_INF)
        s_n = jnp.where(own_n, s_n, NEG_INF)
        mx = jnp.maximum(jnp.maximum(jnp.max(s_c, axis=1, keepdims=True),
                                     jnp.max(s_n, axis=1, keepdims=True)), sink)
        e_c = jnp.exp(s_c - mx)
        e_n = jnp.exp(s_n - mx)
        den = (jnp.sum(e_c, axis=1, keepdims=True) + jnp.sum(e_n, axis=1, keepdims=True)
               + jnp.exp(sink - mx))
        o = (jnp.dot((e_c / den).astype(BF16), vc, preferred_element_type=F32)
             + jnp.dot((e_n / den).astype(BF16), vn, preferred_element_type=F32))
        o_ref[b] = o.astype(o_ref.dtype)
        kw_ref[b, 0:W - 1] = ck_ref[b, 1:W]
        kw_ref[b, W - 1] = kn_ref[b]
        vw_ref[b, 0:W - 1] = cv_ref[b, 1:W]
        vw_ref[b, W - 1] = vn_ref[b]


def _decode_attention(q, k_new, v_new, cache_k, cache_v, sinks):
    batch = q.shape[0]
    bb = math.gcd(batch, DECODE_ATTN_BLOCK_B)
    H, KV, D, W = A_HEADS, A_KV_HEADS, A_HEAD_DIM, WINDOW
    new_spec = pl.BlockSpec((bb, KV, D), lambda b: (b, 0, 0))
    win_spec = pl.BlockSpec((bb, W, KV, D), lambda b: (b, 0, 0, 0))
    o, k_win, v_win = pl.pallas_call(
        _decode_attn_kernel,
        grid=(batch // bb,),
        in_specs=[pl.BlockSpec((H, LANES), lambda b: (0, 0)),
                  pl.BlockSpec((bb, H, D), lambda b: (b, 0, 0)),
                  new_spec, new_spec, win_spec, win_spec],
        out_specs=[pl.BlockSpec((bb, H, D), lambda b: (b, 0, 0)), win_spec, win_spec],
        out_shape=[jax.ShapeDtypeStruct((batch, H, D), BF16),
                   jax.ShapeDtypeStruct((batch, W, KV, D), F32),
                   jax.ShapeDtypeStruct((batch, W, KV, D), F32)],
        compiler_params=_params(("arbitrary",)),
        name="decode_attention",
    )(jnp.broadcast_to(sinks.astype(F32)[:, None], (H, LANES)), q.reshape(batch, H, D),
      k_new.reshape(batch, KV, D), v_new.reshape(batch, KV, D), cache_k, cache_v)
    return o.reshape(batch, H * D), k_win, v_win


def _conv_act(g2, g1, g0, val, cw_ref, cb_ref):
    conv = cb_ref[...] + cw_ref[0:1, :] * g2
    conv = conv + cw_ref[1:2, :] * g1
    conv = conv + cw_ref[2:3, :] * g0
    return conv * jax.nn.sigmoid(conv) * val


class _ConvFfnEpilogue(_Epilogue):
    def __init__(self, n_col_steps, tiles_per_seq):
        self.n_col_steps = n_col_steps
        self.tiles_per_seq = tiles_per_seq

    def scratch_shapes(self, bm, bn):
        return [pltpu.VMEM((SUBLANES, bn), F32)]

    def init_scratch(self, scratch):
        scratch[0][...] = jnp.zeros_like(scratch[0])

    def rows(self, acc, j, extra, extra2, outs2):
        cw_ref, cb_ref = extra[:2]
        g2_ref, g1_ref, wd_ref = extra2
        act_ref, gate_ref, wd_bf_ref = outs2
        wd_bf_ref[...] = wd_ref[...].astype(wd_bf_ref.dtype)
        bn = acc.shape[1] // 2
        gate, val = acc[:, :bn], acc[:, bn:]
        act_ref[...] = _conv_act(g2_ref[...], g1_ref[...], gate, val, cw_ref, cb_ref).astype(act_ref.dtype)
        gate_ref[...] = gate

    def piece(self, p, n_pieces, acc, i, j, extra, outs, scratch):
        cw_ref, cb_ref, halo0_ref = extra
        act_ref, tail_ref = outs
        carry_ref, = scratch
        rows = acc.shape[0]
        bn = acc.shape[1] // 2
        gate, val = acc[:, :bn], acc[:, bn:]
        halo = carry_ref[...]
        if p == 0:
            seq_start = (i % self.tiles_per_seq) == 0
            halo = jnp.where(seq_start, halo0_ref[i // self.tiles_per_seq], halo)
        row = lax.broadcasted_iota(jnp.int32, (SUBLANES, bn), 0)

        def shifted(d):
            body = pltpu.roll(gate, d, axis=0)
            top = jnp.where(row < d, pltpu.roll(halo, d, axis=0), body[:SUBLANES])
            return jnp.concatenate([top, body[SUBLANES:]], axis=0)

        g1, g2 = shifted(1), shifted(2)
        act = _conv_act(g2, g1, gate, val, cw_ref, cb_ref).astype(act_ref.dtype)
        act_ref[p * rows:(p + 1) * rows, :] = act
        tail = gate[rows - SUBLANES:, :]
        carry_ref[...] = tail
        if p == n_pieces - 1:
            tail_ref[i] = tail


def _conv_ffn_up(xn, xn2, w_up, w_down, conv_w, conv_b, conv0, conv0_2, batch, seq):
    m, m2 = xn.shape[0], xn2.shape[0]
    bn = FFN_BLOCK_N
    assert D_FF % bn == 0
    n_j = D_FF // bn
    cw = conv_w.astype(F32)
    cb = conv_b.reshape(1, D_FF).astype(F32)
    col_spec = lambda rows: pl.BlockSpec((rows, bn), lambda j, i: (0, j))
    bm = min(MM_BLOCK_M, seq)
    assert seq % bm == 0 and bm % SUBLANES == 0
    tiles_per_seq = seq // bm
    n_i = m // bm
    halo0 = jnp.concatenate(
        [jnp.zeros((batch, SUBLANES - (CONV_W - 1), D_FF), F32), conv0.astype(F32)], axis=1)
    g2, g1 = conv0_2[:, 0, :], conv0_2[:, 1, :]
    rows_spec = pl.BlockSpec((m2, bn), lambda j, i: (0, j))
    k_down, n_down = w_down.arr.shape[1:]
    wd_rows = k_down // n_j
    assert wd_rows * n_j == k_down and wd_rows % (2 * SUBLANES) == 0 and not w_down.transposed
    (act, tails), (act2, gate2, wd_bf16) = _matmul(
        xn, xn2, w_up, _ConvFfnEpilogue(n_j, tiles_per_seq),
        w_col_blocks=[lambda j: j, lambda j: j + n_j], bn=bn, bm=bm,
        extra=(cw, cb, halo0),
        extra_specs=(col_spec(CONV_W), col_spec(1),
                     pl.BlockSpec((batch, SUBLANES, bn), lambda j, i: (0, 0, j))),
        extra2=(g2, g1, w_down.arr),
        extra2_specs=(rows_spec, rows_spec,
                      pl.BlockSpec((None, wd_rows, n_down), lambda j, i: (w_down.layer, j, 0))),
        out_shapes=[jax.ShapeDtypeStruct((m, D_FF), BF16),
                    jax.ShapeDtypeStruct((n_i, SUBLANES, D_FF), F32)],
        out_specs=[pl.BlockSpec((bm, bn), lambda j, i: (i, j)),
                   pl.BlockSpec((n_i, SUBLANES, bn), lambda j, i: (0, 0, j))],
        out2_shapes=[jax.ShapeDtypeStruct((m2, D_FF), BF16), jax.ShapeDtypeStruct((m2, D_FF), F32),
                     jax.ShapeDtypeStruct((k_down, n_down), BF16)],
        out2_specs=[rows_spec, rows_spec, pl.BlockSpec((wd_rows, n_down), lambda j, i: (j, 0))],
        name="conv_ffn_up", prefetch=True, delayed=FFN_UP_ROW_PIECES)
    last_tiles = tails.reshape(batch, tiles_per_seq, SUBLANES, D_FF)[:, -1]
    return ((act, last_tiles[:, SUBLANES - (CONV_W - 1):, :]),
            (act2, jnp.stack([g1, gate2], axis=1)), wd_bf16)


def _conv_ffn(h, h2, gain, w_up, conv_w, conv_b, w_down, conv0, conv0_2, batch, seq):
    gain = gain.reshape(1, D_MODEL)
    xn, = _rms_norm(h, gain, BF16)
    xn2, = _rms_norm(h2, gain, BF16)
    (act, conv_new), (act2, conv_new2), wd_bf16 = _conv_ffn_up(
        xn, xn2, w_up, w_down, conv_w, conv_b, conv0, conv0_2, batch, seq)
    out, out2 = _linear(act, act2, _Weight(wd_bf16[None], 0), col0=0, n=D_MODEL, out_dtype=F32,
                        bn=DOWN_BLOCK_N, bm=DOWN_BLOCK_M, residual=h, residual2=h2, name="ffn_down")
    return (out, conv_new), (out2, conv_new2)


def _rope_tables(pos):
    inv_freq = ROPE_THETA ** (-jnp.arange(0, A_HEAD_DIM, 2, dtype=F32) / A_HEAD_DIM)
    ang = pos.astype(F32)[:, None] * inv_freq[None, :]
    cos, sin = jnp.cos(ang), jnp.sin(ang)
    return jnp.concatenate([cos, cos], axis=1), jnp.concatenate([-sin, sin], axis=1)


def _model(x_p, x_s, state_p, state_s, k_win0, v_win0, w):
    batch, seq, _ = x_p.shape
    batch2 = x_s.shape[0]
    assert x_s.shape[1] == 1
    hp = x_p.reshape(batch * seq, D_MODEL)
    hs = x_s.reshape(batch2, D_MODEL)
    c0_p, n0_p, m0_p, conv0_p = state_p
    c0_s, n0_s, m0_s, conv0_s = state_s
    qk_w = M_HEADS * M_DK
    v_w = M_HEADS * M_DV
    kv_w = A_KV_HEADS * A_HEAD_DIM
    q_w = A_HEADS * A_HEAD_DIM

    g_mix0 = w["g_mix"][0:1]
    wt_gates = w["wt_mlstm_in"][0, 2 * qk_w + 2 * v_w:, :]
    hn_p, gates_p = _rms_norm_gates(hp, g_mix0, wt_gates, w["b_mlstm_gates"][0])
    hn_s, gates_s = _rms_norm_gates(hs, g_mix0, wt_gates, w["b_mlstm_gates"][0])
    wt_in = _Weight(w["wt_mlstm_in"], 0, transposed=True)
    qkv_p, qkv_s = _linear(hn_p, hn_s, wt_in, col0=0, n=2 * qk_w + v_w, out_dtype=BF16,
                           scaled_cols=qk_w, scale=M_DK ** -0.5, name="mlstm_qkv")
    o_p, o_s = _linear(hn_p, hn_s, wt_in, col0=2 * qk_w + v_w, n=v_w, out_dtype=F32, name="mlstm_ogate")
    hg_p, c_p, n_p, m_p = _mlstm_chunks(qkv_p, o_p, gates_p, c0_p[0], n0_p[0], m0_p[0], batch, seq)
    hg_s, c_s, n_s, m_s = _mlstm_step(qkv_s, o_s, gates_s, c0_s[0], n0_s[0], m0_s[0])
    hp, hs = _linear(hg_p, hg_s, _Weight(w["w_mlstm_out"], 0), col0=0, n=D_MODEL, out_dtype=F32,
                     residual=hp, residual2=hs, name="mlstm_out")
    (hp, conv_a_p), (hs, conv_a_s) = _conv_ffn(
        hp, hs, w["g_ffn"][0], _Weight(w["w_ffn_up"], 0), w["ffn_conv_w"][0], w["ffn_conv_b"][0],
        _Weight(w["w_ffn_down"], 0), conv0_p[0], conv0_s[0], batch, seq)

    g_kv_mix = jnp.stack([w["g_kv"], w["g_mix"][1]])
    kvn_p, hn_p = _rms_norm(hp, g_kv_mix, BF16)
    kvn_s, hn_s = _rms_norm(hs, g_kv_mix, BF16)
    tab_p = _rope_tables(jnp.arange(seq))
    tab_s = tuple(jnp.broadcast_to(t, (batch2, A_HEAD_DIM)) for t in _rope_tables(PAST_LEN + jnp.arange(1)))
    (kv_bf_p, kv_f32_p), (_, kv_f32_s) = _rope_linear(
        kvn_p, kvn_s, _Weight(w["w_kv"][None], 0), tab_p, tab_s, n=2 * kv_w, rope_cols=kv_w,
        out_dtypes=(BF16, F32), rows_per_seq=seq, name="kv_proj")
    (q_p,), (q_s,) = _rope_linear(
        hn_p, hn_s, _Weight(w["w_attn_q"], 0), tab_p, tab_s, n=q_w, rope_cols=q_w,
        out_dtypes=(BF16,), rows_per_seq=seq, name="q_proj")
    sinks = w["attn_sinks"][0]
    att_p = _swa_attention(q_p, kv_bf_p, sinks, batch, seq)
    kv4 = kv_f32_p.reshape(batch, seq, 2, A_KV_HEADS, A_HEAD_DIM)
    k_win_p, v_win_p = kv4[:, -WINDOW:, 0], kv4[:, -WINDOW:, 1]
    att_s, k_win_s, v_win_s = _decode_attention(
        q_s, kv_f32_s[:, :kv_w], kv_f32_s[:, kv_w:], k_win0, v_win0, sinks)
    hp, hs = _linear(att_p, att_s, _Weight(w["w_attn_o"], 0), col0=0, n=D_MODEL, out_dtype=F32,
                     residual=hp, residual2=hs, name="attn_out")
    (hp, conv_b_p), (hs, conv_b_s) = _conv_ffn(
        hp, hs, w["g_ffn"][1], _Weight(w["w_ffn_up"], 1), w["ffn_conv_w"][1], w["ffn_conv_b"][1],
        _Weight(w["w_ffn_down"], 1), conv0_p[1], conv0_s[1], batch, seq)
    g_final = w["g_final"].reshape(1, D_MODEL)
    y_p, = _rms_norm(hp, g_final, F32)
    y_s, = _rms_norm(hs, g_final, F32)
    out_p = (y_p.reshape(batch, seq, D_MODEL), c_p[None], n_p[None], m_p[None], k_win_p, v_win_p,
             jnp.stack([conv_a_p, conv_b_p]))
    out_s = (y_s.reshape(batch2, 1, D_MODEL), c_s[None], n_s[None], m_s[None], k_win_s, v_win_s,
             jnp.stack([conv_a_s, conv_b_s]))
    return out_p, out_s


def kernel(x_prompt, x_sample, state_mlstm_C, state_mlstm_n, state_mlstm_m, cache_k_win, cache_v_win,
           state_ffn_conv, g_mix, g_ffn, w_mlstm_in, b_mlstm_gates, w_mlstm_out, g_kv, w_kv, w_attn_q,
           attn_sinks, w_attn_o, w_ffn_up, ffn_conv_w, ffn_conv_b, w_ffn_down, g_final):
    w = dict(g_mix=g_mix, g_ffn=g_ffn, wt_mlstm_in=jnp.swapaxes(w_mlstm_in, 1, 2),
             b_mlstm_gates=b_mlstm_gates,
             w_mlstm_out=w_mlstm_out, g_kv=g_kv, w_kv=w_kv, w_attn_q=w_attn_q, attn_sinks=attn_sinks,
             w_attn_o=w_attn_o, w_ffn_up=w_ffn_up, ffn_conv_w=ffn_conv_w, ffn_conv_b=ffn_conv_b,
             w_ffn_down=w_ffn_down, g_final=g_final)
    bp = x_prompt.shape[0]
    n_a = state_mlstm_C.shape[0]
    depth = state_ffn_conv.shape[0]
    state_p = (jnp.zeros((n_a, bp, M_HEADS, M_DK, M_DV), F32), jnp.zeros((n_a, bp, M_HEADS, M_DK), F32),
               jnp.zeros((n_a, bp, M_HEADS), F32), jnp.zeros((depth, bp, CONV_W - 1, D_FF), F32))
    state_s = (state_mlstm_C, state_mlstm_n, state_mlstm_m, state_ffn_conv)
    prompt, sample = _model(x_prompt, x_sample, state_p, state_s, cache_k_win, cache_v_win, w)
    return (prompt[0], sample[0]) + prompt[1:] + sample[1:]
```

```python
import functools
import math
from typing import NamedTuple

import jax
import jax.numpy as jnp
from jax import lax
from jax.experimental import pallas as pl
from jax.experimental.pallas import tpu as pltpu

F32 = jnp.float32
BF16 = jnp.bfloat16

D_MODEL = 4096
PAST_LEN = 16384
M_HEADS = 8
M_DK = D_MODEL // (2 * M_HEADS)
M_DV = D_MODEL // M_HEADS
M_CHUNK = 128
GATE_SOFT_CAP = 15.0
A_HEADS = 32
A_KV_HEADS = 8
A_GROUP = A_HEADS // A_KV_HEADS
A_HEAD_DIM = D_MODEL // A_HEADS
WINDOW = 128
ROPE_THETA = 10000.0
NEG_INF = -1e30
D_FF = 11008
CONV_W = 3
RMS_EPS = 1e-6

V7X_VMEM_BYTES = 64 * 1024 * 1024
VMEM_LIMIT_BYTES = V7X_VMEM_BYTES - 8 * 1024 * 1024
LANES = 128
SUBLANES = 8

MM_BLOCK_M = 1024
MM_BLOCK_M_F32_OUT = 512
MM_BLOCK_N = 1024
FFN_BLOCK_N = 256
FFN_UP_ROW_PIECES = 4
X_RING_SLOTS = 3
DOWN_BLOCK_M = 512
DOWN_BLOCK_N = 512
RMS_BLOCK_M = 512
DECODE_ATTN_BLOCK_B = 4
MLSTM_HEAD_GROUP = 4
CAST_ROWS = 256


def _params(sem):
    return pltpu.CompilerParams(dimension_semantics=sem, vmem_limit_bytes=VMEM_LIMIT_BYTES)


def _rms_kernel(x_ref, g_ref, *o_refs):
    xf = x_ref[...]
    r = lax.rsqrt(jnp.mean(xf * xf, axis=-1, keepdims=True) + RMS_EPS)
    xr = xf * r
    for k, o_ref in enumerate(o_refs):
        o_ref[...] = (xr * g_ref[k:k + 1, :]).astype(o_ref.dtype)


def _rms_norm(x, gains, out_dtype):
    m, d = x.shape
    n_g = gains.shape[0]
    bm = min(RMS_BLOCK_M, m)
    assert m % bm == 0
    outs = pl.pallas_call(
        _rms_kernel,
        grid=(m // bm,),
        in_specs=[pl.BlockSpec((bm, d), lambda i: (i, 0)),
                  pl.BlockSpec((n_g, d), lambda i: (0, 0))],
        out_specs=[pl.BlockSpec((bm, d), lambda i: (i, 0)) for _ in range(n_g)],
        out_shape=[jax.ShapeDtypeStruct((m, d), out_dtype) for _ in range(n_g)],
        compiler_params=_params(("arbitrary",)),
        name="rms_norm",
    )(x, gains)
    return outs


def _cast_weight(w_ref, wbf_ref, col0, ncols, transposed):
    k = wbf_ref.shape[0]
    rows_per = min(CAST_ROWS, k)
    assert k % rows_per == 0

    def body(r, carry):
        rows = pl.ds(pl.multiple_of(r * rows_per, rows_per), rows_per)
        if transposed:
            wbf_ref[rows, col0:col0 + ncols] = w_ref[:, rows].astype(BF16).T
        else:
            wbf_ref[rows, col0:col0 + ncols] = w_ref[rows, :].astype(BF16)
        return carry

    lax.fori_loop(0, k // rows_per, body, 0)


class _Prefetch(NamedTuple):
    col_blocks: tuple
    layer: int
    bn: int


def _mm_kernel(*refs, n_w, n_w_refs, counts, epilogue, cast, transposed, prefetch, n_i, n_j, delayed,
               bm, x_ring):
    n_extra, n_extra2, n_out, n_out2 = counts
    x_ref, x2_ref = refs[0], refs[1]
    refs = refs[2:]
    w_refs, refs = refs[:n_w_refs], refs[n_w_refs:]
    extra, refs = refs[:n_extra], refs[n_extra:]
    extra2, refs = refs[:n_extra2], refs[n_extra2:]
    outs, refs = refs[:n_out], refs[n_out:]
    outs2, scratch = refs[:n_out2], refs[n_out2:]
    if delayed:
        n_tiles = n_j * n_i
        s = pl.program_id(0)
        cur = jnp.minimum(s, n_tiles - 1)
        prev = jnp.maximum(s - 1, 0)
        j, i = cur // n_i, cur % n_i
        j_prev, i_prev = prev // n_i, prev % n_i
    else:
        j = pl.program_id(0)
        i = pl.program_id(1)
    if cast:
        wbf_ref = scratch[0]
        scratch = scratch[1:]
    if prefetch is not None:
        w_hbm = w_refs[0]
        stage_ref, sem_ref = scratch[0], scratch[1]
        scratch = scratch[2:]
        w_refs = [stage_ref.at[t] for t in range(n_w)]

        def slab_copy(t, jj):
            start = pl.multiple_of(prefetch.col_blocks[t](jj) * prefetch.bn, prefetch.bn)
            cols = pl.ds(start, prefetch.bn)
            src = w_hbm.at[prefetch.layer, cols, :] if transposed else w_hbm.at[prefetch.layer, :, cols]
            return pltpu.make_async_copy(src, stage_ref.at[t], sem_ref.at[t])

    if delayed:
        acc_ref, acc2_ref = scratch[0], scratch[1]
        scratch = scratch[2:]
    if x_ring:
        x_hbm, ring_ref, ring_sem = x_ref, scratch[0], scratch[1]
        scratch = scratch[2:]
        n_tiles = n_j * n_i
        step = pl.program_id(0) if delayed else j * n_i + i

        def tile_copy(t):
            rows = pl.ds(pl.multiple_of((t % n_i) * bm, bm), bm)
            slot = t % X_RING_SLOTS
            return pltpu.make_async_copy(x_hbm.at[rows, :], ring_ref.at[slot], ring_sem.at[slot])

        @pl.when(step == 0)
        def _():
            for t in range(min(X_RING_SLOTS - 1, n_tiles)):
                tile_copy(t).start()

        @pl.when(step + X_RING_SLOTS - 1 < n_tiles)
        def _():
            tile_copy(step + X_RING_SLOTS - 1).start()

        @pl.when(step < n_tiles)
        def _():
            tile_copy(step).wait()

        x_ref = ring_ref.at[jnp.minimum(step, n_tiles - 1) % X_RING_SLOTS]
    w_ref = wbf_ref if cast else w_refs[0]

    def load_weights():
        if prefetch is not None:
            @pl.when(j == 0)
            def _():
                for t in range(n_w):
                    slab_copy(t, j).start()

            for t in range(n_w):
                slab_copy(t, j).wait()
        if cast:
            bn = wbf_ref.shape[1] // n_w
            for t, wt_ref in enumerate(w_refs):
                _cast_weight(wt_ref, wbf_ref, t * bn, bn, transposed)
        if prefetch is not None:
            @pl.when(j + 1 < n_j)
            def _():
                for t in range(n_w):
                    slab_copy(t, j + 1).start()

    def tile_dot(with_rows2):
        rows = x_ref[...]
        if with_rows2:
            rows = jnp.concatenate([rows, x2_ref[...]], axis=0)
        acc = jnp.dot(rows, w_ref[...], preferred_element_type=F32)
        return acc[:bm], acc[bm:]

    if not delayed:
        @pl.when(i == 0)
        def _():
            epilogue.init_scratch(scratch)
            load_weights()
            acc, acc2 = tile_dot(True)
            epilogue.tile(acc, i, j, extra, outs, scratch)
            epilogue.rows(acc2, j, extra, extra2, outs2)

        @pl.when(i != 0)
        def _():
            acc, _ = tile_dot(False)
            epilogue.tile(acc, i, j, extra, outs, scratch)

        return

    n_pieces = delayed
    cm = bm // n_pieces

    def step(multiply, rows2_now, rows2_prev):
        if rows2_prev:
            epilogue.rows(acc2_ref[...], j_prev, extra, extra2, outs2)
        for p in range(n_pieces):
            rows = slice(p * cm, (p + 1) * cm)
            epilogue.piece(p, n_pieces, acc_ref[rows, :], i_prev, j_prev, extra, outs, scratch)
            if multiply:
                lhs = x_ref[rows, :]
                last = p == n_pieces - 1
                if rows2_now and last:
                    lhs = jnp.concatenate([lhs, x2_ref[...]], axis=0)
                acc = jnp.dot(lhs, w_ref[...], preferred_element_type=F32)
                acc_ref[rows, :] = acc[:cm]
                if rows2_now and last:
                    acc2_ref[...] = acc[cm:]

    @pl.when(s == 0)
    def _():
        acc_ref[...] = jnp.zeros_like(acc_ref)
        epilogue.init_scratch(scratch)

    @pl.when((s < n_tiles) & (i == 0))
    def _():
        load_weights()
        step(True, True, False)

    @pl.when((s < n_tiles) & (i == 1))
    def _():
        step(True, False, True)

    @pl.when((s < n_tiles) & (i > 1))
    def _():
        step(True, False, False)

    @pl.when(s == n_tiles)
    def _():
        step(False, False, False)


class _Epilogue:
    def scratch_shapes(self, bm, bn):
        return []

    def init_scratch(self, scratch):
        pass

    def tile(self, acc, i, j, extra, outs, scratch):
        self.piece(0, 1, acc, i, j, extra, outs, scratch)

    def piece(self, p, n_pieces, acc, i, j, extra, outs, scratch):
        raise NotImplementedError


class _Weight(NamedTuple):
    arr: jax.Array
    layer: int = 0
    transposed: bool = False

    @property
    def k(self):
        return self.arr.shape[2 if self.transposed else 1]


def _matmul(x, x2, w, epilogue, *, w_col_blocks, bn, bm=MM_BLOCK_M, extra=(), extra_specs=(),
            extra2=(), extra2_specs=(), out_shapes, out_specs, out2_shapes, out2_specs, name,
            prefetch=False, delayed=0, x_ring=False):
    m, k = x.shape
    m2 = x2.shape[0]
    bm = min(bm, m)
    assert m % bm == 0 and w.k == k and x2.shape[1] == k
    cast = w.arr.dtype != BF16
    assert cast or not w.transposed
    n_w = len(w_col_blocks)
    n_j = epilogue.n_col_steps
    n_i = m // bm
    if delayed and (n_i < 2 or bm % (delayed * 2 * SUBLANES) != 0):
        delayed = 0
    x_ring = x_ring and n_i * n_j >= X_RING_SLOTS
    x_specs = [pl.BlockSpec(memory_space=pl.ANY) if x_ring else pl.BlockSpec((bm, k), lambda j, i: (i, 0)),
               pl.BlockSpec((m2, k), lambda j, i: (0, 0))]
    slab_shape = (bn, k) if w.transposed else (k, bn)
    if prefetch:
        assert cast
        w_specs = [pl.BlockSpec(memory_space=pl.ANY)]
        w_args = [w.arr]
    else:
        w_specs = []
        for f in w_col_blocks:
            if w.transposed:
                index_map = functools.partial(lambda j, i, f: (w.layer, f(j), 0), f=f)
            else:
                index_map = functools.partial(lambda j, i, f: (w.layer, 0, f(j)), f=f)
            w_specs.append(pl.BlockSpec((None,) + slab_shape, index_map))
        w_args = [w.arr] * n_w
    late_specs = list(extra_specs) + list(extra2_specs)
    all_out_specs = list(out_specs) + list(out2_specs)
    if delayed:
        n_tiles = n_j * n_i

        def at_step(spec, lag):
            if spec.block_shape is None:
                return spec

            def index_map(s):
                t = jnp.clip(s - lag, 0, n_tiles - 1)
                return spec.index_map(t // n_i, t % n_i)
            return pl.BlockSpec(spec.block_shape, index_map)

        x_specs = [at_step(sp, 0) for sp in x_specs]
        w_specs = [at_step(sp, 0) for sp in w_specs]
        late_specs = [at_step(sp, 1) for sp in late_specs]
        all_out_specs = [at_step(sp, 1) for sp in all_out_specs]
        grid, semantics = (n_tiles + 1,), ("arbitrary",)
    else:
        grid, semantics = (n_j, n_i), ("arbitrary", "arbitrary")
    scratch = []
    if cast:
        scratch.append(pltpu.VMEM((k, n_w * bn), BF16))
    if prefetch:
        scratch.append(pltpu.VMEM((n_w,) + slab_shape, F32))
        scratch.append(pltpu.SemaphoreType.DMA((n_w,)))
    if delayed:
        scratch.append(pltpu.VMEM((bm, n_w * bn), F32))
        scratch.append(pltpu.VMEM((m2, n_w * bn), F32))
    if x_ring:
        scratch.append(pltpu.VMEM((X_RING_SLOTS, bm, k), BF16))
        scratch.append(pltpu.SemaphoreType.DMA((X_RING_SLOTS,)))
    scratch.extend(epilogue.scratch_shapes(bm, bn))
    counts = (len(extra), len(extra2), len(out_shapes), len(out2_shapes))
    kern = functools.partial(
        _mm_kernel, n_w=n_w, n_w_refs=len(w_args), counts=counts, epilogue=epilogue, cast=cast,
        transposed=w.transposed,
        prefetch=_Prefetch(tuple(w_col_blocks), w.layer, bn) if prefetch else None,
        n_i=n_i, n_j=n_j, delayed=delayed, bm=bm, x_ring=x_ring)
    res = pl.pallas_call(
        kern,
        grid=grid,
        in_specs=x_specs + w_specs + late_specs,
        out_specs=all_out_specs,
        out_shape=list(out_shapes) + list(out2_shapes),
        scratch_shapes=scratch,
        compiler_params=_params(semantics),
        name=name,
    )(x, x2, *w_args, *extra, *extra2)
    return res[:len(out_shapes)], res[len(out_shapes):]


class _StoreEpilogue(_Epilogue):
    def __init__(self, n_col_steps, scaled_col_steps=0, scale=1.0, residual=False):
        self.n_col_steps = n_col_steps
        self.scaled_col_steps = scaled_col_steps
        self.scale = scale
        self.residual = residual

    def _store(self, acc, j, res_refs, out_ref, rows):
        if self.scaled_col_steps:
            acc = acc * jnp.where(j < self.scaled_col_steps, self.scale, 1.0).astype(F32)
        if self.residual:
            acc = res_refs[0][rows, :] + acc
        out_ref[rows, :] = acc.astype(out_ref.dtype)

    def piece(self, p, n_pieces, acc, i, j, extra, outs, scratch):
        rows = slice(p * acc.shape[0], (p + 1) * acc.shape[0])
        self._store(acc, j, extra, outs[0], rows)

    def rows(self, acc, j, extra, extra2, outs2):
        self._store(acc, j, extra2, outs2[0], slice(None))


def _linear(x, x2, w, *, col0, n, out_dtype, bn=MM_BLOCK_N, scaled_cols=0, scale=1.0,
            residual=None, residual2=None, bm=None, delayed=0, name):
    m, m2 = x.shape[0], x2.shape[0]
    if bm is None:
        bm = MM_BLOCK_M if out_dtype == BF16 else MM_BLOCK_M_F32_OUT
    bm = min(bm, m)
    assert n % bn == 0 and col0 % bn == 0 and scaled_cols % bn == 0
    assert (residual is None) == (residual2 is None)
    j0 = col0 // bn
    ep = _StoreEpilogue(n // bn, scaled_cols // bn, scale, residual is not None)
    tile_spec = pl.BlockSpec((bm, bn), lambda j, i: (i, j))
    rows_spec = pl.BlockSpec((m2, bn), lambda j, i: (0, j))
    has_res = residual is not None
    (out,), (out2,) = _matmul(
        x, x2, w, ep, w_col_blocks=[lambda j: j + j0], bn=bn, bm=bm,
        extra=(residual,) if has_res else (), extra_specs=(tile_spec,) if has_res else (),
        extra2=(residual2,) if has_res else (), extra2_specs=(rows_spec,) if has_res else (),
        out_shapes=[jax.ShapeDtypeStruct((m, n), out_dtype)], out_specs=[tile_spec],
        out2_shapes=[jax.ShapeDtypeStruct((m2, n), out_dtype)], out2_specs=[rows_spec],
        name=name, prefetch=w.arr.dtype != BF16, delayed=delayed,
        x_ring=w.arr.dtype != BF16 and out_dtype != BF16)
    return out, out2


class _RopeEpilogue(_Epilogue):
    def __init__(self, n_col_steps, rope_col_steps, n_out):
        self.n_col_steps = n_col_steps
        self.rope_col_steps = rope_col_steps
        self.n_out = n_out

    def _store(self, acc, j, cos, sin, outs):
        def rotated():
            heads = []
            for h in range(acc.shape[1] // A_HEAD_DIM):
                xh = acc[:, h * A_HEAD_DIM:(h + 1) * A_HEAD_DIM]
                heads.append(xh * cos + pltpu.roll(xh, A_HEAD_DIM // 2, axis=1) * sin)
            return jnp.concatenate(heads, axis=1)

        if self.rope_col_steps >= self.n_col_steps:
            val = rotated()
        else:
            val = jnp.where(j < self.rope_col_steps, rotated(), acc)
        for o_ref in outs:
            o_ref[...] = val.astype(o_ref.dtype)

    def tile(self, acc, i, j, extra, outs, scratch):
        cos_ref, sin_ref = extra
        bm = acc.shape[0]
        rows = pl.ds(pl.multiple_of((i % (cos_ref.shape[0] // bm)) * bm, bm), bm)
        self._store(acc, j, cos_ref[rows, :], sin_ref[rows, :], outs)

    def rows(self, acc, j, extra, extra2, outs2):
        self._store(acc, j, extra2[0][...], extra2[1][...], outs2)


def _rope_linear(x, x2, w, tables, tables2, *, n, rope_cols, out_dtypes, rows_per_seq, name,
                 bn=MM_BLOCK_N):
    m, m2 = x.shape[0], x2.shape[0]
    bm = min(MM_BLOCK_M_F32_OUT, m, rows_per_seq)
    assert rows_per_seq % bm == 0 and n % bn == 0 and rope_cols % bn == 0
    tiles_per_seq = rows_per_seq // bm
    ep = _RopeEpilogue(n // bn, rope_cols // bn, len(out_dtypes))
    tab_spec = pl.BlockSpec((rows_per_seq, A_HEAD_DIM), lambda j, i: (0, 0))
    tab2_spec = pl.BlockSpec((m2, A_HEAD_DIM), lambda j, i: (0, 0))
    return _matmul(
        x, x2, w, ep, w_col_blocks=[lambda j: j], bn=bn, bm=bm,
        extra=tables, extra_specs=(tab_spec, tab_spec),
        extra2=tables2, extra2_specs=(tab2_spec, tab2_spec),
        out_shapes=[jax.ShapeDtypeStruct((m, n), dt) for dt in out_dtypes],
        out_specs=[pl.BlockSpec((bm, bn), lambda j, i: (i, j)) for _ in out_dtypes],
        out2_shapes=[jax.ShapeDtypeStruct((m2, n), dt) for dt in out_dtypes],
        out2_specs=[pl.BlockSpec((m2, bn), lambda j, i: (0, j)) for _ in out_dtypes],
        name=name, prefetch=True, x_ring=True)


def _norm_gates_kernel(x_ref, g_ref, wt_ref, b_ref, xn_ref, gates_ref):
    xf = x_ref[...]
    r = lax.rsqrt(jnp.mean(xf * xf, axis=-1, keepdims=True) + RMS_EPS)
    xn = (xf * r * g_ref[...]).astype(BF16)
    xn_ref[...] = xn
    g = lax.dot_general(xn, wt_ref[...].astype(BF16), (((1,), (1,)), ((), ())),
                        preferred_element_type=F32) + b_ref[...]
    ig = GATE_SOFT_CAP * jnp.tanh(g / GATE_SOFT_CAP)
    lf = jnp.minimum(g, 0.0) - jnp.log1p(jnp.exp(-jnp.abs(g)))
    lane = lax.broadcasted_iota(jnp.int32, g.shape, 1)
    gates_ref[...] = jnp.where(lane < M_HEADS, ig, lf)


def _rms_norm_gates(x, gain, wt_gates, b_gates):
    m, d = x.shape
    bm = min(RMS_BLOCK_M, m)
    ng = 2 * M_HEADS
    assert m % bm == 0
    return pl.pallas_call(
        _norm_gates_kernel,
        grid=(m // bm,),
        in_specs=[pl.BlockSpec((bm, d), lambda i: (i, 0)),
                  pl.BlockSpec((1, d), lambda i: (0, 0)),
                  pl.BlockSpec((ng, d), lambda i: (0, 0)),
                  pl.BlockSpec((1, ng), lambda i: (0, 0))],
        out_specs=[pl.BlockSpec((bm, d), lambda i: (i, 0)),
                   pl.BlockSpec((bm, ng), lambda i: (i, 0))],
        out_shape=[jax.ShapeDtypeStruct((m, d), BF16), jax.ShapeDtypeStruct((m, ng), F32)],
        compiler_params=_params(("arbitrary",)),
        name="rms_norm_gates",
    )(x, gain, wt_gates, b_gates.reshape(1, ng).astype(F32))


def _mlstm_chunk_kernel(q_ref, k_ref, v_ref, o_ref, gcol_ref, grow_ref, c0_ref, n0_ref, m0_ref,
                        h_ref, c_ref, n_ref, m_ref):
    L = q_ref.shape[0]

    @pl.when(pl.program_id(1) == 0)
    def _():
        c_ref[...] = c0_ref[...]
        n_ref[...] = n0_ref[...]
        m_ref[...] = m0_ref[...]

    t_idx = lax.broadcasted_iota(jnp.int32, (L, L), 0)
    s_idx = lax.broadcasted_iota(jnp.int32, (L, L), 1)
    causal = s_idx <= t_idx
    qs = [q_ref[:, h * M_DK:(h + 1) * M_DK] for h in range(M_HEADS)]
    ks = [k_ref[:, h * M_DK:(h + 1) * M_DK] for h in range(M_HEADS)]
    vs = [v_ref[:, h * M_DV:(h + 1) * M_DV] for h in range(M_HEADS)]

    def group_phases(heads):
        cums, dlogs, gw, scores, invs, kws = {}, {}, {}, {}, {}, {}

        def cumulative_gates():
            for h in heads:
                lf_col = gcol_ref[:, M_HEADS + h:M_HEADS + h + 1]
                lf_row = grow_ref[M_HEADS + h:M_HEADS + h + 1, :]
                b_col = jnp.sum(jnp.where(causal, lf_row, 0.0), axis=1, keepdims=True)
                b_row = jnp.sum(jnp.where(t_idx <= s_idx, lf_col, 0.0), axis=0, keepdims=True)
                cums[h] = (b_col, b_row)

        def log_weights():
            for h in heads:
                b_col, b_row = cums[h]
                ig_row = grow_ref[h:h + 1, :]
                m_prev = m_ref[0, h:h + 1, 0:1]
                dlog = jnp.where(causal, b_col - b_row + ig_row, -jnp.inf)
                g = b_col + m_prev
                m_t = jnp.maximum(g, jnp.max(dlog, axis=1, keepdims=True))
                dlogs[h] = (dlog, g, m_t)

        def weights():
            for h in heads:
                b_col, _ = cums[h]
                dlog, g, m_t = dlogs[h]
                ig_col = gcol_ref[:, h:h + 1]
                w_intra = jnp.exp(dlog - m_t)
                w_inter = jnp.exp(g - m_t)
                b_last = b_col[L - 1:L, :]
                m_last = m_t[L - 1:L, :]
                w_end = jnp.exp(b_last - b_col + ig_col - m_last)
                gw[h] = (w_intra, w_inter, m_t, w_end)

        def intra_scores():
            for h in heads:
                qk = lax.dot_general(qs[h], ks[h], (((1,), (1,)), ((), ())),
                                     preferred_element_type=F32)
                scores[h] = qk * gw[h][0]

        def denominators():
            for h in heads:
                _, w_inter, m_t, _ = gw[h]
                n_prev = n_ref[0, h:h + 1, :]
                qn = jnp.sum(qs[h].astype(F32) * n_prev, axis=1, keepdims=True)
                den = jnp.sum(scores[h], axis=1, keepdims=True) + w_inter * qn
                invs[h] = 1.0 / jnp.maximum(jnp.abs(den), jnp.exp(-m_t))

        def outputs():
            for h in heads:
                w_inter = gw[h][1]
                c_prev = c_ref[0, h]
                num = (jnp.dot(scores[h].astype(BF16), vs[h], preferred_element_type=F32)
                       + w_inter * jnp.dot(qs[h], c_prev.astype(BF16), preferred_element_type=F32))
                gate = jax.nn.sigmoid(o_ref[:, h * M_DV:(h + 1) * M_DV])
                h_ref[:, h * M_DV:(h + 1) * M_DV] = (gate * (num * invs[h])).astype(h_ref.dtype)

        def normaliser_state():
            for h in heads:
                _, w_inter, m_t, w_end = gw[h]
                decay = w_inter[L - 1:L, :]
                kw = ks[h].astype(F32) * w_end
                n_ref[0, h:h + 1, :] = decay * n_ref[0, h:h + 1, :] + jnp.sum(kw, axis=0, keepdims=True)
                m_ref[0, h:h + 1, :] = jnp.broadcast_to(m_t[L - 1:L, :], (1, LANES))
                kws[h] = kw.astype(BF16)

        def memory_state():
            for h in heads:
                decay = gw[h][1][L - 1:L, :]
                c_ref[0, h] = decay * c_ref[0, h] + lax.dot_general(
                    kws[h], vs[h], (((0,), (0,)), ((), ())), preferred_element_type=F32)

        return ([cumulative_gates, log_weights, weights], [intra_scores, denominators, outputs],
                [normaliser_state, memory_state])

    for g0 in range(0, M_HEADS, MLSTM_HEAD_GROUP):
        for stages in group_phases(range(g0, g0 + MLSTM_HEAD_GROUP)):
            for stage in stages:
                stage()


def _mlstm_chunks(qkv, o_pre, gates, c0, n0, m0, batch, seq):
    L = M_CHUNK
    assert seq % L == 0
    nc = seq // L
    qk_w = M_HEADS * M_DK
    v_w = M_HEADS * M_DV
    m_rows = batch * seq
    grow = gates.T
    m0b = jnp.broadcast_to(m0[:, :, None], (batch, M_HEADS, LANES))
    row = lambda b, c: b * nc + c
    hg, c_new, n_new, m_new = pl.pallas_call(
        _mlstm_chunk_kernel,
        grid=(batch, nc),
        in_specs=[
            pl.BlockSpec((L, qk_w), lambda b, c: (row(b, c), 0)),
            pl.BlockSpec((L, qk_w), lambda b, c: (row(b, c), 1)),
            pl.BlockSpec((L, v_w), lambda b, c: (row(b, c), 1)),
            pl.BlockSpec((L, v_w), lambda b, c: (row(b, c), 0)),
            pl.BlockSpec((L, 2 * M_HEADS), lambda b, c: (row(b, c), 0)),
            pl.BlockSpec((2 * M_HEADS, L), lambda b, c: (0, row(b, c))),
            pl.BlockSpec((1, M_HEADS, M_DK, M_DV), lambda b, c: (b, 0, 0, 0)),
            pl.BlockSpec((1, M_HEADS, M_DK), lambda b, c: (b, 0, 0)),
            pl.BlockSpec((1, M_HEADS, LANES), lambda b, c: (b, 0, 0)),
        ],
        out_specs=[
            pl.BlockSpec((L, v_w), lambda b, c: (row(b, c), 0)),
            pl.BlockSpec((1, M_HEADS, M_DK, M_DV), lambda b, c: (b, 0, 0, 0)),
            pl.BlockSpec((1, M_HEADS, M_DK), lambda b, c: (b, 0, 0)),
            pl.BlockSpec((1, M_HEADS, LANES), lambda b, c: (b, 0, 0)),
        ],
        out_shape=[
            jax.ShapeDtypeStruct((m_rows, v_w), BF16),
            jax.ShapeDtypeStruct((batch, M_HEADS, M_DK, M_DV), F32),
            jax.ShapeDtypeStruct((batch, M_HEADS, M_DK), F32),
            jax.ShapeDtypeStruct((batch, M_HEADS, LANES), F32),
        ],
        compiler_params=_params(("arbitrary", "arbitrary")),
        name="mlstm_chunks",
    )(qkv, qkv, qkv, o_pre, gates, grow, c0, n0, m0b)
    return hg, c_new, n_new, m_new[:, :, 0]


def _mlstm_step_kernel(q_ref, k_ref, v_ref, o_ref, g_ref, c0_ref, n0_ref, m0_ref,
                       h_ref, c_ref, n_ref, m_ref):
    dk_idx = lax.broadcasted_iota(jnp.int32, (M_DK, M_DK), 0)
    dk_lane = lax.broadcasted_iota(jnp.int32, (M_DK, M_DK), 1)
    eye = dk_idx == dk_lane
    m_rows = []
    for h in range(M_HEADS):
        q = q_ref[0, :, h * M_DK:(h + 1) * M_DK]
        k = k_ref[0, :, h * M_DK:(h + 1) * M_DK]
        v = v_ref[0, :, h * M_DV:(h + 1) * M_DV]
        ig = g_ref[0, :, h:h + 1]
        lf = g_ref[0, :, M_HEADS + h:M_HEADS + h + 1]
        c_prev = c0_ref[0, h]
        n_prev = n0_ref[0, h:h + 1, :]
        m_prev = m0_ref[0, h:h + 1, 0:1]
        qf, kf, vf = q.astype(F32), k.astype(F32), v.astype(F32)

        g = lf + m_prev
        m_t = jnp.maximum(g, ig)
        w_intra = jnp.exp(ig - m_t)
        w_inter = jnp.exp(g - m_t)
        s = jnp.sum(qf * kf, axis=1, keepdims=True) * w_intra
        q_rows = jnp.broadcast_to(q, (2 * SUBLANES, M_DK))
        qc = jnp.dot(q_rows, c_prev.astype(BF16), preferred_element_type=F32)[0:1, :]
        num = s * vf + w_inter * qc
        den = s + w_inter * jnp.sum(qf * n_prev, axis=1, keepdims=True)
        hh = num / jnp.maximum(jnp.abs(den), jnp.exp(-m_t))
        gate = jax.nn.sigmoid(o_ref[0, :, h * M_DV:(h + 1) * M_DV])
        h_ref[0, :, h * M_DV:(h + 1) * M_DV] = (gate * hh).astype(h_ref.dtype)

        kw = kf * w_intra
        kw_col = jnp.sum(jnp.where(eye, kw, 0.0), axis=1, keepdims=True)
        c_ref[0, h] = w_inter * c_prev + kw_col * vf
        n_ref[0, h:h + 1, :] = w_inter * n_prev + kw
        m_rows.append(jnp.broadcast_to(m_t, (1, LANES)))
    m_ref[0] = jnp.concatenate(m_rows, axis=0)


def _mlstm_step(qkv, o_pre, gates, c0, n0, m0):
    batch = qkv.shape[0]
    qk_w = M_HEADS * M_DK
    v_w = M_HEADS * M_DV
    qkv3 = qkv.reshape(batch, 1, 2 * qk_w + v_w)
    m0b = jnp.broadcast_to(m0[:, :, None], (batch, M_HEADS, LANES))
    hg, c_new, n_new, m_new = pl.pallas_call(
        _mlstm_step_kernel,
        grid=(batch,),
        in_specs=[
            pl.BlockSpec((1, 1, qk_w), lambda b: (b, 0, 0)),
            pl.BlockSpec((1, 1, qk_w), lambda b: (b, 0, 1)),
            pl.BlockSpec((1, 1, v_w), lambda b: (b, 0, 1)),
            pl.BlockSpec((1, 1, v_w), lambda b: (b, 0, 0)),
            pl.BlockSpec((1, 1, 2 * M_HEADS), lambda b: (b, 0, 0)),
            pl.BlockSpec((1, M_HEADS, M_DK, M_DV), lambda b: (b, 0, 0, 0)),
            pl.BlockSpec((1, M_HEADS, M_DK), lambda b: (b, 0, 0)),
            pl.BlockSpec((1, M_HEADS, LANES), lambda b: (b, 0, 0)),
        ],
        out_specs=[
            pl.BlockSpec((1, 1, v_w), lambda b: (b, 0, 0)),
            pl.BlockSpec((1, M_HEADS, M_DK, M_DV), lambda b: (b, 0, 0, 0)),
            pl.BlockSpec((1, M_HEADS, M_DK), lambda b: (b, 0, 0)),
            pl.BlockSpec((1, M_HEADS, LANES), lambda b: (b, 0, 0)),
        ],
        out_shape=[
            jax.ShapeDtypeStruct((batch, 1, v_w), BF16),
            jax.ShapeDtypeStruct((batch, M_HEADS, M_DK, M_DV), F32),
            jax.ShapeDtypeStruct((batch, M_HEADS, M_DK), F32),
            jax.ShapeDtypeStruct((batch, M_HEADS, LANES), F32),
        ],
        compiler_params=_params(("arbitrary",)),
        name="mlstm_step",
    )(qkv3, qkv3, qkv3, o_pre.reshape(batch, 1, v_w), gates.reshape(batch, 1, 2 * M_HEADS),
      c0, n0, m0b)
    return hg.reshape(batch, v_w), c_new, n_new, m_new[:, :, 0]


def _swa_kernel(sink_ref, q_ref, kp_ref, ko_ref, vp_ref, vo_ref, o_ref):
    W = WINDOW
    D = A_HEAD_DIM
    blk = pl.program_id(1)
    i_idx = lax.broadcasted_iota(jnp.int32, (W, 2 * W), 0)
    j_idx = lax.broadcasted_iota(jnp.int32, (W, 2 * W), 1)
    first_key = jnp.where(blk > 0, 0, W)
    mask = (j_idx >= jnp.maximum(i_idx, first_key)) & (j_idx <= i_idx + W)
    def head_scores(kv):
        cols = slice(kv * D, (kv + 1) * D)
        k_cat = jnp.concatenate([kp_ref[:, cols], ko_ref[:, cols]], axis=0)
        q = jnp.concatenate([q_ref[:, (kv * A_GROUP + g) * D:(kv * A_GROUP + g + 1) * D]
                             for g in range(A_GROUP)], axis=0)
        sc = lax.dot_general(q, k_cat, (((1,), (1,)), ((), ())), preferred_element_type=F32)
        return sc * (A_HEAD_DIM ** -0.5)

    next_scores = head_scores(0)
    for kv in range(A_KV_HEADS):
        cols = slice(kv * D, (kv + 1) * D)
        v_cat = jnp.concatenate([vp_ref[:, cols], vo_ref[:, cols]], axis=0)
        scores = next_scores
        if kv + 1 < A_KV_HEADS:
            next_scores = head_scores(kv + 1)
        sinks = [sink_ref[kv * A_GROUP + g] for g in range(A_GROUP)]
        masked = [jnp.where(mask, scores[g * W:(g + 1) * W, :], NEG_INF) for g in range(A_GROUP)]
        mx = [jnp.maximum(jnp.max(masked[g], axis=-1, keepdims=True), sinks[g]) for g in range(A_GROUP)]
        e = [jnp.exp(masked[g] - mx[g]) for g in range(A_GROUP)]
        den = [jnp.sum(e[g], axis=-1, keepdims=True) + jnp.exp(sinks[g] - mx[g]) for g in range(A_GROUP)]
        p = jnp.concatenate([(e[g] / den[g]).astype(BF16) for g in range(A_GROUP)], axis=0)
        o = jnp.dot(p, v_cat, preferred_element_type=F32)
        for g in range(A_GROUP):
            hq = kv * A_GROUP + g
            o_ref[:, hq * D:(hq + 1) * D] = o[g * W:(g + 1) * W, :].astype(o_ref.dtype)


def _swa_attention(q, kv, sinks, batch, seq):
    W = WINDOW
    assert seq % W == 0
    nb = seq // W
    kvw = A_KV_HEADS * A_HEAD_DIM
    qw = A_HEADS * A_HEAD_DIM
    row = lambda b, n: b * nb + n
    prev = lambda b, n: jnp.maximum(b * nb + n - 1, 0)
    return pl.pallas_call(
        _swa_kernel,
        grid=(batch, nb),
        in_specs=[
            pl.BlockSpec(memory_space=pltpu.SMEM),
            pl.BlockSpec((W, qw), lambda b, n: (row(b, n), 0)),
            pl.BlockSpec((W, kvw), lambda b, n: (prev(b, n), 0)),
            pl.BlockSpec((W, kvw), lambda b, n: (row(b, n), 0)),
            pl.BlockSpec((W, kvw), lambda b, n: (prev(b, n), 1)),
            pl.BlockSpec((W, kvw), lambda b, n: (row(b, n), 1)),
        ],
        out_specs=pl.BlockSpec((W, qw), lambda b, n: (row(b, n), 0)),
        out_shape=jax.ShapeDtypeStruct((batch * seq, qw), BF16),
        compiler_params=_params(("arbitrary", "arbitrary")),
        name="swa_attention",
    )(sinks.astype(F32), q, kv, kv, kv, kv)


def _decode_attn_kernel(sink_ref, q_ref, kn_ref, vn_ref, ck_ref, cv_ref, o_ref, kw_ref, vw_ref):
    W, D, KV, H = WINDOW, A_HEAD_DIM, A_KV_HEADS, A_HEADS
    scale = A_HEAD_DIM ** -0.5
    head_kv = lax.broadcasted_iota(jnp.int32, (H, W * KV), 0) // A_GROUP
    own_c = (lax.broadcasted_iota(jnp.int32, (H, W * KV), 1) % KV) == head_kv
    own_n = lax.broadcasted_iota(jnp.int32, (H, KV), 1) == head_kv[:, :KV]
    sink = sink_ref[:, 0:1]
    nt = (((1,), (1,)), ((), ()))
    for b in range(q_ref.shape[0]):
        q = q_ref[b]
        kc = ck_ref[b].reshape(W * KV, D).astype(BF16)
        vc = cv_ref[b].reshape(W * KV, D).astype(BF16)
        kn = kn_ref[b].astype(BF16)
        vn = vn_ref[b].astype(BF16)
        s_c = lax.dot_general(q, kc, nt, preferred_element_type=F32) * scale
        s_n = lax.dot_general(q, kn, nt, preferred_element_type=F32) * scale
        s_c = jnp.where(own_c, s_c, NEG_INF)
        s_n = jnp.where(own_n, s_n, NEG_INF)
        mx = jnp.maximum(jnp.maximum(jnp.max(s_c, axis=1, keepdims=True),
                                     jnp.max(s_n, axis=1, keepdims=True)), sink)
        e_c = jnp.exp(s_c - mx)
        e_n = jnp.exp(s_n - mx)
        den = (jnp.sum(e_c, axis=1, keepdims=True) + jnp.sum(e_n, axis=1, keepdims=True)
               + jnp.exp(sink - mx))
        o = (jnp.dot((e_c / den).astype(BF16), vc, preferred_element_type=F32)
             + jnp.dot((e_n / den).astype(BF16), vn, preferred_element_type=F32))
        o_ref[b] = o.astype(o_ref.dtype)
        kw_ref[b, 0:W - 1] = ck_ref[b, 1:W]
        kw_ref[b, W - 1] = kn_ref[b]
        vw_ref[b, 0:W - 1] = cv_ref[b, 1:W]
        vw_ref[b, W - 1] = vn_ref[b]


def _decode_attention(q, k_new, v_new, cache_k, cache_v, sinks):
    batch = q.shape[0]
    bb = math.gcd(batch, DECODE_ATTN_BLOCK_B)
    H, KV, D, W = A_HEADS, A_KV_HEADS, A_HEAD_DIM, WINDOW
    new_spec = pl.BlockSpec((bb, KV, D), lambda b: (b, 0, 0))
    win_spec = pl.BlockSpec((bb, W, KV, D), lambda b: (b, 0, 0, 0))
    o, k_win, v_win = pl.pallas_call(
        _decode_attn_kernel,
        grid=(batch // bb,),
        in_specs=[pl.BlockSpec((H, LANES), lambda b: (0, 0)),
                  pl.BlockSpec((bb, H, D), lambda b: (b, 0, 0)),
                  new_spec, new_spec, win_spec, win_spec],
        out_specs=[pl.BlockSpec((bb, H, D), lambda b: (b, 0, 0)), win_spec, win_spec],
        out_shape=[jax.ShapeDtypeStruct((batch, H, D), BF16),
                   jax.ShapeDtypeStruct((batch, W, KV, D), F32),
                   jax.ShapeDtypeStruct((batch, W, KV, D), F32)],
        compiler_params=_params(("arbitrary",)),
        name="decode_attention",
    )(jnp.broadcast_to(sinks.astype(F32)[:, None], (H, LANES)), q.reshape(batch, H, D),
      k_new.reshape(batch, KV, D), v_new.reshape(batch, KV, D), cache_k, cache_v)
    return o.reshape(batch, H * D), k_win, v_win


def _conv_act(g2, g1, g0, val, cw_ref, cb_ref):
    conv = cb_ref[...] + cw_ref[0:1, :] * g2
    conv = conv + cw_ref[1:2, :] * g1
    conv = conv + cw_ref[2:3, :] * g0
    return conv * jax.nn.sigmoid(conv) * val


class _ConvFfnEpilogue(_Epilogue):
    def __init__(self, n_col_steps, tiles_per_seq):
        self.n_col_steps = n_col_steps
        self.tiles_per_seq = tiles_per_seq

    def scratch_shapes(self, bm, bn):
        return [pltpu.VMEM((SUBLANES, bn), F32)]

    def init_scratch(self, scratch):
        scratch[0][...] = jnp.zeros_like(scratch[0])

    def rows(self, acc, j, extra, extra2, outs2):
        cw_ref, cb_ref = extra[:2]
        g2_ref, g1_ref, wd_ref = extra2
        act_ref, gate_ref, wd_bf_ref = outs2
        wd_bf_ref[...] = wd_ref[...].astype(wd_bf_ref.dtype)
        bn = acc.shape[1] // 2
        gate, val = acc[:, :bn], acc[:, bn:]
        act_ref[...] = _conv_act(g2_ref[...], g1_ref[...], gate, val, cw_ref, cb_ref).astype(act_ref.dtype)
        gate_ref[...] = gate

    def piece(self, p, n_pieces, acc, i, j, extra, outs, scratch):
        cw_ref, cb_ref, halo0_ref = extra
        act_ref, tail_ref = outs
        carry_ref, = scratch
        rows = acc.shape[0]
        bn = acc.shape[1] // 2
        gate, val = acc[:, :bn], acc[:, bn:]
        halo = carry_ref[...]
        if p == 0:
            seq_start = (i % self.tiles_per_seq) == 0
            halo = jnp.where(seq_start, halo0_ref[i // self.tiles_per_seq], halo)
        row = lax.broadcasted_iota(jnp.int32, (SUBLANES, bn), 0)

        def shifted(d):
            body = pltpu.roll(gate, d, axis=0)
            top = jnp.where(row < d, pltpu.roll(halo, d, axis=0), body[:SUBLANES])
            return jnp.concatenate([top, body[SUBLANES:]], axis=0)

        g1, g2 = shifted(1), shifted(2)
        act = _conv_act(g2, g1, gate, val, cw_ref, cb_ref).astype(act_ref.dtype)
        act_ref[p * rows:(p + 1) * rows, :] = act
        tail = gate[rows - SUBLANES:, :]
        carry_ref[...] = tail
        if p == n_pieces - 1:
            tail_ref[i] = tail


def _conv_ffn_up(xn, xn2, w_up, w_down, conv_w, conv_b, conv0, conv0_2, batch, seq):
    m, m2 = xn.shape[0], xn2.shape[0]
    bn = FFN_BLOCK_N
    assert D_FF % bn == 0
    n_j = D_FF // bn
    cw = conv_w.astype(F32)
    cb = conv_b.reshape(1, D_FF).astype(F32)
    col_spec = lambda rows: pl.BlockSpec((rows, bn), lambda j, i: (0, j))
    bm = min(MM_BLOCK_M, seq)
    assert seq % bm == 0 and bm % SUBLANES == 0
    tiles_per_seq = seq // bm
    n_i = m // bm
    halo0 = jnp.concatenate(
        [jnp.zeros((batch, SUBLANES - (CONV_W - 1), D_FF), F32), conv0.astype(F32)], axis=1)
    g2, g1 = conv0_2[:, 0, :], conv0_2[:, 1, :]
    rows_spec = pl.BlockSpec((m2, bn), lambda j, i: (0, j))
    k_down, n_down = w_down.arr.shape[1:]
    wd_rows = k_down // n_j
    assert wd_rows * n_j == k_down and wd_rows % (2 * SUBLANES) == 0 and not w_down.transposed
    (act, tails), (act2, gate2, wd_bf16) = _matmul(
        xn, xn2, w_up, _ConvFfnEpilogue(n_j, tiles_per_seq),
        w_col_blocks=[lambda j: j, lambda j: j + n_j], bn=bn, bm=bm,
        extra=(cw, cb, halo0),
        extra_specs=(col_spec(CONV_W), col_spec(1),
                     pl.BlockSpec((batch, SUBLANES, bn), lambda j, i: (0, 0, j))),
        extra2=(g2, g1, w_down.arr),
        extra2_specs=(rows_spec, rows_spec,
                      pl.BlockSpec((None, wd_rows, n_down), lambda j, i: (w_down.layer, j, 0))),
        out_shapes=[jax.ShapeDtypeStruct((m, D_FF), BF16),
                    jax.ShapeDtypeStruct((n_i, SUBLANES, D_FF), F32)],
        out_specs=[pl.BlockSpec((bm, bn), lambda j, i: (i, j)),
                   pl.BlockSpec((n_i, SUBLANES, bn), lambda j, i: (0, 0, j))],
        out2_shapes=[jax.ShapeDtypeStruct((m2, D_FF), BF16), jax.ShapeDtypeStruct((m2, D_FF), F32),
                     jax.ShapeDtypeStruct((k_down, n_down), BF16)],
        out2_specs=[rows_spec, rows_spec, pl.BlockSpec((wd_rows, n_down), lambda j, i: (j, 0))],
        name="conv_ffn_up", prefetch=True, delayed=FFN_UP_ROW_PIECES, x_ring=True)
    last_tiles = tails.reshape(batch, tiles_per_seq, SUBLANES, D_FF)[:, -1]
    return ((act, last_tiles[:, SUBLANES - (CONV_W - 1):, :]),
            (act2, jnp.stack([g1, gate2], axis=1)), wd_bf16)


def _conv_ffn(h, h2, gain, w_up, conv_w, conv_b, w_down, conv0, conv0_2, batch, seq):
    gain = gain.reshape(1, D_MODEL)
    xn, = _rms_norm(h, gain, BF16)
    xn2, = _rms_norm(h2, gain, BF16)
    (act, conv_new), (act2, conv_new2), wd_bf16 = _conv_ffn_up(
        xn, xn2, w_up, w_down, conv_w, conv_b, conv0, conv0_2, batch, seq)
    out, out2 = _linear(act, act2, _Weight(wd_bf16[None], 0), col0=0, n=D_MODEL, out_dtype=F32,
                        bn=DOWN_BLOCK_N, bm=DOWN_BLOCK_M, residual=h, residual2=h2, name="ffn_down")
    return (out, conv_new), (out2, conv_new2)


def _rope_tables(pos):
    inv_freq = ROPE_THETA ** (-jnp.arange(0, A_HEAD_DIM, 2, dtype=F32) / A_HEAD_DIM)
    ang = pos.astype(F32)[:, None] * inv_freq[None, :]
    cos, sin = jnp.cos(ang), jnp.sin(ang)
    return jnp.concatenate([cos, cos], axis=1), jnp.concatenate([-sin, sin], axis=1)


def _model(x_p, x_s, state_p, state_s, k_win0, v_win0, w):
    batch, seq, _ = x_p.shape
    batch2 = x_s.shape[0]
    assert x_s.shape[1] == 1
    hp = x_p.reshape(batch * seq, D_MODEL)
    hs = x_s.reshape(batch2, D_MODEL)
    c0_p, n0_p, m0_p, conv0_p = state_p
    c0_s, n0_s, m0_s, conv0_s = state_s
    qk_w = M_HEADS * M_DK
    v_w = M_HEADS * M_DV
    kv_w = A_KV_HEADS * A_HEAD_DIM
    q_w = A_HEADS * A_HEAD_DIM

    g_mix0 = w["g_mix"][0:1]
    wt_gates = w["wt_mlstm_in"][0, 2 * qk_w + 2 * v_w:, :]
    hn_p, gates_p = _rms_norm_gates(hp, g_mix0, wt_gates, w["b_mlstm_gates"][0])
    hn_s, gates_s = _rms_norm_gates(hs, g_mix0, wt_gates, w["b_mlstm_gates"][0])
    wt_in = _Weight(w["wt_mlstm_in"], 0, transposed=True)
    qkv_p, qkv_s = _linear(hn_p, hn_s, wt_in, col0=0, n=2 * qk_w + v_w, out_dtype=BF16,
                           scaled_cols=qk_w, scale=M_DK ** -0.5, name="mlstm_qkv")
    o_p, o_s = _linear(hn_p, hn_s, wt_in, col0=2 * qk_w + v_w, n=v_w, out_dtype=F32, name="mlstm_ogate")
    hg_p, c_p, n_p, m_p = _mlstm_chunks(qkv_p, o_p, gates_p, c0_p[0], n0_p[0], m0_p[0], batch, seq)
    hg_s, c_s, n_s, m_s = _mlstm_step(qkv_s, o_s, gates_s, c0_s[0], n0_s[0], m0_s[0])
    hp, hs = _linear(hg_p, hg_s, _Weight(w["w_mlstm_out"], 0), col0=0, n=D_MODEL, out_dtype=F32,
                     residual=hp, residual2=hs, name="mlstm_out")
    (hp, conv_a_p), (hs, conv_a_s) = _conv_ffn(
        hp, hs, w["g_ffn"][0], _Weight(w["w_ffn_up"], 0), w["ffn_conv_w"][0], w["ffn_conv_b"][0],
        _Weight(w["w_ffn_down"], 0), conv0_p[0], conv0_s[0], batch, seq)

    g_kv_mix = jnp.stack([w["g_kv"], w["g_mix"][1]])
    kvn_p, hn_p = _rms_norm(hp, g_kv_mix, BF16)
    kvn_s, hn_s = _rms_norm(hs, g_kv_mix, BF16)
    tab_p = _rope_tables(jnp.arange(seq))
    tab_s = tuple(jnp.broadcast_to(t, (batch2, A_HEAD_DIM)) for t in _rope_tables(PAST_LEN + jnp.arange(1)))
    (kv_bf_p, kv_f32_p), (_, kv_f32_s) = _rope_linear(
        kvn_p, kvn_s, _Weight(w["w_kv"][None], 0), tab_p, tab_s, n=2 * kv_w, rope_cols=kv_w,
        out_dtypes=(BF16, F32), rows_per_seq=seq, name="kv_proj")
    (q_p,), (q_s,) = _rope_linear(
        hn_p, hn_s, _Weight(w["w_attn_q"], 0), tab_p, tab_s, n=q_w, rope_cols=q_w,
        out_dtypes=(BF16,), rows_per_seq=seq, name="q_proj")
    sinks = w["attn_sinks"][0]
    att_p = _swa_attention(q_p, kv_bf_p, sinks, batch, seq)
    kv4 = kv_f32_p.reshape(batch, seq, 2, A_KV_HEADS, A_HEAD_DIM)
    k_win_p, v_win_p = kv4[:, -WINDOW:, 0], kv4[:, -WINDOW:, 1]
    att_s, k_win_s, v_win_s = _decode_attention(
        q_s, kv_f32_s[:, :kv_w], kv_f32_s[:, kv_w:], k_win0, v_win0, sinks)
    hp, hs = _linear(att_p, att_s, _Weight(w["w_attn_o"], 0), col0=0, n=D_MODEL, out_dtype=F32,
                     residual=hp, residual2=hs, name="attn_out")
    (hp, conv_b_p), (hs, conv_b_s) = _conv_ffn(
        hp, hs, w["g_ffn"][1], _Weight(w["w_ffn_up"], 1), w["ffn_conv_w"][1], w["ffn_conv_b"][1],
        _Weight(w["w_ffn_down"], 1), conv0_p[1], conv0_s[1], batch, seq)
    g_final = w["g_final"].reshape(1, D_MODEL)
    y_p, = _rms_norm(hp, g_final, F32)
    y_s, = _rms_norm(hs, g_final, F32)
    out_p = (y_p.reshape(batch, seq, D_MODEL), c_p[None], n_p[None], m_p[None], k_win_p, v_win_p,
             jnp.stack([conv_a_p, conv_b_p]))
    out_s = (y_s.reshape(batch2, 1, D_MODEL), c_s[None], n_s[None], m_s[None], k_win_s, v_win_s,
             jnp.stack([conv_a_s, conv_b_s]))
    return out_p, out_s


def kernel(x_prompt, x_sample, state_mlstm_C, state_mlstm_n, state_mlstm_m, cache_k_win, cache_v_win,
           state_ffn_conv, g_mix, g_ffn, w_mlstm_in, b_mlstm_gates, w_mlstm_out, g_kv, w_kv, w_attn_q,
           attn_sinks, w_attn_o, w_ffn_up, ffn_conv_w, ffn_conv_b, w_ffn_down, g_final):
    w = dict(g_mix=g_mix, g_ffn=g_ffn, wt_mlstm_in=jnp.swapaxes(w_mlstm_in, 1, 2),
             b_mlstm_gates=b_mlstm_gates,
             w_mlstm_out=w_mlstm_out, g_kv=g_kv, w_kv=w_kv, w_attn_q=w_attn_q, attn_sinks=attn_sinks,
             w_attn_o=w_attn_o, w_ffn_up=w_ffn_up, ffn_conv_w=ffn_conv_w, ffn_conv_b=ffn_conv_b,
             w_ffn_down=w_ffn_down, g_final=g_final)
    bp = x_prompt.shape[0]
    n_a = state_mlstm_C.shape[0]
    depth = state_ffn_conv.shape[0]
    state_p = (jnp.zeros((n_a, bp, M_HEADS, M_DK, M_DV), F32), jnp.zeros((n_a, bp, M_HEADS, M_DK), F32),
               jnp.zeros((n_a, bp, M_HEADS), F32), jnp.zeros((depth, bp, CONV_W - 1, D_FF), F32))
    state_s = (state_mlstm_C, state_mlstm_n, state_mlstm_m, state_ffn_conv)
    prompt, sample = _model(x_prompt, x_sample, state_p, state_s, cache_k_win, cache_v_win, w)
    return (prompt[0], sample[0]) + prompt[1:] + sample[1:]
```
